```python
import math
import jax, jax.numpy as jnp
from jax import lax
import numpy as np

D_MODEL = 1024
BATCH = 2
SEQ = 8192
DEPTH = 2

CHUNK = 64
N_META = 16
SSM_WIDTH = D_MODEL // 2
SSM_GROUP = 16
SSM_GROUPS = SSM_WIDTH // SSM_GROUP
SSM_STATE = 64
DT_MIN = 0.001
DT_MAX = 0.1
CONV_WIDTH = D_MODEL // 2
CONV_K = 3
ATTN_HEADS = 8
ATTN_HEAD_DIM = 64
ATTN_VALUE_DIM = 2 * ATTN_HEAD_DIM
ATTN_WIDTH = ATTN_HEADS * ATTN_VALUE_DIM
ROPE_THETA = 10000.0
Q_BLOCK = 128
N_BRANCH = 3
IN_WIDTH = SSM_WIDTH + 3 * CONV_WIDTH + 3 * ATTN_WIDTH
N_EXPERTS = 32
TOP_K = 4
D_FF = D_MODEL
SWIGLU_LIMIT = 7.0
SWIGLU_ALPHA = 1.702
DEEPNORM_ALPHA = (2.0 * DEPTH) ** 0.25
DEEPNORM_BETA = (8.0 * DEPTH) ** -0.25
LN_EPS = 1e-5
RMS_EPS = 1e-5
NEG_INF = -1e30

kernel_name = "hybrid_s5_shortconv_diffattn_moe_deepnorm"


def layer_norm(x, g, b):
    xf = x.astype(jnp.float32)
    mu = xf.mean(-1, keepdims=True)
    var = jnp.square(xf - mu).mean(-1, keepdims=True)
    return ((xf - mu) * lax.rsqrt(var + LN_EPS) * g.astype(jnp.float32) + b.astype(jnp.float32)).astype(x.dtype)


def rope_tables(length):
    pos = jnp.arange(length, dtype=jnp.float32)
    inv = ROPE_THETA ** (-jnp.arange(0, ATTN_HEAD_DIM, 2, dtype=jnp.float32) / ATTN_HEAD_DIM)
    ang = pos[:, None] * inv[None, :]
    ang = jnp.concatenate([ang, ang], axis=-1)
    return jnp.cos(ang), jnp.sin(ang)


def apply_rope(x, cos, sin):
    xf = x.astype(jnp.float32)
    half = ATTN_HEAD_DIM // 2
    rot = jnp.concatenate([-xf[..., half:], xf[..., :half]], axis=-1)
    c = cos[None, :, None, None, :]
    s = sin[None, :, None, None, :]
    return (xf * c + rot * s).astype(x.dtype)


def chunk_ids(length, padded_len):
    p = jnp.arange(padded_len)
    cid = jnp.where(p < N_META, 0, 1 + (p - N_META) // CHUNK)
    return jnp.where(p < length, cid, padded_len)


def _complex_affine_combine(e1, e2):
    a1r, a1i, b1r, b1i = e1
    a2r, a2i, b2r, b2i = e2
    ar = a2r * a1r - a2i * a1i
    ai = a2r * a1i + a2i * a1r
    br = a2r * b1r - a2i * b1i + b2r
    bi = a2r * b1i + a2i * b1r + b2i
    return (ar, ai, br, bi)


def s5_branch(u, lam_re, lam_im, log_dt, b_re, b_im, c_re, c_im, d_skip, w_glu):
    bsz, length, _ = u.shape
    f32 = jnp.float32
    uf = u.astype(f32).reshape(bsz, length, SSM_GROUPS, SSM_GROUP)
    lr = lam_re.astype(f32)
    li = lam_im.astype(f32)
    dt = jnp.exp(log_dt.astype(f32))[:, None]
    mag = jnp.exp(lr * dt)
    ar = mag * jnp.cos(li * dt)
    ai = mag * jnp.sin(li * dt)
    denom = lr * lr + li * li
    nr, ni = ar - 1.0, ai
    coef_r = (nr * lr + ni * li) / denom
    coef_i = (ni * lr - nr * li) / denom
    br, bi = b_re.astype(f32), b_im.astype(f32)
    bbar_r = coef_r[..., None] * br - coef_i[..., None] * bi
    bbar_i = coef_r[..., None] * bi + coef_i[..., None] * br
    bu_r = jnp.einsum('blgc,gpc->blgp', uf, bbar_r)
    bu_i = jnp.einsum('blgc,gpc->blgp', uf, bbar_i)
    a_r = jnp.broadcast_to(ar, bu_r.shape)
    a_i = jnp.broadcast_to(ai, bu_i.shape)
    _, _, xr, xi = lax.associative_scan(_complex_affine_combine, (a_r, a_i, bu_r, bu_i), axis=1)
    y = (jnp.einsum('gcp,blgp->blgc', c_re.astype(f32), xr)
         - jnp.einsum('gcp,blgp->blgc', c_im.astype(f32), xi)
         + d_skip.astype(f32).reshape(SSM_GROUPS, SSM_GROUP) * uf)
    y = jax.nn.gelu(y.reshape(bsz, length, SSM_WIDTH))
    y = y * jax.nn.sigmoid(y @ w_glu.astype(f32))
    return y.astype(u.dtype)


def short_conv_branch(gate_b, gate_c, h, conv_w):
    z = gate_c * h
    length = z.shape[1]
    zp = jnp.pad(z, ((0, 0), (CONV_K - 1, 0), (0, 0)))
    y = zp[:, 0:length] * conv_w[0]
    for j in range(1, CONV_K):
        y = y + zp[:, j:j + length] * conv_w[j]
    return gate_b * y


def diff_attention(q, k, v, lam, lam_init, subln_g):
    bsz, length = q.shape[0], q.shape[1]
    padded = -(-length // Q_BLOCK) * Q_BLOCK
    n_blocks = padded // Q_BLOCK
    pad = padded - length
    q = jnp.pad(q, ((0, 0), (0, pad), (0, 0), (0, 0), (0, 0)))
    k = jnp.pad(k, ((0, 0), (0, pad), (0, 0), (0, 0), (0, 0)))
    v = jnp.pad(v, ((0, 0), (0, pad), (0, 0), (0, 0)))
    cid = chunk_ids(length, padded)
    q_blocks = q.reshape(bsz, n_blocks, Q_BLOCK, ATTN_HEADS, 2, ATTN_HEAD_DIM).transpose(1, 0, 2, 3, 4, 5)
    cid_blocks = cid.reshape(n_blocks, Q_BLOCK)
    scale = ATTN_HEAD_DIM ** -0.5

    def one_block(args):
        qi, ci = args
        s = jnp.einsum('bqhcd,bkhcd->bchqk', qi, k).astype(jnp.float32) * scale
        mask = ci[:, None] >= cid[None, :]
        s = jnp.where(mask, s, NEG_INF)
        p = jax.nn.softmax(s, axis=-1)
        p = p[:, 0] - lam * p[:, 1]
        return jnp.einsum('bhqk,bkhe->bqhe', p.astype(v.dtype), v)

    o = lax.map(one_block, (q_blocks, cid_blocks))
    o = o.transpose(1, 0, 2, 3, 4).reshape(bsz, padded, ATTN_HEADS, ATTN_VALUE_DIM)[:, :length]
    of = o.astype(jnp.float32)
    of = of * lax.rsqrt(jnp.mean(of * of, axis=-1, keepdims=True) + RMS_EPS)
    of = of * subln_g.astype(jnp.float32) * (1.0 - lam_init)
    return of.reshape(bsz, length, ATTN_WIDTH).astype(q.dtype)


def moe_ffn(h, router_w, router_b, w_gu, b_gu, w_down, b_down):
    bsz, length, d = h.shape
    xt = h.reshape(bsz * length, d)
    logits = (xt @ router_w + router_b).astype(jnp.float32)
    top_v, top_i = lax.top_k(logits, TOP_K)
    top_w = jax.nn.softmax(top_v, axis=-1)
    combine = jnp.einsum('tk,tke->te', top_w, jax.nn.one_hot(top_i, N_EXPERTS, dtype=jnp.float32))

    def expert_step(acc, ex):
        wgu, bgu, wd, bd, cw = ex
        gu = xt @ wgu + bgu
        gate = jnp.minimum(gu[:, :D_FF], SWIGLU_LIMIT)
        up = jnp.clip(gu[:, D_FF:], -SWIGLU_LIMIT, SWIGLU_LIMIT)
        hid = (up + 1.0) * gate * jax.nn.sigmoid(SWIGLU_ALPHA * gate)
        out = hid @ wd + bd
        return acc + cw[:, None] * out.astype(jnp.float32), None

    acc0 = jnp.zeros((bsz * length, d), jnp.float32)
    out, _ = lax.scan(expert_step, acc0, (w_gu, b_gu, w_down, b_down, combine.T))
    return out.reshape(bsz, length, d).astype(h.dtype)


def setup_inputs(seed: int = 0) -> dict:
    key = jax.random.key(seed)
    ks = jax.random.split(key, 40)
    f32 = jnp.float32

    def nrm(k, shape, scale):
        return jax.random.normal(k, shape, f32) * scale

    D = D_MODEL
    col_scale = jnp.concatenate([jnp.ones((IN_WIDTH - ATTN_WIDTH,), f32),
                                 jnp.full((ATTN_WIDTH,), DEEPNORM_BETA, f32)])
    n = jnp.arange(SSM_STATE, dtype=f32)
    return {
        "x": nrm(ks[0], (BATCH, SEQ, D), 1.0),
        "meta_tokens": nrm(ks[1], (N_META, D), 1.0),
        "ln_in_g": 1.0 + nrm(ks[2], (D,), 0.01),
        "ln_in_b": nrm(ks[3], (D,), 0.01),
        "w_in": nrm(ks[4], (DEPTH, D, IN_WIDTH), D ** -0.5) * col_scale,
        "ssm_lambda_re": -0.5 + nrm(ks[5], (DEPTH, SSM_GROUPS, SSM_STATE), 0.01),
        "ssm_lambda_im": math.pi * n + nrm(ks[6], (DEPTH, SSM_GROUPS, SSM_STATE), 0.01),
        "ssm_log_dt": jax.random.uniform(ks[7], (DEPTH, SSM_GROUPS), f32, math.log(DT_MIN), math.log(DT_MAX)),
        "ssm_b_re": nrm(ks[8], (DEPTH, SSM_GROUPS, SSM_STATE, SSM_GROUP), (2 * SSM_GROUP) ** -0.5),
        "ssm_b_im": nrm(ks[9], (DEPTH, SSM_GROUPS, SSM_STATE, SSM_GROUP), (2 * SSM_GROUP) ** -0.5),
        "ssm_c_re": nrm(ks[10], (DEPTH, SSM_GROUPS, SSM_GROUP, SSM_STATE), (2 * SSM_STATE) ** -0.5),
        "ssm_c_im": nrm(ks[11], (DEPTH, SSM_GROUPS, SSM_GROUP, SSM_STATE), (2 * SSM_STATE) ** -0.5),
        "ssm_d": nrm(ks[12], (DEPTH, SSM_WIDTH), 1.0),
        "ssm_w_glu": nrm(ks[13], (DEPTH, SSM_WIDTH, SSM_WIDTH), SSM_WIDTH ** -0.5),
        "ssm_w_out": nrm(ks[14], (DEPTH, SSM_WIDTH, D), SSM_WIDTH ** -0.5),
        "conv_w": nrm(ks[15], (DEPTH, CONV_K, CONV_WIDTH), CONV_K ** -0.5),
        "conv_w_out": nrm(ks[16], (DEPTH, CONV_WIDTH, D), CONV_WIDTH ** -0.5),
        "attn_lambda_q1": nrm(ks[17], (DEPTH, ATTN_HEAD_DIM), 0.1),
        "attn_lambda_k1": nrm(ks[18], (DEPTH, ATTN_HEAD_DIM), 0.1),
        "attn_lambda_q2": nrm(ks[19], (DEPTH, ATTN_HEAD_DIM), 0.1),
        "attn_lambda_k2": nrm(ks[20], (DEPTH, ATTN_HEAD_DIM), 0.1),
        "attn_subln_g": 1.0 + nrm(ks[21], (DEPTH, ATTN_VALUE_DIM), 0.01),
        "attn_w_out": nrm(ks[22], (DEPTH, ATTN_WIDTH, D), ATTN_WIDTH ** -0.5),
        "gate_w": nrm(ks[23], (DEPTH, D, N_BRANCH * D), D ** -0.5),
        "gate_b": nrm(ks[24], (DEPTH, N_BRANCH * D), 0.01),
        "w_o": nrm(ks[25], (DEPTH, D, D), D ** -0.5 * DEEPNORM_BETA),
        "ln1_g": 1.0 + nrm(ks[26], (DEPTH, D), 0.01),
        "ln1_b": nrm(ks[27], (DEPTH, D), 0.01),
        "router_w": nrm(ks[28], (DEPTH, D, N_EXPERTS), D ** -0.5),
        "router_b": nrm(ks[29], (DEPTH, N_EXPERTS), 0.01),
        "expert_w_gu": nrm(ks[30], (DEPTH, N_EXPERTS, D, 2 * D_FF), D ** -0.5),
        "expert_b_gu": nrm(ks[31], (DEPTH, N_EXPERTS, 2 * D_FF), 0.01),
        "expert_w_down": nrm(ks[32], (DEPTH, N_EXPERTS, D_FF, D), D_FF ** -0.5 * DEEPNORM_BETA),
        "expert_b_down": nrm(ks[33], (DEPTH, N_EXPERTS, D), 0.01),
        "ln2_g": 1.0 + nrm(ks[34], (DEPTH, D), 0.01),
        "ln2_b": nrm(ks[35], (DEPTH, D), 0.01),
    }


def reference(x, meta_tokens, ln_in_g, ln_in_b, w_in, ssm_lambda_re, ssm_lambda_im, ssm_log_dt,
              ssm_b_re, ssm_b_im, ssm_c_re, ssm_c_im, ssm_d, ssm_w_glu, ssm_w_out, conv_w, conv_w_out,
              attn_lambda_q1, attn_lambda_k1, attn_lambda_q2, attn_lambda_k2, attn_subln_g, attn_w_out,
              gate_w, gate_b, w_o, ln1_g, ln1_b, router_w, router_b, expert_w_gu, expert_b_gu,
              expert_w_down, expert_b_down, ln2_g, ln2_b):
    bsz, seq, d = x.shape
    meta = jnp.broadcast_to(meta_tokens.astype(x.dtype)[None], (bsz, N_META, d))
    h = jnp.concatenate([meta, x], axis=1)
    length = h.shape[1]
    h = layer_norm(h, ln_in_g, ln_in_b)
    cos, sin = rope_tables(length)
    s0 = SSM_WIDTH
    s1 = s0 + CONV_WIDTH
    s2 = s1 + CONV_WIDTH
    s3 = s2 + CONV_WIDTH
    s4 = s3 + ATTN_WIDTH
    s5 = s4 + ATTN_WIDTH
    for l in range(DEPTH):
        proj = h @ w_in[l]
        u_ssm = proj[..., :s0]
        cb, cc, ch = proj[..., s0:s1], proj[..., s1:s2], proj[..., s2:s3]
        q = proj[..., s3:s4].reshape(bsz, length, ATTN_HEADS, 2, ATTN_HEAD_DIM)
        k = proj[..., s4:s5].reshape(bsz, length, ATTN_HEADS, 2, ATTN_HEAD_DIM)
        v = proj[..., s5:].reshape(bsz, length, ATTN_HEADS, ATTN_VALUE_DIM)
        y_ssm = s5_branch(u_ssm, ssm_lambda_re[l], ssm_lambda_im[l], ssm_log_dt[l], ssm_b_re[l], ssm_b_im[l],
                          ssm_c_re[l], ssm_c_im[l], ssm_d[l], ssm_w_glu[l]) @ ssm_w_out[l]
        y_conv = short_conv_branch(cb, cc, ch, conv_w[l]) @ conv_w_out[l]
        lam_init = 0.8 - 0.6 * math.exp(-0.3 * l)
        lam = (jnp.exp(jnp.sum(attn_lambda_q1[l].astype(jnp.float32) * attn_lambda_k1[l].astype(jnp.float32)))
               - jnp.exp(jnp.sum(attn_lambda_q2[l].astype(jnp.float32) * attn_lambda_k2[l].astype(jnp.float32)))
               + lam_init)
        y_attn = diff_attention(apply_rope(q, cos, sin), apply_rope(k, cos, sin), v, lam, lam_init,
                                attn_subln_g[l]) @ attn_w_out[l]
        g = jax.nn.sigmoid(h @ gate_w[l] + gate_b[l]).reshape(bsz, length, N_BRANCH, d)
        merged = g[:, :, 0] * y_ssm + g[:, :, 1] * y_conv + g[:, :, 2] * y_attn
        h = layer_norm(DEEPNORM_ALPHA * h + merged @ w_o[l], ln1_g[l], ln1_b[l])
        ffn = moe_ffn(h, router_w[l], router_b[l], expert_w_gu[l], expert_b_gu[l], expert_w_down[l], expert_b_down[l])
        h = layer_norm(DEEPNORM_ALPHA * h + ffn, ln2_g[l], ln2_b[l])
    return h[:, N_META:]
```

```python
import functools
import math

import jax
import jax.numpy as jnp
from jax import lax
from jax.experimental import pallas as pl
from jax.experimental.pallas import tpu as pltpu

F32 = jnp.float32
BF16 = jnp.bfloat16

D_MODEL = 1024
DEPTH = 2
CHUNK = 64
N_META = 16
SSM_WIDTH = 512
SSM_GROUP = 16
SSM_GROUPS = 32
SSM_STATE = 64
SSM_CH = SSM_GROUPS * SSM_STATE
CONV_WIDTH = 512
CONV_K = 3
ATTN_HEADS = 8
ATTN_HEAD_DIM = 64
ATTN_VALUE_DIM = 128
ATTN_WIDTH = 1024
ROPE_THETA = 10000.0
N_EXPERTS = 32
TOP_K = 4
D_FF = 1024
SWIGLU_LIMIT = 7.0
SWIGLU_ALPHA = 1.702
DEEPNORM_ALPHA = (2.0 * DEPTH) ** 0.25
LN_EPS = 1e-5
RMS_EPS = 1e-5
NEG_INF = -1e30

ATT_BLOCK = 256
LANES = 128
SUBLANES = 8
VMEM_LIMIT = 48 * 1024 * 1024


def _cparams(sem):
    return pltpu.CompilerParams(dimension_semantics=sem, vmem_limit_bytes=VMEM_LIMIT)


def _pick_tile(n, candidates):
    for c in candidates:
        if n % c == 0:
            return c
    raise ValueError(f"no tile for {n}")


def _layer_norm(x, g, b):
    mu = jnp.mean(x, axis=-1, keepdims=True)
    xc = x - mu
    var = jnp.mean(xc * xc, axis=-1, keepdims=True)
    return xc * lax.rsqrt(var + LN_EPS) * g + b


def _seq_pos(tile_idx, tiles_per_seq, tm):
    base = lax.rem(tile_idx, tiles_per_seq) * tm
    return base + lax.broadcasted_iota(jnp.int32, (tm, 1), 0)


def _ln_in_kernel(x_ref, g_ref, b_ref, h_ref, hb_ref, *, tiles_per_seq, front):
    tm = x_ref.shape[0]
    y = _layer_norm(x_ref[...], g_ref[...], b_ref[...])
    pos = _seq_pos(pl.program_id(0), tiles_per_seq, tm)
    y = jnp.where(pos >= front, y, 0.0)
    h_ref[...] = y
    hb_ref[...] = y.astype(BF16)


def _ln_in(hcat, g, b, *, lp, front):
    t, d = hcat.shape
    tm = _pick_tile(lp, (512, 256))
    return pl.pallas_call(
        functools.partial(_ln_in_kernel, tiles_per_seq=lp // tm, front=front),
        grid=(t // tm,),
        in_specs=[pl.BlockSpec((tm, d), lambda i: (i, 0)),
                  pl.BlockSpec((1, d), lambda i: (0, 0)),
                  pl.BlockSpec((1, d), lambda i: (0, 0))],
        out_specs=[pl.BlockSpec((tm, d), lambda i: (i, 0)),
                   pl.BlockSpec((tm, d), lambda i: (i, 0))],
        out_shape=[jax.ShapeDtypeStruct((t, d), F32), jax.ShapeDtypeStruct((t, d), BF16)],
        compiler_params=_cparams(("parallel",)),
        name="ln_in",
    )(hcat, g.reshape(1, d), b.reshape(1, d))


def _proj_sc_kernel(x_ref, w_ref, u_ref, cb_ref, z_ref):
    acc = jnp.dot(x_ref[...], w_ref[...], preferred_element_type=F32)
    s0, s1, s2, s3 = SSM_WIDTH, SSM_WIDTH + CONV_WIDTH, SSM_WIDTH + 2 * CONV_WIDTH, SSM_WIDTH + 3 * CONV_WIDTH
    u_ref[...] = acc[:, :s0]
    cb_ref[...] = acc[:, s0:s1]
    z_ref[...] = acc[:, s1:s2] * acc[:, s2:s3]


def _proj_sc(hb, w):
    t, d = hb.shape
    n = w.shape[1]
    tm = _pick_tile(t, (512, 256))
    out = jax.ShapeDtypeStruct((t, SSM_WIDTH), F32)
    ospec = pl.BlockSpec((tm, SSM_WIDTH), lambda i: (i, 0))
    return pl.pallas_call(
        _proj_sc_kernel,
        grid=(t // tm,),
        in_specs=[pl.BlockSpec((tm, d), lambda i: (i, 0)),
                  pl.BlockSpec((d, n), lambda i: (0, 0))],
        out_specs=[ospec, ospec, ospec],
        out_shape=[out, out, out],
        compiler_params=_cparams(("parallel",)),
        name="proj_ssm_conv",
    )(hb, w)


def _proj_qk_kernel(x_ref, w_ref, cos_ref, sin_ref, q_ref, k_ref):
    acc = jnp.dot(x_ref[...], w_ref[...], preferred_element_type=F32)
    cos = cos_ref[...]
    sin = sin_ref[...]
    lane = lax.broadcasted_iota(jnp.int32, (1, LANES), 1)
    low_half = lax.rem(lane, ATTN_HEAD_DIM) < ATTN_HEAD_DIM // 2
    nblk = ATTN_WIDTH // LANES
    for blk in range(2 * nblk):
        a = acc[:, blk * LANES:(blk + 1) * LANES]
        rot = jnp.where(low_half, -pltpu.roll(a, LANES - ATTN_HEAD_DIM // 2, 1),
                        pltpu.roll(a, ATTN_HEAD_DIM // 2, 1))
        r = a * cos + rot * sin
        if blk < nblk:
            q_ref[:, blk * LANES:(blk + 1) * LANES] = (r * (ATTN_HEAD_DIM ** -0.5)).astype(BF16)
        else:
            k_ref[:, (blk - nblk) * LANES:(blk - nblk + 1) * LANES] = r.astype(BF16)


def _proj_qk(hb, w, cos, sin, *, lp):
    t, d = hb.shape
    tm = _pick_tile(lp, (512, 256))
    tps = lp // tm
    out = jax.ShapeDtypeStruct((t, ATTN_WIDTH), BF16)
    return pl.pallas_call(
        _proj_qk_kernel,
        grid=(t // tm,),
        in_specs=[pl.BlockSpec((tm, d), lambda i: (i, 0)),
                  pl.BlockSpec((d, 2 * ATTN_WIDTH), lambda i: (0, 0)),
                  pl.BlockSpec((tm, LANES), lambda i: (i % tps, 0)),
                  pl.BlockSpec((tm, LANES), lambda i: (i % tps, 0))],
        out_specs=[pl.BlockSpec((tm, ATTN_WIDTH), lambda i: (i, 0)),
                   pl.BlockSpec((tm, ATTN_WIDTH), lambda i: (i, 0))],
        out_shape=[out, out],
        compiler_params=_cparams(("parallel",)),
        name="proj_qk_rope",
    )(hb, w, cos, sin)


def _proj_vt_kernel(wt_ref, x_ref, o_ref):
    o_ref[...] = lax.dot_general(wt_ref[...], x_ref[...], (((1,), (1,)), ((), ())),
                                 preferred_element_type=F32).astype(BF16)


def _proj_vt(hb, wt):
    t, d = hb.shape
    tm = _pick_tile(t, (512, 256))
    return pl.pallas_call(
        _proj_vt_kernel,
        grid=(t // tm,),
        in_specs=[pl.BlockSpec((ATTN_WIDTH, d), lambda i: (0, 0)),
                  pl.BlockSpec((tm, d), lambda i: (i, 0))],
        out_specs=pl.BlockSpec((ATTN_WIDTH, tm), lambda i: (0, i)),
        out_shape=jax.ShapeDtypeStruct((ATTN_WIDTH, t), BF16),
        compiler_params=_cparams(("parallel",)),
        name="proj_v_t",
    )(wt, hb)


def _proj_gate_kernel(x_ref, w_ref, b_ref, o_ref):
    acc = jnp.dot(x_ref[...], w_ref[...], preferred_element_type=F32) + b_ref[...]
    o_ref[...] = jax.nn.sigmoid(acc).astype(BF16)


def _proj_gate(hb, w, b):
    t, d = hb.shape
    n = w.shape[1]
    tm = _pick_tile(t, (512, 256))
    tn = 1024
    return pl.pallas_call(
        _proj_gate_kernel,
        grid=(t // tm, n // tn),
        in_specs=[pl.BlockSpec((tm, d), lambda i, j: (i, 0)),
                  pl.BlockSpec((d, tn), lambda i, j: (0, j)),
                  pl.BlockSpec((1, tn), lambda i, j: (0, j))],
        out_specs=pl.BlockSpec((tm, tn), lambda i, j: (i, j)),
        out_shape=jax.ShapeDtypeStruct((t, n), BF16),
        compiler_params=_cparams(("parallel", "parallel")),
        name="proj_gate",
    )(hb, w, b.reshape(1, n))


SCAN_STRIP = 512


def _ssm_kernel(u_ref, bdb_ref, bdc_ref, pw_ref, dskip_ref, wglu_ref, wout_ref, y_ref, bu_ref, carry_ref):
    tm = u_ref.shape[0]

    @pl.when(pl.program_id(1) == 0)
    def _():
        carry_ref[...] = jnp.zeros_like(carry_ref)

    u = u_ref[...]
    bu_ref[...] = jnp.dot(u.astype(BF16), bdb_ref[...], preferred_element_type=F32)

    row = lax.broadcasted_iota(jnp.int32, (SUBLANES, SCAN_STRIP), 0)
    for s in range(SSM_CH // SCAN_STRIP):
        re = slice(s * SCAN_STRIP, (s + 1) * SCAN_STRIP)
        im = slice(SSM_CH + s * SCAN_STRIP, SSM_CH + (s + 1) * SCAN_STRIP)
        steps = []
        for k, d in enumerate((1, 2, 4)):
            steps.append((d, jnp.broadcast_to(pw_ref[k:k + 1, re], (SUBLANES, SCAN_STRIP)),
                          jnp.broadcast_to(pw_ref[k:k + 1, im], (SUBLANES, SCAN_STRIP))))
        pr = pw_ref[SUBLANES:2 * SUBLANES, re]
        pi = pw_ref[SUBLANES:2 * SUBLANES, im]

        def group(gi, carry):
            cr, ci = carry
            r0 = pl.multiple_of(gi * SUBLANES, SUBLANES)
            xr = bu_ref[pl.ds(r0, SUBLANES), re]
            xi = bu_ref[pl.ds(r0, SUBLANES), im]
            for d, ar, ai in steps:
                sr = jnp.where(row >= d, pltpu.roll(xr, d, 0), 0.0)
                si = jnp.where(row >= d, pltpu.roll(xi, d, 0), 0.0)
                xr, xi = xr + ar * sr - ai * si, xi + ar * si + ai * sr
            xr, xi = xr + pr * cr - pi * ci, xi + pr * ci + pi * cr
            bu_ref[pl.ds(r0, SUBLANES), re] = xr
            bu_ref[pl.ds(r0, SUBLANES), im] = xi
            last_r = jnp.broadcast_to(xr[SUBLANES - 1:SUBLANES, :], (SUBLANES, SCAN_STRIP))
            last_i = jnp.broadcast_to(xi[SUBLANES - 1:SUBLANES, :], (SUBLANES, SCAN_STRIP))
            return last_r, last_i

        cr, ci = lax.fori_loop(0, tm // SUBLANES, group, (carry_ref[:, re], carry_ref[:, im]))
        carry_ref[:, re] = cr
        carry_ref[:, im] = ci

    y = jnp.dot(bu_ref[...].astype(BF16), bdc_ref[...], preferred_element_type=F32) + dskip_ref[...] * u
    y = jax.nn.gelu(y)
    y = y * jax.nn.sigmoid(jnp.dot(y.astype(BF16), wglu_ref[...], preferred_element_type=F32))
    y_ref[...] = jnp.dot(y.astype(BF16), wout_ref[...], preferred_element_type=F32)


def _ssm_branch(u, bdb, bdc, pw, dskip, wglu, wout, *, bsz, lp):
    t = u.shape[0]
    tm = _pick_tile(lp, (256,))
    tps = lp // tm
    const = lambda shape: pl.BlockSpec(shape, lambda b, i: (0, 0))
    return pl.pallas_call(
        _ssm_kernel,
        grid=(bsz, tps),
        in_specs=[pl.BlockSpec((tm, SSM_WIDTH), lambda b, i: (b * tps + i, 0)),
                  const(bdb.shape), const(bdc.shape), const(pw.shape), const((1, SSM_WIDTH)),
                  const(wglu.shape), const(wout.shape)],
        out_specs=pl.BlockSpec((tm, D_MODEL), lambda b, i: (b * tps + i, 0)),
        out_shape=jax.ShapeDtypeStruct((t, D_MODEL), F32),
        scratch_shapes=[pltpu.VMEM((tm, 2 * SSM_CH), F32), pltpu.VMEM((SUBLANES, 2 * SSM_CH), F32)],
        compiler_params=_cparams(("arbitrary", "arbitrary")),
        name="ssm_branch",
    )(u, bdb, bdc, pw, dskip.reshape(1, SSM_WIDTH), wglu, wout)


def _ssm_tables(lam_re, lam_im, log_dt, b_re, b_im, c_re, c_im):
    lr, li = lam_re.astype(F32), lam_im.astype(F32)
    dt = jnp.exp(log_dt.astype(F32))[:, None]
    mag = jnp.exp(lr * dt)
    ar = mag * jnp.cos(li * dt)
    ai = mag * jnp.sin(li * dt)
    denom = lr * lr + li * li
    nr, ni = ar - 1.0, ai
    coef_r = (nr * lr + ni * li) / denom
    coef_i = (ni * lr - nr * li) / denom
    br, bi = b_re.astype(F32), b_im.astype(F32)
    bbar_r = coef_r[..., None] * br - coef_i[..., None] * bi
    bbar_i = coef_r[..., None] * bi + coef_i[..., None] * br
    eye = jnp.eye(SSM_GROUPS, dtype=F32)
    bd_in = lambda m: jnp.einsum('gpc,gh->gchp', m, eye).reshape(SSM_WIDTH, SSM_CH)
    bdb = jnp.concatenate([bd_in(bbar_r), bd_in(bbar_i)], axis=1)
    bd_out = lambda m: jnp.einsum('gcp,gh->gphc', m, eye).reshape(SSM_CH, SSM_WIDTH)
    bdc = jnp.concatenate([bd_out(c_re.astype(F32)), -bd_out(c_im.astype(F32))], axis=0)
    a1 = (ar.reshape(-1), ai.reshape(-1))
    cmul = lambda x, y: (x[0] * y[0] - x[1] * y[1], x[0] * y[1] + x[1] * y[0])
    pows = [a1]
    for _ in range(SUBLANES - 1):
        pows.append(cmul(pows[-1], a1))
    rows = [pows[0], pows[1], pows[3]] + [(jnp.zeros_like(a1[0]),) * 2] * (SUBLANES - 3) + pows
    pw = jnp.stack([jnp.concatenate(r) for r in rows])
    return bdb.astype(BF16), bdc.astype(BF16), pw


def _conv_kernel(cb_ref, z_ref, zprev_ref, w_ref, wout_ref, y_ref):
    tm = z_ref.shape[0]
    z = z_ref[...]
    zz = jnp.concatenate([zprev_ref[...], z], axis=0)
    w = w_ref[...]
    y = w[2:3, :] * z
    for j in range(CONV_K - 1):
        shift = CONV_K - 1 - j
        y = y + w[j:j + 1, :] * zz[SUBLANES - shift:SUBLANES - shift + tm, :]
    y = cb_ref[...] * y
    y_ref[...] = jnp.dot(y.astype(BF16), wout_ref[...], preferred_element_type=F32)


def _conv_branch(cb, z, w, wout):
    t = z.shape[0]
    tm = _pick_tile(t, (512, 256))
    per = tm // SUBLANES
    return pl.pallas_call(
        _conv_kernel,
        grid=(t // tm,),
        in_specs=[pl.BlockSpec((tm, CONV_WIDTH), lambda i: (i, 0)),
                  pl.BlockSpec((tm, CONV_WIDTH), lambda i: (i, 0)),
                  pl.BlockSpec((SUBLANES, CONV_WIDTH), lambda i: (jnp.maximum(i * per - 1, 0), 0)),
                  pl.BlockSpec((SUBLANES, CONV_WIDTH), lambda i: (0, 0)),
                  pl.BlockSpec((CONV_WIDTH, D_MODEL), lambda i: (0, 0))],
        out_specs=pl.BlockSpec((tm, D_MODEL), lambda i: (i, 0)),
        out_shape=jax.ShapeDtypeStruct((t, D_MODEL), F32),
        compiler_params=_cparams(("parallel",)),
        name="conv_branch",
    )(cb, z, z, jnp.pad(w, ((0, SUBLANES - CONV_K), (0, 0))), wout)


def _attn_kernel(lam_ref, q_ref, k_ref, vt_ref, g_ref, o_ref,
                 qlo_ref, qhi_ref, m_ref, l_ref, acc1_ref, acc2_ref, *, front, lam_init):
    qi = pl.program_id(2)
    kj = pl.program_id(3)
    tq = q_ref.shape[0]
    tk = k_ref.shape[0]
    real0 = front + N_META

    @pl.when(kj == 0)
    def _():
        q = q_ref[...]
        lane = lax.broadcasted_iota(jnp.int32, q.shape, 1)
        qlo_ref[...] = jnp.where(lane < ATTN_HEAD_DIM, q, jnp.zeros_like(q))
        qhi_ref[...] = jnp.where(lane >= ATTN_HEAD_DIM, q, jnp.zeros_like(q))
        m_ref[...] = jnp.full_like(m_ref, NEG_INF)
        l_ref[...] = jnp.zeros_like(l_ref)
        acc1_ref[...] = jnp.zeros_like(acc1_ref)
        acc2_ref[...] = jnp.zeros_like(acc2_ref)

    def step(masked):
        k = k_ref[...]
        vt = vt_ref[...]
        nt = (((1,), (1,)), ((), ()))
        s1 = lax.dot_general(k, qlo_ref[...], nt, preferred_element_type=F32)
        s2 = lax.dot_general(k, qhi_ref[...], nt, preferred_element_type=F32)
        if masked:
            pk = kj * tk + lax.broadcasted_iota(jnp.int32, (tk, 1), 0)
            pq = qi * tq + lax.broadcasted_iota(jnp.int32, (1, tq), 1)
            cid_k = jnp.where(pk < front, 1 << 30, jnp.where(pk < real0, 0, 1 + (pk - real0) // CHUNK))
            cid_q = jnp.where(pq < real0, 0, 1 + (pq - real0) // CHUNK)
            vis = cid_k <= cid_q
            s1 = jnp.where(vis, s1, NEG_INF)
            s2 = jnp.where(vis, s2, NEG_INF)
        for idx, (s, acc_ref) in enumerate(((s1, acc1_ref), (s2, acc2_ref))):
            m_old = m_ref[idx:idx + 1, :]
            m_new = jnp.maximum(m_old, jnp.max(s, axis=0, keepdims=True))
            alpha = jnp.exp(m_old - m_new)
            p = jnp.exp(s - m_new)
            l_ref[idx:idx + 1, :] = alpha * l_ref[idx:idx + 1, :] + jnp.sum(p, axis=0, keepdims=True)
            m_ref[idx:idx + 1, :] = m_new
            acc_ref[...] = alpha * acc_ref[...] + jnp.dot(vt, p.astype(BF16), preferred_element_type=F32)

    needs_mask = jnp.logical_or(kj == qi, kj == 0)

    @pl.when(jnp.logical_and(kj <= qi, needs_mask))
    def _():
        step(True)

    @pl.when(jnp.logical_and(kj <= qi, jnp.logical_not(needs_mask)))
    def _():
        step(False)

    @pl.when(kj == qi)
    def _():
        lam = lam_ref[0]
        o = acc1_ref[...] / l_ref[0:1, :] - lam * (acc2_ref[...] / l_ref[1:2, :])
        o = o * lax.rsqrt(jnp.mean(o * o, axis=0, keepdims=True) + RMS_EPS)
        o = o * g_ref[...] * (1.0 - lam_init)
        o_ref[...] = o.T.astype(BF16)


def _attention(lam, q, k, vt, g_col, *, bsz, lp, front, lam_init):
    t = q.shape[0]
    blk = ATT_BLOCK
    nb = lp // blk
    kv_idx = lambda b, i, j: b * nb + jnp.minimum(j, i)
    return pl.pallas_call(
        functools.partial(_attn_kernel, front=front, lam_init=lam_init),
        grid=(bsz, ATTN_HEADS, nb, nb),
        in_specs=[pl.BlockSpec(memory_space=pltpu.SMEM),
                  pl.BlockSpec((blk, LANES), lambda b, h, i, j: (b * nb + i, h)),
                  pl.BlockSpec((blk, LANES), lambda b, h, i, j: (kv_idx(b, i, j), h)),
                  pl.BlockSpec((LANES, blk), lambda b, h, i, j: (h, kv_idx(b, i, j))),
                  pl.BlockSpec((LANES, blk), lambda b, h, i, j: (0, 0))],
        out_specs=pl.BlockSpec((blk, LANES), lambda b, h, i, j: (b * nb + i, h)),
        out_shape=jax.ShapeDtypeStruct((t, ATTN_WIDTH), BF16),
        scratch_shapes=[pltpu.VMEM((blk, LANES), BF16), pltpu.VMEM((blk, LANES), BF16),
                        pltpu.VMEM((SUBLANES, blk), F32), pltpu.VMEM((SUBLANES, blk), F32),
                        pltpu.VMEM((LANES, blk), F32), pltpu.VMEM((LANES, blk), F32)],
        compiler_params=_cparams(("parallel", "parallel", "arbitrary", "arbitrary")),
        name="diff_attention",
    )(lam, q, k, vt, g_col)


def _merge_kernel(h_ref, g_ref, ys_ref, yc_ref, o_ref, wao_ref, wo_ref, lg_ref, lb_ref,
                  h1_ref, h1b_ref, *, tiles_per_seq, front):
    tm = h_ref.shape[0]
    ya = jnp.dot(o_ref[...], wao_ref[...], preferred_element_type=F32)
    g = g_ref[...].astype(F32)
    merged = g[:, :D_MODEL] * ys_ref[...] + g[:, D_MODEL:2 * D_MODEL] * yc_ref[...] + g[:, 2 * D_MODEL:] * ya
    r = DEEPNORM_ALPHA * h_ref[...] + jnp.dot(merged.astype(BF16), wo_ref[...], preferred_element_type=F32)
    y = _layer_norm(r, lg_ref[...], lb_ref[...])
    pos = _seq_pos(pl.program_id(0), tiles_per_seq, tm)
    y = jnp.where(pos >= front, y, 0.0)
    h1_ref[...] = y
    h1b_ref[...] = y.astype(BF16)


def _merge(h, g, ys, yc, o, wao, wo, lg, lb, *, lp, front):
    t, d = h.shape
    tm = _pick_tile(lp, (256,))
    row = lambda w: pl.BlockSpec((tm, w), lambda i: (i, 0))
    const = lambda a, b: pl.BlockSpec((a, b), lambda i: (0, 0))
    return pl.pallas_call(
        functools.partial(_merge_kernel, tiles_per_seq=lp // tm, front=front),
        grid=(t // tm,),
        in_specs=[row(d), row(3 * d), row(d), row(d), row(ATTN_WIDTH),
                  const(ATTN_WIDTH, d), const(d, d), const(1, d), const(1, d)],
        out_specs=[row(d), row(d)],
        out_shape=[jax.ShapeDtypeStruct((t, d), F32), jax.ShapeDtypeStruct((t, d), BF16)],
        compiler_params=_cparams(("parallel",)),
        name="merge_ln1",
    )(h, g, ys, yc, o, wao, wo, lg.reshape(1, d), lb.reshape(1, d))


def _router_kernel(h_ref, w_ref, b_ref, cw_ref):
    logits = jnp.dot(h_ref[...], w_ref[...], preferred_element_type=F32,
                     precision=lax.Precision.HIGHEST) + b_ref[...]
    lane = lax.broadcasted_iota(jnp.int32, logits.shape, 1)
    work = logits
    sels, vals = [], []
    for _ in range(TOP_K):
        mx = jnp.max(work, axis=1, keepdims=True)
        first = jnp.min(jnp.where(work == mx, lane, N_EXPERTS), axis=1, keepdims=True)
        sel = lane == first
        sels.append(sel)
        vals.append(mx)
        work = jnp.where(sel, -jnp.inf, work)
    exps = [jnp.exp(v - vals[0]) for v in vals]
    den = exps[0] + exps[1] + exps[2] + exps[3]
    cw = jnp.zeros_like(logits)
    for sel, e in zip(sels, exps):
        cw = cw + jnp.where(sel, e / den, 0.0)
    cw_ref[...] = cw


def _router(h, w, b):
    t, d = h.shape
    tm = _pick_tile(t, (512, 256))
    return pl.pallas_call(
        _router_kernel,
        grid=(t // tm,),
        in_specs=[pl.BlockSpec((tm, d), lambda i: (i, 0)),
                  pl.BlockSpec((d, N_EXPERTS), lambda i: (0, 0)),
                  pl.BlockSpec((1, N_EXPERTS), lambda i: (0, 0))],
        out_specs=pl.BlockSpec((tm, N_EXPERTS), lambda i: (i, 0)),
        out_shape=jax.ShapeDtypeStruct((t, N_EXPERTS), F32),
        compiler_params=_cparams(("parallel",)),
        name="moe_router",
    )(h, w, b.reshape(1, N_EXPERTS))


def _moe_kernel(x_ref, cw_ref, h_ref, wgu_ref, bgu_ref, wd_ref, bd_ref, lg_ref, lb_ref,
                h2_ref, h2b_ref, acc_ref, *, tiles_per_seq, front):
    e = pl.program_id(1)
    tm = x_ref.shape[0]

    @pl.when(e == 0)
    def _():
        acc_ref[...] = jnp.zeros_like(acc_ref)

    gu = jnp.dot(x_ref[...], wgu_ref[0], preferred_element_type=F32) + bgu_ref[0]
    gate = jnp.minimum(gu[:, :D_FF], SWIGLU_LIMIT)
    up = jnp.clip(gu[:, D_FF:], -SWIGLU_LIMIT, SWIGLU_LIMIT)
    hid = (up + 1.0) * gate * jax.nn.sigmoid(SWIGLU_ALPHA * gate)
    out = jnp.dot(hid.astype(BF16), wd_ref[0], preferred_element_type=F32) + bd_ref[0]
    cw = cw_ref[...]
    lane = lax.broadcasted_iota(jnp.int32, cw.shape, 1)
    col = jnp.sum(jnp.where(lane == e, cw, 0.0), axis=1, keepdims=True)
    acc_ref[...] += col * out

    @pl.when(e == N_EXPERTS - 1)
    def _():
        y = _layer_norm(DEEPNORM_ALPHA * h_ref[...] + acc_ref[...], lg_ref[...], lb_ref[...])
        pos = _seq_pos(pl.program_id(0), tiles_per_seq, tm)
        y = jnp.where(pos >= front, y, 0.0)
        h2_ref[...] = y
        h2b_ref[...] = y.astype(BF16)


def _moe(xb, cw, h, wgu, bgu, wd, bd, lg, lb, *, lp, front):
    t, d = h.shape
    tm = _pick_tile(lp, (768, 256))
    row = lambda w: pl.BlockSpec((tm, w), lambda i, e: (i, 0))
    const = lambda a, b: pl.BlockSpec((a, b), lambda i, e: (0, 0))
    return pl.pallas_call(
        functools.partial(_moe_kernel, tiles_per_seq=lp // tm, front=front),
        grid=(t // tm, N_EXPERTS),
        in_specs=[row(d), row(N_EXPERTS), row(d),
                  pl.BlockSpec((1, d, 2 * D_FF), lambda i, e: (e, 0, 0)),
                  pl.BlockSpec((1, 1, 2 * D_FF), lambda i, e: (e, 0, 0)),
                  pl.BlockSpec((1, D_FF, d), lambda i, e: (e, 0, 0)),
                  pl.BlockSpec((1, 1, d), lambda i, e: (e, 0, 0)),
                  const(1, d), const(1, d)],
        out_specs=[row(d), row(d)],
        out_shape=[jax.ShapeDtypeStruct((t, d), F32), jax.ShapeDtypeStruct((t, d), BF16)],
        scratch_shapes=[pltpu.VMEM((tm, d), F32)],
        compiler_params=_cparams(("parallel", "arbitrary")),
        name="moe_experts_ln2",
    )(xb, cw, h, wgu, bgu.reshape(N_EXPERTS, 1, 2 * D_FF), wd, bd.reshape(N_EXPERTS, 1, d),
      lg.reshape(1, d), lb.reshape(1, d))


def _rope_tables(lp, front):
    pos = (jnp.arange(lp, dtype=jnp.int32) - front).astype(F32)
    inv = ROPE_THETA ** (-jnp.arange(0, ATTN_HEAD_DIM, 2, dtype=F32) / ATTN_HEAD_DIM)
    ang = pos[:, None] * inv[None, :]
    ang = jnp.concatenate([ang, ang, ang, ang], axis=-1)
    return jnp.cos(ang), jnp.sin(ang)


def kernel(x, meta_tokens, ln_in_g, ln_in_b, w_in, ssm_lambda_re, ssm_lambda_im, ssm_log_dt, ssm_b_re, ssm_b_im, ssm_c_re, ssm_c_im, ssm_d, ssm_w_glu, ssm_w_out, conv_w, conv_w_out, attn_lambda_q1, attn_lambda_k1, attn_lambda_q2, attn_lambda_k2, attn_subln_g, attn_w_out, gate_w, gate_b, w_o, ln1_g, ln1_b, router_w, router_b, expert_w_gu, expert_b_gu, expert_w_down, expert_b_down, ln2_g, ln2_b):
    bsz, seq, d = x.shape
    assert d == D_MODEL and seq % ATT_BLOCK == 0
    lp = seq + ATT_BLOCK
    front = ATT_BLOCK - N_META
    t = bsz * lp

    meta = jnp.broadcast_to(meta_tokens.astype(x.dtype)[None], (bsz, N_META, d))
    hcat = jnp.concatenate([jnp.zeros((bsz, front, d), x.dtype), meta, x], axis=1).reshape(t, d)
    h, hb = _ln_in(hcat, ln_in_g, ln_in_b, lp=lp, front=front)
    cos, sin = _rope_tables(lp, front)

    s3 = SSM_WIDTH + 3 * CONV_WIDTH
    s5 = s3 + 2 * ATTN_WIDTH
    for l in range(DEPTH):
        w_in_b = w_in[l].astype(BF16)
        u, cb, z = _proj_sc(hb, w_in_b[:, :s3])
        q, k = _proj_qk(hb, w_in_b[:, s3:s5], cos, sin, lp=lp)
        vt = _proj_vt(hb, w_in_b[:, s5:].T)
        g = _proj_gate(hb, gate_w[l].astype(BF16), gate_b[l])

        bdb, bdc, pw = _ssm_tables(ssm_lambda_re[l], ssm_lambda_im[l], ssm_log_dt[l], ssm_b_re[l], ssm_b_im[l],
                                   ssm_c_re[l], ssm_c_im[l])
        ys = _ssm_branch(u, bdb, bdc, pw, ssm_d[l].astype(F32), ssm_w_glu[l].astype(BF16),
                         ssm_w_out[l].astype(BF16), bsz=bsz, lp=lp)
        yc = _conv_branch(cb, z, conv_w[l].astype(F32), conv_w_out[l].astype(BF16))

        lam_init = 0.8 - 0.6 * math.exp(-0.3 * l)
        lam = (jnp.exp(jnp.sum(attn_lambda_q1[l].astype(F32) * attn_lambda_k1[l].astype(F32)))
               - jnp.exp(jnp.sum(attn_lambda_q2[l].astype(F32) * attn_lambda_k2[l].astype(F32)))
               + lam_init).reshape(1)
        g_col = jnp.broadcast_to(attn_subln_g[l].astype(F32)[:, None], (ATTN_VALUE_DIM, ATT_BLOCK))
        o = _attention(lam, q, k, vt, g_col, bsz=bsz, lp=lp, front=front, lam_init=lam_init)

        h, hb = _merge(h, g, ys, yc, o, attn_w_out[l].astype(BF16), w_o[l].astype(BF16), ln1_g[l], ln1_b[l],
                       lp=lp, front=front)

        cw = _router(h, router_w[l].astype(F32), router_b[l].astype(F32))
        h, hb = _moe(hb, cw, h, expert_w_gu[l].astype(BF16), expert_b_gu[l].astype(F32),
                     expert_w_down[l].astype(BF16), expert_b_down[l].astype(F32), ln2_g[l], ln2_b[l],
                     lp=lp, front=front)

    return h.reshape(bsz, lp, d)[:, ATT_BLOCK:]
```

```python
import functools
import math

import jax
import jax.numpy as jnp
from jax import lax
from jax.experimental import pallas as pl
from jax.experimental.pallas import tpu as pltpu

F32 = jnp.float32
BF16 = jnp.bfloat16

D_MODEL = 1024
DEPTH = 2
CHUNK = 64
N_META = 16
SSM_WIDTH = 512
SSM_GROUP = 16
SSM_GROUPS = 32
SSM_STATE = 64
SSM_CH = SSM_GROUPS * SSM_STATE
CONV_WIDTH = 512
CONV_K = 3
ATTN_HEADS = 8
ATTN_HEAD_DIM = 64
ATTN_VALUE_DIM = 128
ATTN_WIDTH = 1024
ROPE_THETA = 10000.0
N_EXPERTS = 32
TOP_K = 4
D_FF = 1024
SWIGLU_LIMIT = 7.0
SWIGLU_ALPHA = 1.702
DEEPNORM_ALPHA = (2.0 * DEPTH) ** 0.25
LN_EPS = 1e-5
RMS_EPS = 1e-5
NEG_INF = -1e30

ATT_BLOCK = 256
LANES = 128
SUBLANES = 8
VMEM_LIMIT = 48 * 1024 * 1024


def _cparams(sem):
    return pltpu.CompilerParams(dimension_semantics=sem, vmem_limit_bytes=VMEM_LIMIT)


def _pick_tile(n, candidates):
    for c in candidates:
        if n % c == 0:
            return c
    raise ValueError(f"no tile for {n}")


def _layer_norm(x, g, b):
    mu = jnp.mean(x, axis=-1, keepdims=True)
    xc = x - mu
    var = jnp.mean(xc * xc, axis=-1, keepdims=True)
    return xc * lax.rsqrt(var + LN_EPS) * g + b


def _seq_pos(tile_idx, tiles_per_seq, tm):
    base = lax.rem(tile_idx, tiles_per_seq) * tm
    return base + lax.broadcasted_iota(jnp.int32, (tm, 1), 0)


def _ln_in_kernel(x_ref, g_ref, b_ref, h_ref, hb_ref, *, tiles_per_seq, front):
    tm = x_ref.shape[0]
    y = _layer_norm(x_ref[...], g_ref[...], b_ref[...])
    pos = _seq_pos(pl.program_id(0), tiles_per_seq, tm)
    y = jnp.where(pos >= front, y, 0.0)
    h_ref[...] = y
    hb_ref[...] = y.astype(BF16)


def _ln_in(hcat, g, b, *, lp, front):
    t, d = hcat.shape
    tm = _pick_tile(lp, (512, 256))
    return pl.pallas_call(
        functools.partial(_ln_in_kernel, tiles_per_seq=lp // tm, front=front),
        grid=(t // tm,),
        in_specs=[pl.BlockSpec((tm, d), lambda i: (i, 0)),
                  pl.BlockSpec((1, d), lambda i: (0, 0)),
                  pl.BlockSpec((1, d), lambda i: (0, 0))],
        out_specs=[pl.BlockSpec((tm, d), lambda i: (i, 0)),
                   pl.BlockSpec((tm, d), lambda i: (i, 0))],
        out_shape=[jax.ShapeDtypeStruct((t, d), F32), jax.ShapeDtypeStruct((t, d), BF16)],
        compiler_params=_cparams(("parallel",)),
        name="ln_in",
    )(hcat, g.reshape(1, d), b.reshape(1, d))


def _proj_sc_kernel(x_ref, w_ref, u_ref, cb_ref, z_ref):
    acc = jnp.dot(x_ref[...], w_ref[...], preferred_element_type=F32)
    s0, s1, s2, s3 = SSM_WIDTH, SSM_WIDTH + CONV_WIDTH, SSM_WIDTH + 2 * CONV_WIDTH, SSM_WIDTH + 3 * CONV_WIDTH
    u_ref[...] = acc[:, :s0]
    cb_ref[...] = acc[:, s0:s1]
    z_ref[...] = acc[:, s1:s2] * acc[:, s2:s3]


def _proj_sc(hb, w):
    t, d = hb.shape
    n = w.shape[1]
    tm = _pick_tile(t, (512, 256))
    out = jax.ShapeDtypeStruct((t, SSM_WIDTH), F32)
    ospec = pl.BlockSpec((tm, SSM_WIDTH), lambda i: (i, 0))
    return pl.pallas_call(
        _proj_sc_kernel,
        grid=(t // tm,),
        in_specs=[pl.BlockSpec((tm, d), lambda i: (i, 0)),
                  pl.BlockSpec((d, n), lambda i: (0, 0))],
        out_specs=[ospec, ospec, ospec],
        out_shape=[out, out, out],
        compiler_params=_cparams(("parallel",)),
        name="proj_ssm_conv",
    )(hb, w)


def _proj_qk_kernel(x_ref, w_ref, cos_ref, sin_ref, q_ref, k_ref):
    acc = jnp.dot(x_ref[...], w_ref[...], preferred_element_type=F32)
    cos = cos_ref[...]
    sin = sin_ref[...]
    lane = lax.broadcasted_iota(jnp.int32, (1, LANES), 1)
    low_half = lax.rem(lane, ATTN_HEAD_DIM) < ATTN_HEAD_DIM // 2
    nblk = ATTN_WIDTH // LANES
    for blk in range(2 * nblk):
        a = acc[:, blk * LANES:(blk + 1) * LANES]
        rot = jnp.where(low_half, -pltpu.roll(a, LANES - ATTN_HEAD_DIM // 2, 1),
                        pltpu.roll(a, ATTN_HEAD_DIM // 2, 1))
        r = a * cos + rot * sin
        if blk < nblk:
            q_ref[:, blk * LANES:(blk + 1) * LANES] = (r * (ATTN_HEAD_DIM ** -0.5 * math.log2(math.e))).astype(BF16)
        else:
            k_ref[:, (blk - nblk) * LANES:(blk - nblk + 1) * LANES] = r.astype(BF16)


def _proj_qk(hb, w, cos, sin, *, lp):
    t, d = hb.shape
    tm = _pick_tile(lp, (512, 256))
    tps = lp // tm
    out = jax.ShapeDtypeStruct((t, ATTN_WIDTH), BF16)
    return pl.pallas_call(
        _proj_qk_kernel,
        grid=(t // tm,),
        in_specs=[pl.BlockSpec((tm, d), lambda i: (i, 0)),
                  pl.BlockSpec((d, 2 * ATTN_WIDTH), lambda i: (0, 0)),
                  pl.BlockSpec((tm, LANES), lambda i: (i % tps, 0)),
                  pl.BlockSpec((tm, LANES), lambda i: (i % tps, 0))],
        out_specs=[pl.BlockSpec((tm, ATTN_WIDTH), lambda i: (i, 0)),
                   pl.BlockSpec((tm, ATTN_WIDTH), lambda i: (i, 0))],
        out_shape=[out, out],
        compiler_params=_cparams(("parallel",)),
        name="proj_qk_rope",
    )(hb, w, cos, sin)


def _proj_vt_kernel(wt_ref, x_ref, o_ref):
    vt = lax.dot_general(wt_ref[...], x_ref[...], (((1,), (1,)), ((), ())),
                         preferred_element_type=F32).astype(BF16)
    for c in range(o_ref.shape[0]):
        o_ref[c] = vt[:, c * ATT_BLOCK:(c + 1) * ATT_BLOCK]


def _proj_vt(hb, wt):
    t, d = hb.shape
    tm = _pick_tile(t, (512, 256))
    per = tm // ATT_BLOCK
    return pl.pallas_call(
        _proj_vt_kernel,
        grid=(t // tm,),
        in_specs=[pl.BlockSpec((ATTN_WIDTH, d), lambda i: (0, 0)),
                  pl.BlockSpec((tm, d), lambda i: (i, 0))],
        out_specs=pl.BlockSpec((per, ATTN_WIDTH, ATT_BLOCK), lambda i: (i, 0, 0)),
        out_shape=jax.ShapeDtypeStruct((t // ATT_BLOCK, ATTN_WIDTH, ATT_BLOCK), BF16),
        compiler_params=_cparams(("parallel",)),
        name="proj_v_t",
    )(wt, hb)


def _proj_gate_kernel(x_ref, w_ref, b_ref, o_ref):
    acc = jnp.dot(x_ref[...], w_ref[...], preferred_element_type=F32) + b_ref[...]
    o_ref[...] = jax.nn.sigmoid(acc).astype(BF16)


def _proj_gate(hb, w, b):
    t, d = hb.shape
    n = w.shape[1]
    tm = _pick_tile(t, (512, 256))
    tn = 1024
    return pl.pallas_call(
        _proj_gate_kernel,
        grid=(t // tm, n // tn),
        in_specs=[pl.BlockSpec((tm, d), lambda i, j: (i, 0)),
                  pl.BlockSpec((d, tn), lambda i, j: (0, j)),
                  pl.BlockSpec((1, tn), lambda i, j: (0, j))],
        out_specs=pl.BlockSpec((tm, tn), lambda i, j: (i, j)),
        out_shape=jax.ShapeDtypeStruct((t, n), BF16),
        compiler_params=_cparams(("parallel", "parallel")),
        name="proj_gate",
    )(hb, w, b.reshape(1, n))


SCAN_STRIP = 512


def _ssm_kernel(u_ref, bdb_ref, bdc_ref, pw_ref, dskip_ref, wglu_ref, wout_ref, y_ref, bu_ref, carry_ref):
    tm = u_ref.shape[0]

    @pl.when(pl.program_id(1) == 0)
    def _():
        carry_ref[...] = jnp.zeros_like(carry_ref)

    u = u_ref[...]
    bu_ref[...] = jnp.dot(u.astype(BF16), bdb_ref[...], preferred_element_type=F32)

    row = lax.broadcasted_iota(jnp.int32, (SUBLANES, SCAN_STRIP), 0)
    for s in range(SSM_CH // SCAN_STRIP):
        re = slice(s * SCAN_STRIP, (s + 1) * SCAN_STRIP)
        im = slice(SSM_CH + s * SCAN_STRIP, SSM_CH + (s + 1) * SCAN_STRIP)
        steps = []
        for k, d in enumerate((1, 2, 4)):
            steps.append((d, jnp.broadcast_to(pw_ref[k:k + 1, re], (SUBLANES, SCAN_STRIP)),
                          jnp.broadcast_to(pw_ref[k:k + 1, im], (SUBLANES, SCAN_STRIP))))
        pr = pw_ref[SUBLANES:2 * SUBLANES, re]
        pi = pw_ref[SUBLANES:2 * SUBLANES, im]

        def group(gi, carry):
            cr, ci = carry
            r0 = pl.multiple_of(gi * SUBLANES, SUBLANES)
            xr = bu_ref[pl.ds(r0, SUBLANES), re]
            xi = bu_ref[pl.ds(r0, SUBLANES), im]
            for d, ar, ai in steps:
                sr = jnp.where(row >= d, pltpu.roll(xr, d, 0), 0.0)
                si = jnp.where(row >= d, pltpu.roll(xi, d, 0), 0.0)
                xr, xi = xr + ar * sr - ai * si, xi + ar * si + ai * sr
            xr, xi = xr + pr * cr - pi * ci, xi + pr * ci + pi * cr
            bu_ref[pl.ds(r0, SUBLANES), re] = xr
            bu_ref[pl.ds(r0, SUBLANES), im] = xi
            last_r = jnp.broadcast_to(xr[SUBLANES - 1:SUBLANES, :], (SUBLANES, SCAN_STRIP))
            last_i = jnp.broadcast_to(xi[SUBLANES - 1:SUBLANES, :], (SUBLANES, SCAN_STRIP))
            return last_r, last_i

        cr, ci = lax.fori_loop(0, tm // SUBLANES, group, (carry_ref[:, re], carry_ref[:, im]))
        carry_ref[:, re] = cr
        carry_ref[:, im] = ci

    y = jnp.dot(bu_ref[...].astype(BF16), bdc_ref[...], preferred_element_type=F32) + dskip_ref[...] * u
    y = jax.nn.gelu(y)
    y = y * jax.nn.sigmoid(jnp.dot(y.astype(BF16), wglu_ref[...], preferred_element_type=F32))
    y_ref[...] = jnp.dot(y.astype(BF16), wout_ref[...], preferred_element_type=F32)


def _ssm_branch(u, bdb, bdc, pw, dskip, wglu, wout, *, bsz, lp):
    t = u.shape[0]
    tm = _pick_tile(lp, (256,))
    tps = lp // tm
    const = lambda shape: pl.BlockSpec(shape, lambda b, i: (0, 0))
    return pl.pallas_call(
        _ssm_kernel,
        grid=(bsz, tps),
        in_specs=[pl.BlockSpec((tm, SSM_WIDTH), lambda b, i: (b * tps + i, 0)),
                  const(bdb.shape), const(bdc.shape), const(pw.shape), const((1, SSM_WIDTH)),
                  const(wglu.shape), const(wout.shape)],
        out_specs=pl.BlockSpec((tm, D_MODEL), lambda b, i: (b * tps + i, 0)),
        out_shape=jax.ShapeDtypeStruct((t, D_MODEL), F32),
        scratch_shapes=[pltpu.VMEM((tm, 2 * SSM_CH), F32), pltpu.VMEM((SUBLANES, 2 * SSM_CH), F32)],
        compiler_params=_cparams(("arbitrary", "arbitrary")),
        name="ssm_branch",
    )(u, bdb, bdc, pw, dskip.reshape(1, SSM_WIDTH), wglu, wout)


def _ssm_tables(lam_re, lam_im, log_dt, b_re, b_im, c_re, c_im):
    lr, li = lam_re.astype(F32), lam_im.astype(F32)
    dt = jnp.exp(log_dt.astype(F32))[:, None]
    mag = jnp.exp(lr * dt)
    ar = mag * jnp.cos(li * dt)
    ai = mag * jnp.sin(li * dt)
    denom = lr * lr + li * li
    nr, ni = ar - 1.0, ai
    coef_r = (nr * lr + ni * li) / denom
    coef_i = (ni * lr - nr * li) / denom
    br, bi = b_re.astype(F32), b_im.astype(F32)
    bbar_r = coef_r[..., None] * br - coef_i[..., None] * bi
    bbar_i = coef_r[..., None] * bi + coef_i[..., None] * br
    eye = jnp.eye(SSM_GROUPS, dtype=F32)
    bd_in = lambda m: jnp.einsum('gpc,gh->gchp', m, eye).reshape(SSM_WIDTH, SSM_CH)
    bdb = jnp.concatenate([bd_in(bbar_r), bd_in(bbar_i)], axis=1)
    bd_out = lambda m: jnp.einsum('gcp,gh->gphc', m, eye).reshape(SSM_CH, SSM_WIDTH)
    bdc = jnp.concatenate([bd_out(c_re.astype(F32)), -bd_out(c_im.astype(F32))], axis=0)
    a1 = (ar.reshape(-1), ai.reshape(-1))
    cmul = lambda x, y: (x[0] * y[0] - x[1] * y[1], x[0] * y[1] + x[1] * y[0])
    pows = [a1]
    for _ in range(SUBLANES - 1):
        pows.append(cmul(pows[-1], a1))
    rows = [pows[0], pows[1], pows[3]] + [(jnp.zeros_like(a1[0]),) * 2] * (SUBLANES - 3) + pows
    pw = jnp.stack([jnp.concatenate(r) for r in rows])
    return bdb.astype(BF16), bdc.astype(BF16), pw


def _conv_kernel(cb_ref, z_ref, zprev_ref, w_ref, wout_ref, y_ref):
    tm = z_ref.shape[0]
    z = z_ref[...]
    zz = jnp.concatenate([zprev_ref[...], z], axis=0)
    w = w_ref[...]
    y = w[2:3, :] * z
    for j in range(CONV_K - 1):
        shift = CONV_K - 1 - j
        y = y + w[j:j + 1, :] * zz[SUBLANES - shift:SUBLANES - shift + tm, :]
    y = cb_ref[...] * y
    y_ref[...] = jnp.dot(y.astype(BF16), wout_ref[...], preferred_element_type=F32)


def _conv_branch(cb, z, w, wout):
    t = z.shape[0]
    tm = _pick_tile(t, (512, 256))
    per = tm // SUBLANES
    return pl.pallas_call(
        _conv_kernel,
        grid=(t // tm,),
        in_specs=[pl.BlockSpec((tm, CONV_WIDTH), lambda i: (i, 0)),
                  pl.BlockSpec((tm, CONV_WIDTH), lambda i: (i, 0)),
                  pl.BlockSpec((SUBLANES, CONV_WIDTH), lambda i: (jnp.maximum(i * per - 1, 0), 0)),
                  pl.BlockSpec((SUBLANES, CONV_WIDTH), lambda i: (0, 0)),
                  pl.BlockSpec((CONV_WIDTH, D_MODEL), lambda i: (0, 0))],
        out_specs=pl.BlockSpec((tm, D_MODEL), lambda i: (i, 0)),
        out_shape=jax.ShapeDtypeStruct((t, D_MODEL), F32),
        compiler_params=_cparams(("parallel",)),
        name="conv_branch",
    )(cb, z, z, jnp.pad(w, ((0, SUBLANES - CONV_K), (0, 0))), wout)


ATT_SUB = 3


def _attn_kernel(lam_ref, q_ref, k_ref, vt_ref, g_ref, o_ref,
                 qlo_ref, qhi_ref, m_ref, l_ref, acc_ref, *, front, lam_init):
    sb = pl.program_id(2)
    blk = ATT_BLOCK
    real0 = front + N_META
    nt = (((1,), (1,)), ((), ()))

    q = q_ref[...]
    lane = lax.broadcasted_iota(jnp.int32, q.shape, 1)
    qlo_ref[...] = jnp.where(lane < ATTN_HEAD_DIM, q, jnp.zeros_like(q))
    qhi_ref[...] = jnp.where(lane >= ATTN_HEAD_DIM, q, jnp.zeros_like(q))
    m_ref[...] = jnp.full_like(m_ref, NEG_INF)
    l_ref[...] = jnp.zeros_like(l_ref)
    acc_ref[...] = jnp.zeros_like(acc_ref)

    def pair(a, j, k, vt, masked):
        rows = slice(a * blk, (a + 1) * blk)
        scores = [lax.dot_general(k, qref[rows, :], nt, preferred_element_type=F32)
                  for qref in (qlo_ref, qhi_ref)]
        if masked:
            pk = j * blk + lax.broadcasted_iota(jnp.int32, (blk, 1), 0)
            pq = (sb * ATT_SUB + a) * blk + lax.broadcasted_iota(jnp.int32, (1, blk), 1)
            cid_k = jnp.where(pk < front, 1 << 30, jnp.where(pk < real0, 0, 1 + (pk - real0) // CHUNK))
            cid_q = jnp.where(pq < real0, 0, 1 + (pq - real0) // CHUNK)
            vis = cid_k <= cid_q
            scores = [jnp.where(vis, s, NEG_INF) for s in scores]
        for idx, s in enumerate(scores):
            m_old = m_ref[a, idx]
            m_new = jnp.maximum(m_old, jnp.max(s, axis=0, keepdims=True))
            alpha = jnp.exp2(m_old - m_new)
            p = jnp.exp2(s - m_new)
            l_ref[a, idx] = alpha * l_ref[a, idx] + jnp.sum(p, axis=0, keepdims=True)
            m_ref[a, idx] = m_new
            acc_ref[a, idx] = alpha * acc_ref[a, idx] + jnp.dot(vt, p.astype(BF16), preferred_element_type=F32)

    def kv_block(j):
        return k_ref[pl.ds(pl.multiple_of(j * blk, blk), blk), :], vt_ref[j]

    @pl.when(sb > 0)
    def _():
        k, vt = kv_block(0)
        for a in range(ATT_SUB):
            pair(a, 0, k, vt, True)

    def body(j, carry):
        k, vt = kv_block(j)
        for a in range(ATT_SUB):
            pair(a, j, k, vt, False)
        return carry

    lax.fori_loop(1, sb * ATT_SUB, body, 0)

    for dj in range(ATT_SUB):
        j = sb * ATT_SUB + dj
        k, vt = kv_block(j)
        for a in range(dj, ATT_SUB):
            pair(a, j, k, vt, True)

    lam = lam_ref[0]
    for a in range(ATT_SUB):
        o = acc_ref[a, 0] / l_ref[a, 0] - lam * (acc_ref[a, 1] / l_ref[a, 1])
        o = o * lax.rsqrt(jnp.mean(o * o, axis=0, keepdims=True) + RMS_EPS)
        o = o * g_ref[...] * (1.0 - lam_init)
        o_ref[a * blk:(a + 1) * blk, :] = o.T.astype(BF16)


def _attention(lam, q, k, vt3, g_col, *, bsz, lp, front, lam_init):
    t = q.shape[0]
    blk = ATT_BLOCK
    nb = lp // blk
    assert nb % ATT_SUB == 0
    nsb = nb // ATT_SUB
    tq = ATT_SUB * blk
    return pl.pallas_call(
        functools.partial(_attn_kernel, front=front, lam_init=lam_init),
        grid=(bsz, ATTN_HEADS, nsb),
        in_specs=[pl.BlockSpec(memory_space=pltpu.SMEM),
                  pl.BlockSpec((tq, LANES), lambda b, h, i: (b * nsb + i, h)),
                  pl.BlockSpec((lp, LANES), lambda b, h, i: (b, h)),
                  pl.BlockSpec((nb, LANES, blk), lambda b, h, i: (b, h, 0)),
                  pl.BlockSpec((LANES, blk), lambda b, h, i: (0, 0))],
        out_specs=pl.BlockSpec((tq, LANES), lambda b, h, i: (b * nsb + i, h)),
        out_shape=jax.ShapeDtypeStruct((t, ATTN_WIDTH), BF16),
        scratch_shapes=[pltpu.VMEM((tq, LANES), BF16), pltpu.VMEM((tq, LANES), BF16),
                        pltpu.VMEM((ATT_SUB, 2, 1, blk), F32), pltpu.VMEM((ATT_SUB, 2, 1, blk), F32),
                        pltpu.VMEM((ATT_SUB, 2, LANES, blk), F32)],
        compiler_params=_cparams(("parallel", "parallel", "arbitrary")),
        name="diff_attention",
    )(lam, q, k, vt3, g_col)


def _merge_kernel(h_ref, g_ref, ys_ref, yc_ref, o_ref, wao_ref, wo_ref, lg_ref, lb_ref,
                  h1_ref, h1b_ref, *, tiles_per_seq, front):
    tm = h_ref.shape[0]
    ya = jnp.dot(o_ref[...], wao_ref[...], preferred_element_type=F32)
    g = g_ref[...].astype(F32)
    merged = g[:, :D_MODEL] * ys_ref[...] + g[:, D_MODEL:2 * D_MODEL] * yc_ref[...] + g[:, 2 * D_MODEL:] * ya
    r = DEEPNORM_ALPHA * h_ref[...] + jnp.dot(merged.astype(BF16), wo_ref[...], preferred_element_type=F32)
    y = _layer_norm(r, lg_ref[...], lb_ref[...])
    pos = _seq_pos(pl.program_id(0), tiles_per_seq, tm)
    y = jnp.where(pos >= front, y, 0.0)
    h1_ref[...] = y
    h1b_ref[...] = y.astype(BF16)


def _merge(h, g, ys, yc, o, wao, wo, lg, lb, *, lp, front):
    t, d = h.shape
    tm = _pick_tile(lp, (256,))
    row = lambda w: pl.BlockSpec((tm, w), lambda i: (i, 0))
    const = lambda a, b: pl.BlockSpec((a, b), lambda i: (0, 0))
    return pl.pallas_call(
        functools.partial(_merge_kernel, tiles_per_seq=lp // tm, front=front),
        grid=(t // tm,),
        in_specs=[row(d), row(3 * d), row(d), row(d), row(ATTN_WIDTH),
                  const(ATTN_WIDTH, d), const(d, d), const(1, d), const(1, d)],
        out_specs=[row(d), row(d)],
        out_shape=[jax.ShapeDtypeStruct((t, d), F32), jax.ShapeDtypeStruct((t, d), BF16)],
        compiler_params=_cparams(("parallel",)),
        name="merge_ln1",
    )(h, g, ys, yc, o, wao, wo, lg.reshape(1, d), lb.reshape(1, d))


def _router_kernel(h_ref, w_ref, b_ref, cw_ref):
    logits = jnp.dot(h_ref[...], w_ref[...], preferred_element_type=F32,
                     precision=lax.Precision.HIGHEST) + b_ref[...]
    lane = lax.broadcasted_iota(jnp.int32, logits.shape, 1)
    work = logits
    sels, vals = [], []
    for _ in range(TOP_K):
        mx = jnp.max(work, axis=1, keepdims=True)
        first = jnp.min(jnp.where(work == mx, lane, N_EXPERTS), axis=1, keepdims=True)
        sel = lane == first
        sels.append(sel)
        vals.append(mx)
        work = jnp.where(sel, -jnp.inf, work)
    exps = [jnp.exp(v - vals[0]) for v in vals]
    den = exps[0] + exps[1] + exps[2] + exps[3]
    cw = jnp.zeros_like(logits)
    for sel, e in zip(sels, exps):
        cw = cw + jnp.where(sel, e / den, 0.0)
    cw_ref[...] = cw


def _router(h, w, b):
    t, d = h.shape
    tm = _pick_tile(t, (512, 256))
    return pl.pallas_call(
        _router_kernel,
        grid=(t // tm,),
        in_specs=[pl.BlockSpec((tm, d), lambda i: (i, 0)),
                  pl.BlockSpec((d, N_EXPERTS), lambda i: (0, 0)),
                  pl.BlockSpec((1, N_EXPERTS), lambda i: (0, 0))],
        out_specs=pl.BlockSpec((tm, N_EXPERTS), lambda i: (i, 0)),
        out_shape=jax.ShapeDtypeStruct((t, N_EXPERTS), F32),
        compiler_params=_cparams(("parallel",)),
        name="moe_router",
    )(h, w, b.reshape(1, N_EXPERTS))


def _moe_kernel(x_ref, cw_ref, h_ref, wgu_ref, bgu_ref, wd_ref, bd_ref, lg_ref, lb_ref,
                h2_ref, h2b_ref, acc_ref, *, tiles_per_seq, front):
    e = pl.program_id(1)
    tm = x_ref.shape[0]

    @pl.when(e == 0)
    def _():
        acc_ref[...] = jnp.zeros_like(acc_ref)

    gu = jnp.dot(x_ref[...], wgu_ref[0], preferred_element_type=F32) + bgu_ref[0]
    gate = jnp.minimum(gu[:, :D_FF], SWIGLU_LIMIT)
    up = jnp.clip(gu[:, D_FF:], -SWIGLU_LIMIT, SWIGLU_LIMIT)
    hid = (up + 1.0) * gate * jax.nn.sigmoid(SWIGLU_ALPHA * gate)
    out = jnp.dot(hid.astype(BF16), wd_ref[0], preferred_element_type=F32) + bd_ref[0]
    cw = cw_ref[...]
    lane = lax.broadcasted_iota(jnp.int32, cw.shape, 1)
    col = jnp.sum(jnp.where(lane == e, cw, 0.0), axis=1, keepdims=True)
    acc_ref[...] += col * out

    @pl.when(e == N_EXPERTS - 1)
    def _():
        y = _layer_norm(DEEPNORM_ALPHA * h_ref[...] + acc_ref[...], lg_ref[...], lb_ref[...])
        pos = _seq_pos(pl.program_id(0), tiles_per_seq, tm)
        y = jnp.where(pos >= front, y, 0.0)
        h2_ref[...] = y
        h2b_ref[...] = y.astype(BF16)


def _moe(xb, cw, h, wgu, bgu, wd, bd, lg, lb, *, lp, front):
    t, d = h.shape
    tm = _pick_tile(lp, (768, 256))
    row = lambda w: pl.BlockSpec((tm, w), lambda i, e: (i, 0))
    const = lambda a, b: pl.BlockSpec((a, b), lambda i, e: (0, 0))
    return pl.pallas_call(
        functools.partial(_moe_kernel, tiles_per_seq=lp // tm, front=front),
        grid=(t // tm, N_EXPERTS),
        in_specs=[row(d), row(N_EXPERTS), row(d),
                  pl.BlockSpec((1, d, 2 * D_FF), lambda i, e: (e, 0, 0)),
                  pl.BlockSpec((1, 1, 2 * D_FF), lambda i, e: (e, 0, 0)),
                  pl.BlockSpec((1, D_FF, d), lambda i, e: (e, 0, 0)),
                  pl.BlockSpec((1, 1, d), lambda i, e: (e, 0, 0)),
                  const(1, d), const(1, d)],
        out_specs=[row(d), row(d)],
        out_shape=[jax.ShapeDtypeStruct((t, d), F32), jax.ShapeDtypeStruct((t, d), BF16)],
        scratch_shapes=[pltpu.VMEM((tm, d), F32)],
        compiler_params=_cparams(("parallel", "arbitrary")),
        name="moe_experts_ln2",
    )(xb, cw, h, wgu, bgu.reshape(N_EXPERTS, 1, 2 * D_FF), wd, bd.reshape(N_EXPERTS, 1, d),
      lg.reshape(1, d), lb.reshape(1, d))


def _rope_tables(lp, front):
    pos = (jnp.arange(lp, dtype=jnp.int32) - front).astype(F32)
    inv = ROPE_THETA ** (-jnp.arange(0, ATTN_HEAD_DIM, 2, dtype=F32) / ATTN_HEAD_DIM)
    ang = pos[:, None] * inv[None, :]
    ang = jnp.concatenate([ang, ang, ang, ang], axis=-1)
    return jnp.cos(ang), jnp.sin(ang)


def kernel(x, meta_tokens, ln_in_g, ln_in_b, w_in, ssm_lambda_re, ssm_lambda_im, ssm_log_dt, ssm_b_re, ssm_b_im, ssm_c_re, ssm_c_im, ssm_d, ssm_w_glu, ssm_w_out, conv_w, conv_w_out, attn_lambda_q1, attn_lambda_k1, attn_lambda_q2, attn_lambda_k2, attn_subln_g, attn_w_out, gate_w, gate_b, w_o, ln1_g, ln1_b, router_w, router_b, expert_w_gu, expert_b_gu, expert_w_down, expert_b_down, ln2_g, ln2_b):
    bsz, seq, d = x.shape
    assert d == D_MODEL and seq % ATT_BLOCK == 0
    lp = seq + ATT_BLOCK
    front = ATT_BLOCK - N_META
    t = bsz * lp

    meta = jnp.broadcast_to(meta_tokens.astype(x.dtype)[None], (bsz, N_META, d))
    hcat = jnp.concatenate([jnp.zeros((bsz, front, d), x.dtype), meta, x], axis=1).reshape(t, d)
    h, hb = _ln_in(hcat, ln_in_g, ln_in_b, lp=lp, front=front)
    cos, sin = _rope_tables(lp, front)

    s3 = SSM_WIDTH + 3 * CONV_WIDTH
    s5 = s3 + 2 * ATTN_WIDTH
    for l in range(DEPTH):
        w_in_b = w_in[l].astype(BF16)
        u, cb, z = _proj_sc(hb, w_in_b[:, :s3])
        q, k = _proj_qk(hb, w_in_b[:, s3:s5], cos, sin, lp=lp)
        vt = _proj_vt(hb, w_in_b[:, s5:].T)
        g = _proj_gate(hb, gate_w[l].astype(BF16), gate_b[l])

        bdb, bdc, pw = _ssm_tables(ssm_lambda_re[l], ssm_lambda_im[l], ssm_log_dt[l], ssm_b_re[l], ssm_b_im[l],
                                   ssm_c_re[l], ssm_c_im[l])
        ys = _ssm_branch(u, bdb, bdc, pw, ssm_d[l].astype(F32), ssm_w_glu[l].astype(BF16),
                         ssm_w_out[l].astype(BF16), bsz=bsz, lp=lp)
        yc = _conv_branch(cb, z, conv_w[l].astype(F32), conv_w_out[l].astype(BF16))

        lam_init = 0.8 - 0.6 * math.exp(-0.3 * l)
        lam = (jnp.exp(jnp.sum(attn_lambda_q1[l].astype(F32) * attn_lambda_k1[l].astype(F32)))
               - jnp.exp(jnp.sum(attn_lambda_q2[l].astype(F32) * attn_lambda_k2[l].astype(F32)))
               + lam_init).reshape(1)
        g_col = jnp.broadcast_to(attn_subln_g[l].astype(F32)[:, None], (ATTN_VALUE_DIM, ATT_BLOCK))
        o = _attention(lam, q, k, vt, g_col, bsz=bsz, lp=lp, front=front, lam_init=lam_init)

        h, hb = _merge(h, g, ys, yc, o, attn_w_out[l].astype(BF16), w_o[l].astype(BF16), ln1_g[l], ln1_b[l],
                       lp=lp, front=front)

        cw = _router(h, router_w[l].astype(F32), router_b[l].astype(F32))
        h, hb = _moe(hb, cw, h, expert_w_gu[l].astype(BF16), expert_b_gu[l].astype(F32),
                     expert_w_down[l].astype(BF16), expert_b_down[l].astype(F32), ln2_g[l], ln2_b[l],
                     lp=lp, front=front)

    return h.reshape(bsz, lp, d)[:, ATT_BLOCK:]
```

```python
import functools
import math

import jax
import jax.numpy as jnp
from jax import lax
from jax.experimental import pallas as pl
from jax.experimental.pallas import tpu as pltpu

F32 = jnp.float32
BF16 = jnp.bfloat16

D_MODEL = 1024
DEPTH = 2
CHUNK = 64
N_META = 16
SSM_WIDTH = 512
SSM_GROUP = 16
SSM_GROUPS = 32
SSM_STATE = 64
SSM_CH = SSM_GROUPS * SSM_STATE
CONV_WIDTH = 512
CONV_K = 3
ATTN_HEADS = 8
ATTN_HEAD_DIM = 64
ATTN_VALUE_DIM = 128
ATTN_WIDTH = 1024
ROPE_THETA = 10000.0
N_EXPERTS = 32
TOP_K = 4
D_FF = 1024
SWIGLU_LIMIT = 7.0
SWIGLU_ALPHA = 1.702
DEEPNORM_ALPHA = (2.0 * DEPTH) ** 0.25
LN_EPS = 1e-5
RMS_EPS = 1e-5
NEG_INF = -1e30

ATT_BLOCK = 256
LANES = 128
SUBLANES = 8
VMEM_LIMIT = 48 * 1024 * 1024


def _cparams(sem):
    return pltpu.CompilerParams(dimension_semantics=sem, vmem_limit_bytes=VMEM_LIMIT)


def _pick_tile(n, candidates):
    for c in candidates:
        if n % c == 0:
            return c
    raise ValueError(f"no tile for {n}")


def _layer_norm(x, g, b):
    mu = jnp.mean(x, axis=-1, keepdims=True)
    xc = x - mu
    var = jnp.mean(xc * xc, axis=-1, keepdims=True)
    return xc * lax.rsqrt(var + LN_EPS) * g + b


def _seq_pos(tile_idx, tiles_per_seq, tm):
    base = lax.rem(tile_idx, tiles_per_seq) * tm
    return base + lax.broadcasted_iota(jnp.int32, (tm, 1), 0)


def _ln_in_kernel(x_ref, g_ref, b_ref, h_ref, hb_ref, *, tiles_per_seq, front):
    tm = x_ref.shape[0]
    y = _layer_norm(x_ref[...], g_ref[...], b_ref[...])
    pos = _seq_pos(pl.program_id(0), tiles_per_seq, tm)
    y = jnp.where(pos >= front, y, 0.0)
    h_ref[...] = y
    hb_ref[...] = y.astype(BF16)


def _ln_in(hcat, g, b, *, lp, front):
    t, d = hcat.shape
    tm = _pick_tile(lp, (512, 256))
    return pl.pallas_call(
        functools.partial(_ln_in_kernel, tiles_per_seq=lp // tm, front=front),
        grid=(t // tm,),
        in_specs=[pl.BlockSpec((tm, d), lambda i: (i, 0)),
                  pl.BlockSpec((1, d), lambda i: (0, 0)),
                  pl.BlockSpec((1, d), lambda i: (0, 0))],
        out_specs=[pl.BlockSpec((tm, d), lambda i: (i, 0)),
                   pl.BlockSpec((tm, d), lambda i: (i, 0))],
        out_shape=[jax.ShapeDtypeStruct((t, d), F32), jax.ShapeDtypeStruct((t, d), BF16)],
        compiler_params=_cparams(("parallel",)),
        name="ln_in",
    )(hcat, g.reshape(1, d), b.reshape(1, d))


def _proj_sc_kernel(x_ref, w_ref, u_ref, cb_ref, z_ref):
    acc = jnp.dot(x_ref[...], w_ref[...], preferred_element_type=F32)
    s0, s1, s2, s3 = SSM_WIDTH, SSM_WIDTH + CONV_WIDTH, SSM_WIDTH + 2 * CONV_WIDTH, SSM_WIDTH + 3 * CONV_WIDTH
    u_ref[...] = acc[:, :s0]
    cb_ref[...] = acc[:, s0:s1]
    z_ref[...] = acc[:, s1:s2] * acc[:, s2:s3]


def _proj_sc(hb, w):
    t, d = hb.shape
    n = w.shape[1]
    tm = _pick_tile(t, (512, 256))
    out = jax.ShapeDtypeStruct((t, SSM_WIDTH), F32)
    ospec = pl.BlockSpec((tm, SSM_WIDTH), lambda i: (i, 0))
    return pl.pallas_call(
        _proj_sc_kernel,
        grid=(t // tm,),
        in_specs=[pl.BlockSpec((tm, d), lambda i: (i, 0)),
                  pl.BlockSpec((d, n), lambda i: (0, 0))],
        out_specs=[ospec, ospec, ospec],
        out_shape=[out, out, out],
        compiler_params=_cparams(("parallel",)),
        name="proj_ssm_conv",
    )(hb, w)


def _proj_qk_kernel(x_ref, w_ref, cos_ref, sin_ref, q_ref, k_ref):
    acc = jnp.dot(x_ref[...], w_ref[...], preferred_element_type=F32)
    cos = cos_ref[...]
    sin = sin_ref[...]
    lane = lax.broadcasted_iota(jnp.int32, (1, LANES), 1)
    low_half = lax.rem(lane, ATTN_HEAD_DIM) < ATTN_HEAD_DIM // 2
    nblk = ATTN_WIDTH // LANES
    for blk in range(2 * nblk):
        a = acc[:, blk * LANES:(blk + 1) * LANES]
        rot = jnp.where(low_half, -pltpu.roll(a, LANES - ATTN_HEAD_DIM // 2, 1),
                        pltpu.roll(a, ATTN_HEAD_DIM // 2, 1))
        r = a * cos + rot * sin
        if blk < nblk:
            q_ref[:, blk * LANES:(blk + 1) * LANES] = (r * (ATTN_HEAD_DIM ** -0.5 * math.log2(math.e))).astype(BF16)
        else:
            k_ref[:, (blk - nblk) * LANES:(blk - nblk + 1) * LANES] = r.astype(BF16)


def _proj_qk(hb, w, cos, sin, *, lp):
    t, d = hb.shape
    tm = _pick_tile(lp, (512, 256))
    tps = lp // tm
    out = jax.ShapeDtypeStruct((t, ATTN_WIDTH), BF16)
    return pl.pallas_call(
        _proj_qk_kernel,
        grid=(t // tm,),
        in_specs=[pl.BlockSpec((tm, d), lambda i: (i, 0)),
                  pl.BlockSpec((d, 2 * ATTN_WIDTH), lambda i: (0, 0)),
                  pl.BlockSpec((tm, LANES), lambda i: (i % tps, 0)),
                  pl.BlockSpec((tm, LANES), lambda i: (i % tps, 0))],
        out_specs=[pl.BlockSpec((tm, ATTN_WIDTH), lambda i: (i, 0)),
                   pl.BlockSpec((tm, ATTN_WIDTH), lambda i: (i, 0))],
        out_shape=[out, out],
        compiler_params=_cparams(("parallel",)),
        name="proj_qk_rope",
    )(hb, w, cos, sin)


def _proj_vt_kernel(wt_ref, x_ref, o_ref):
    vt = lax.dot_general(wt_ref[...], x_ref[...], (((1,), (1,)), ((), ())),
                         preferred_element_type=F32).astype(BF16)
    for c in range(o_ref.shape[0]):
        o_ref[c] = vt[:, c * ATT_BLOCK:(c + 1) * ATT_BLOCK]


def _proj_vt(hb, wt):
    t, d = hb.shape
    tm = _pick_tile(t, (512, 256))
    per = tm // ATT_BLOCK
    return pl.pallas_call(
        _proj_vt_kernel,
        grid=(t // tm,),
        in_specs=[pl.BlockSpec((ATTN_WIDTH, d), lambda i: (0, 0)),
                  pl.BlockSpec((tm, d), lambda i: (i, 0))],
        out_specs=pl.BlockSpec((per, ATTN_WIDTH, ATT_BLOCK), lambda i: (i, 0, 0)),
        out_shape=jax.ShapeDtypeStruct((t // ATT_BLOCK, ATTN_WIDTH, ATT_BLOCK), BF16),
        compiler_params=_cparams(("parallel",)),
        name="proj_v_t",
    )(wt, hb)


def _proj_gate_kernel(x_ref, w_ref, b_ref, o_ref):
    acc = jnp.dot(x_ref[...], w_ref[...], preferred_element_type=F32) + b_ref[...]
    o_ref[...] = jax.nn.sigmoid(acc).astype(BF16)


def _proj_gate(hb, w, b):
    t, d = hb.shape
    n = w.shape[1]
    tm = _pick_tile(t, (512, 256))
    tn = 1024
    return pl.pallas_call(
        _proj_gate_kernel,
        grid=(t // tm, n // tn),
        in_specs=[pl.BlockSpec((tm, d), lambda i, j: (i, 0)),
                  pl.BlockSpec((d, tn), lambda i, j: (0, j)),
                  pl.BlockSpec((1, tn), lambda i, j: (0, j))],
        out_specs=pl.BlockSpec((tm, tn), lambda i, j: (i, j)),
        out_shape=jax.ShapeDtypeStruct((t, n), BF16),
        compiler_params=_cparams(("parallel", "parallel")),
        name="proj_gate",
    )(hb, w, b.reshape(1, n))


SCAN_STRIP = 512


def _ssm_kernel(u_ref, bdb_ref, bdc_ref, pw_ref, dskip_ref, wglu_ref, wout_ref, y_ref, bu_ref, carry_ref):
    tm = u_ref.shape[0]

    @pl.when(pl.program_id(1) == 0)
    def _():
        carry_ref[...] = jnp.zeros_like(carry_ref)

    u = u_ref[...]
    bu_ref[...] = jnp.dot(u.astype(BF16), bdb_ref[...], preferred_element_type=F32)

    row = lax.broadcasted_iota(jnp.int32, (SUBLANES, SCAN_STRIP), 0)
    for s in range(SSM_CH // SCAN_STRIP):
        re = slice(s * SCAN_STRIP, (s + 1) * SCAN_STRIP)
        im = slice(SSM_CH + s * SCAN_STRIP, SSM_CH + (s + 1) * SCAN_STRIP)
        steps = []
        for k, d in enumerate((1, 2, 4)):
            steps.append((d, jnp.broadcast_to(pw_ref[k:k + 1, re], (SUBLANES, SCAN_STRIP)),
                          jnp.broadcast_to(pw_ref[k:k + 1, im], (SUBLANES, SCAN_STRIP))))
        pr = pw_ref[SUBLANES:2 * SUBLANES, re]
        pi = pw_ref[SUBLANES:2 * SUBLANES, im]

        def group(gi, carry):
            cr, ci = carry
            r0 = pl.multiple_of(gi * SUBLANES, SUBLANES)
            xr = bu_ref[pl.ds(r0, SUBLANES), re]
            xi = bu_ref[pl.ds(r0, SUBLANES), im]
            for d, ar, ai in steps:
                sr = jnp.where(row >= d, pltpu.roll(xr, d, 0), 0.0)
                si = jnp.where(row >= d, pltpu.roll(xi, d, 0), 0.0)
                xr, xi = xr + ar * sr - ai * si, xi + ar * si + ai * sr
            xr, xi = xr + pr * cr - pi * ci, xi + pr * ci + pi * cr
            bu_ref[pl.ds(r0, SUBLANES), re] = xr
            bu_ref[pl.ds(r0, SUBLANES), im] = xi
            last_r = jnp.broadcast_to(xr[SUBLANES - 1:SUBLANES, :], (SUBLANES, SCAN_STRIP))
            last_i = jnp.broadcast_to(xi[SUBLANES - 1:SUBLANES, :], (SUBLANES, SCAN_STRIP))
            return last_r, last_i

        cr, ci = lax.fori_loop(0, tm // SUBLANES, group, (carry_ref[:, re], carry_ref[:, im]))
        carry_ref[:, re] = cr
        carry_ref[:, im] = ci

    y = jnp.dot(bu_ref[...].astype(BF16), bdc_ref[...], preferred_element_type=F32) + dskip_ref[...] * u
    y = jax.nn.gelu(y)
    y = y * jax.nn.sigmoid(jnp.dot(y.astype(BF16), wglu_ref[...], preferred_element_type=F32))
    y_ref[...] = jnp.dot(y.astype(BF16), wout_ref[...], preferred_element_type=F32)


def _ssm_branch(u, bdb, bdc, pw, dskip, wglu, wout, *, bsz, lp):
    t = u.shape[0]
    tm = _pick_tile(lp, (256,))
    tps = lp // tm
    const = lambda shape: pl.BlockSpec(shape, lambda b, i: (0, 0))
    return pl.pallas_call(
        _ssm_kernel,
        grid=(bsz, tps),
        in_specs=[pl.BlockSpec((tm, SSM_WIDTH), lambda b, i: (b * tps + i, 0)),
                  const(bdb.shape), const(bdc.shape), const(pw.shape), const((1, SSM_WIDTH)),
                  const(wglu.shape), const(wout.shape)],
        out_specs=pl.BlockSpec((tm, D_MODEL), lambda b, i: (b * tps + i, 0)),
        out_shape=jax.ShapeDtypeStruct((t, D_MODEL), F32),
        scratch_shapes=[pltpu.VMEM((tm, 2 * SSM_CH), F32), pltpu.VMEM((SUBLANES, 2 * SSM_CH), F32)],
        compiler_params=_cparams(("arbitrary", "arbitrary")),
        name="ssm_branch",
    )(u, bdb, bdc, pw, dskip.reshape(1, SSM_WIDTH), wglu, wout)


def _ssm_tables(lam_re, lam_im, log_dt, b_re, b_im, c_re, c_im):
    lr, li = lam_re.astype(F32), lam_im.astype(F32)
    dt = jnp.exp(log_dt.astype(F32))[:, None]
    mag = jnp.exp(lr * dt)
    ar = mag * jnp.cos(li * dt)
    ai = mag * jnp.sin(li * dt)
    denom = lr * lr + li * li
    nr, ni = ar - 1.0, ai
    coef_r = (nr * lr + ni * li) / denom
    coef_i = (ni * lr - nr * li) / denom
    br, bi = b_re.astype(F32), b_im.astype(F32)
    bbar_r = coef_r[..., None] * br - coef_i[..., None] * bi
    bbar_i = coef_r[..., None] * bi + coef_i[..., None] * br
    eye = jnp.eye(SSM_GROUPS, dtype=F32)
    bd_in = lambda m: jnp.einsum('gpc,gh->gchp', m, eye).reshape(SSM_WIDTH, SSM_CH)
    bdb = jnp.concatenate([bd_in(bbar_r), bd_in(bbar_i)], axis=1)
    bd_out = lambda m: jnp.einsum('gcp,gh->gphc', m, eye).reshape(SSM_CH, SSM_WIDTH)
    bdc = jnp.concatenate([bd_out(c_re.astype(F32)), -bd_out(c_im.astype(F32))], axis=0)
    a1 = (ar.reshape(-1), ai.reshape(-1))
    cmul = lambda x, y: (x[0] * y[0] - x[1] * y[1], x[0] * y[1] + x[1] * y[0])
    pows = [a1]
    for _ in range(SUBLANES - 1):
        pows.append(cmul(pows[-1], a1))
    rows = [pows[0], pows[1], pows[3]] + [(jnp.zeros_like(a1[0]),) * 2] * (SUBLANES - 3) + pows
    pw = jnp.stack([jnp.concatenate(r) for r in rows])
    return bdb.astype(BF16), bdc.astype(BF16), pw


def _conv_kernel(cb_ref, z_ref, zprev_ref, w_ref, wout_ref, y_ref):
    tm = z_ref.shape[0]
    z = z_ref[...]
    zz = jnp.concatenate([zprev_ref[...], z], axis=0)
    w = w_ref[...]
    y = w[2:3, :] * z
    for j in range(CONV_K - 1):
        shift = CONV_K - 1 - j
        y = y + w[j:j + 1, :] * zz[SUBLANES - shift:SUBLANES - shift + tm, :]
    y = cb_ref[...] * y
    y_ref[...] = jnp.dot(y.astype(BF16), wout_ref[...], preferred_element_type=F32)


def _conv_branch(cb, z, w, wout):
    t = z.shape[0]
    tm = _pick_tile(t, (512, 256))
    per = tm // SUBLANES
    return pl.pallas_call(
        _conv_kernel,
        grid=(t // tm,),
        in_specs=[pl.BlockSpec((tm, CONV_WIDTH), lambda i: (i, 0)),
                  pl.BlockSpec((tm, CONV_WIDTH), lambda i: (i, 0)),
                  pl.BlockSpec((SUBLANES, CONV_WIDTH), lambda i: (jnp.maximum(i * per - 1, 0), 0)),
                  pl.BlockSpec((SUBLANES, CONV_WIDTH), lambda i: (0, 0)),
                  pl.BlockSpec((CONV_WIDTH, D_MODEL), lambda i: (0, 0))],
        out_specs=pl.BlockSpec((tm, D_MODEL), lambda i: (i, 0)),
        out_shape=jax.ShapeDtypeStruct((t, D_MODEL), F32),
        compiler_params=_cparams(("parallel",)),
        name="conv_branch",
    )(cb, z, z, jnp.pad(w, ((0, SUBLANES - CONV_K), (0, 0))), wout)


ATT_SUB = 3


def _attn_kernel(lam_ref, q_ref, k_ref, vt_ref, g_ref, o_ref,
                 qlo_ref, qhi_ref, m_ref, l_ref, acc_ref, *, front, lam_init):
    sb = pl.program_id(2)
    blk = ATT_BLOCK
    real0 = front + N_META
    nt = (((1,), (1,)), ((), ()))

    q = q_ref[...]
    lane = lax.broadcasted_iota(jnp.int32, q.shape, 1)
    qlo_ref[...] = jnp.where(lane < ATTN_HEAD_DIM, q, jnp.zeros_like(q))
    qhi_ref[...] = jnp.where(lane >= ATTN_HEAD_DIM, q, jnp.zeros_like(q))
    m_ref[...] = jnp.full_like(m_ref, NEG_INF)
    l_ref[...] = jnp.zeros_like(l_ref)
    acc_ref[...] = jnp.zeros_like(acc_ref)

    def pair(a, j, k, vt, masked):
        rows = slice(a * blk, (a + 1) * blk)
        scores = [lax.dot_general(k, qref[rows, :], nt, preferred_element_type=F32)
                  for qref in (qlo_ref, qhi_ref)]
        if masked:
            pk = j * blk + lax.broadcasted_iota(jnp.int32, (blk, 1), 0)
            pq = (sb * ATT_SUB + a) * blk + lax.broadcasted_iota(jnp.int32, (1, blk), 1)
            cid_k = jnp.where(pk < front, 1 << 30, jnp.where(pk < real0, 0, 1 + (pk - real0) // CHUNK))
            cid_q = jnp.where(pq < real0, 0, 1 + (pq - real0) // CHUNK)
            vis = cid_k <= cid_q
            scores = [jnp.where(vis, s, NEG_INF) for s in scores]
        for idx, s in enumerate(scores):
            m_old = m_ref[a, idx]
            m_new = jnp.maximum(m_old, jnp.max(s, axis=0, keepdims=True))
            alpha = jnp.exp2(m_old - m_new)
            p = jnp.exp2(s - m_new)
            l_ref[a, idx] = alpha * l_ref[a, idx] + jnp.sum(p, axis=0, keepdims=True)
            m_ref[a, idx] = m_new
            acc_ref[a, idx] = alpha * acc_ref[a, idx] + jnp.dot(vt, p.astype(BF16), preferred_element_type=F32)

    def kv_block(j):
        return k_ref[pl.ds(pl.multiple_of(j * blk, blk), blk), :], vt_ref[j]

    @pl.when(sb > 0)
    def _():
        k, vt = kv_block(0)
        for a in range(ATT_SUB):
            pair(a, 0, k, vt, True)

    def body(j, carry):
        k, vt = kv_block(j)
        for a in range(ATT_SUB):
            pair(a, j, k, vt, False)
        return carry

    lax.fori_loop(1, sb * ATT_SUB, body, 0)

    for dj in range(ATT_SUB):
        j = sb * ATT_SUB + dj
        k, vt = kv_block(j)
        for a in range(dj, ATT_SUB):
            pair(a, j, k, vt, True)

    lam = lam_ref[0]
    for a in range(ATT_SUB):
        o = acc_ref[a, 0] / l_ref[a, 0] - lam * (acc_ref[a, 1] / l_ref[a, 1])
        o = o * lax.rsqrt(jnp.mean(o * o, axis=0, keepdims=True) + RMS_EPS)
        o = o * g_ref[...] * (1.0 - lam_init)
        o_ref[a * blk:(a + 1) * blk, :] = o.T.astype(BF16)


def _attention(lam, q, k, vt3, g_col, *, bsz, lp, front, lam_init):
    t = q.shape[0]
    blk = ATT_BLOCK
    nb = lp // blk
    assert nb % ATT_SUB == 0
    nsb = nb // ATT_SUB
    tq = ATT_SUB * blk
    return pl.pallas_call(
        functools.partial(_attn_kernel, front=front, lam_init=lam_init),
        grid=(bsz, ATTN_HEADS, nsb),
        in_specs=[pl.BlockSpec(memory_space=pltpu.SMEM),
                  pl.BlockSpec((tq, LANES), lambda b, h, i: (b * nsb + i, h)),
                  pl.BlockSpec((lp, LANES), lambda b, h, i: (b, h)),
                  pl.BlockSpec((nb, LANES, blk), lambda b, h, i: (b, h, 0)),
                  pl.BlockSpec((LANES, blk), lambda b, h, i: (0, 0))],
        out_specs=pl.BlockSpec((tq, LANES), lambda b, h, i: (b * nsb + i, h)),
        out_shape=jax.ShapeDtypeStruct((t, ATTN_WIDTH), BF16),
        scratch_shapes=[pltpu.VMEM((tq, LANES), BF16), pltpu.VMEM((tq, LANES), BF16),
                        pltpu.VMEM((ATT_SUB, 2, 1, blk), F32), pltpu.VMEM((ATT_SUB, 2, 1, blk), F32),
                        pltpu.VMEM((ATT_SUB, 2, LANES, blk), F32)],
        compiler_params=_cparams(("parallel", "parallel", "arbitrary")),
        name="diff_attention",
    )(lam, q, k, vt3, g_col)


def _merge_kernel(h_ref, g_ref, ys_ref, yc_ref, o_ref, wao_ref, wo_ref, lg_ref, lb_ref,
                  h1_ref, h1b_ref, *, tiles_per_seq, front):
    tm = h_ref.shape[0]
    ya = jnp.dot(o_ref[...], wao_ref[...], preferred_element_type=F32)
    g = g_ref[...].astype(F32)
    merged = g[:, :D_MODEL] * ys_ref[...] + g[:, D_MODEL:2 * D_MODEL] * yc_ref[...] + g[:, 2 * D_MODEL:] * ya
    r = DEEPNORM_ALPHA * h_ref[...] + jnp.dot(merged.astype(BF16), wo_ref[...], preferred_element_type=F32)
    y = _layer_norm(r, lg_ref[...], lb_ref[...])
    pos = _seq_pos(pl.program_id(0), tiles_per_seq, tm)
    y = jnp.where(pos >= front, y, 0.0)
    h1_ref[...] = y
    h1b_ref[...] = y.astype(BF16)


def _merge(h, g, ys, yc, o, wao, wo, lg, lb, *, lp, front):
    t, d = h.shape
    tm = _pick_tile(lp, (256,))
    row = lambda w: pl.BlockSpec((tm, w), lambda i: (i, 0))
    const = lambda a, b: pl.BlockSpec((a, b), lambda i: (0, 0))
    return pl.pallas_call(
        functools.partial(_merge_kernel, tiles_per_seq=lp // tm, front=front),
        grid=(t // tm,),
        in_specs=[row(d), row(3 * d), row(d), row(d), row(ATTN_WIDTH),
                  const(ATTN_WIDTH, d), const(d, d), const(1, d), const(1, d)],
        out_specs=[row(d), row(d)],
        out_shape=[jax.ShapeDtypeStruct((t, d), F32), jax.ShapeDtypeStruct((t, d), BF16)],
        compiler_params=_cparams(("parallel",)),
        name="merge_ln1",
    )(h, g, ys, yc, o, wao, wo, lg.reshape(1, d), lb.reshape(1, d))


def _router_kernel(h_ref, w_ref, b_ref, ids_ref, wts_ref, rank_ref, cnt_ref, base_ref):
    tm = h_ref.shape[0]

    @pl.when(pl.program_id(0) == 0)
    def _():
        base_ref[...] = jnp.zeros_like(base_ref)

    logits = jnp.dot(h_ref[...], w_ref[...], preferred_element_type=F32,
                     precision=lax.Precision.HIGHEST) + b_ref[...]
    lane = lax.broadcasted_iota(jnp.int32, logits.shape, 1).astype(F32)
    work = logits
    sels, vals, firsts = [], [], []
    for _ in range(TOP_K):
        mx = jnp.max(work, axis=1, keepdims=True)
        first = jnp.min(jnp.where(work == mx, lane, float(N_EXPERTS)), axis=1, keepdims=True)
        sel = lane == first
        sels.append(sel)
        vals.append(mx)
        firsts.append(first)
        work = jnp.where(sel, -jnp.inf, work)
    exps = [jnp.exp(v - vals[0]) for v in vals]
    den = exps[0] + exps[1] + exps[2] + exps[3]

    onehot = jnp.zeros_like(logits)
    for sel in sels:
        onehot = onehot + jnp.where(sel, 1.0, 0.0)
    r = lax.broadcasted_iota(jnp.int32, (tm, tm), 0)
    c = lax.broadcasted_iota(jnp.int32, (tm, tm), 1)
    tri = jnp.where(r > c, 1.0, 0.0).astype(BF16)
    before = jnp.dot(tri, onehot.astype(BF16), preferred_element_type=F32) + base_ref[...]
    base_ref[...] = base_ref[...] + jnp.sum(onehot, axis=0, keepdims=True)
    cnt_ref[...] = base_ref[...]

    out_lane = lax.broadcasted_iota(jnp.int32, (tm, LANES), 1)
    ids = jnp.zeros((tm, LANES), F32)
    wts = jnp.zeros((tm, LANES), F32)
    rank = jnp.zeros((tm, LANES), F32)
    for kk in range(TOP_K):
        rk = jnp.sum(jnp.where(sels[kk], before, 0.0), axis=1, keepdims=True)
        ids = jnp.where(out_lane == kk, firsts[kk], ids)
        wts = jnp.where(out_lane == kk, exps[kk] / den, wts)
        rank = jnp.where(out_lane == kk, rk, rank)
    ids_ref[...] = ids.astype(jnp.int32)
    wts_ref[...] = wts
    rank_ref[...] = rank.astype(jnp.int32)


def _router(h, w, b):
    t, d = h.shape
    tm = _pick_tile(t, (512, 256))
    wide = lambda dt: jax.ShapeDtypeStruct((t, LANES), dt)
    row = pl.BlockSpec((tm, LANES), lambda i: (i, 0))
    return pl.pallas_call(
        _router_kernel,
        grid=(t // tm,),
        in_specs=[pl.BlockSpec((tm, d), lambda i: (i, 0)),
                  pl.BlockSpec((d, N_EXPERTS), lambda i: (0, 0)),
                  pl.BlockSpec((1, N_EXPERTS), lambda i: (0, 0))],
        out_specs=[row, row, row, pl.BlockSpec((1, N_EXPERTS), lambda i: (0, 0))],
        out_shape=[wide(jnp.int32), wide(F32), wide(jnp.int32), jax.ShapeDtypeStruct((1, N_EXPERTS), F32)],
        scratch_shapes=[pltpu.VMEM((1, N_EXPERTS), F32)],
        compiler_params=_cparams(("arbitrary",)),
        name="moe_router",
    )(h, w, b.reshape(1, N_EXPERTS))


MOE_TILE = 256
ROUTE_TOK = 256


def _dispatch_kernel(last_ref, nu_ref, dest_ref, h_ref, xs_ref, zero_ref, zsem, sem):
    i = pl.program_id(0)
    n = pl.num_programs(0)
    n_tiles = xs_ref.shape[0] // MOE_TILE

    def fill(tile):
        return pltpu.make_async_copy(zero_ref, xs_ref.at[pl.ds(tile * MOE_TILE, MOE_TILE)], zsem)

    @pl.when(i == 0)
    def _():
        zero_ref[...] = jnp.zeros_like(zero_ref)
        for e in range(N_EXPERTS):
            @pl.when(last_ref[e] >= 0)
            def _():
                fill(last_ref[e]).start()

        def start_unused(tile, carry):
            fill(tile).start()
            return carry

        def wait_unused(tile, carry):
            fill(tile).wait()
            return carry

        lax.fori_loop(nu_ref[0], n_tiles, start_unused, 0)
        for e in range(N_EXPERTS):
            @pl.when(last_ref[e] >= 0)
            def _():
                fill(last_ref[e]).wait()
        lax.fori_loop(nu_ref[0], n_tiles, wait_unused, 0)

    def issue(r, carry):
        for kk in range(TOP_K):
            d = dest_ref[r * TOP_K + kk]
            pltpu.make_async_copy(h_ref.at[pl.ds(i * ROUTE_TOK + r, 1)], xs_ref.at[pl.ds(d, 1)], sem).start()
        return carry

    lax.fori_loop(0, ROUTE_TOK, issue, 0)

    def drain():
        for _ in range(TOP_K):
            pltpu.make_async_copy(h_ref.at[pl.ds(0, ROUTE_TOK)], xs_ref.at[pl.ds(0, ROUTE_TOK)], sem).wait()

    @pl.when(i > 0)
    def _():
        drain()

    @pl.when(i == n - 1)
    def _():
        drain()


def _dispatch(h, dest_flat, last_tile, n_used, n_rows):
    t, d = h.shape
    return pl.pallas_call(
        _dispatch_kernel,
        grid_spec=pltpu.PrefetchScalarGridSpec(
            num_scalar_prefetch=2,
            grid=(t // ROUTE_TOK,),
            in_specs=[pl.BlockSpec((ROUTE_TOK * TOP_K,), lambda i, last, nu: (i,), memory_space=pltpu.SMEM),
                      pl.BlockSpec(memory_space=pl.ANY)],
            out_specs=pl.BlockSpec(memory_space=pl.ANY),
            scratch_shapes=[pltpu.VMEM((MOE_TILE, d), F32), pltpu.SemaphoreType.DMA(()),
                            pltpu.SemaphoreType.DMA(())]),
        out_shape=jax.ShapeDtypeStruct((n_rows, d), F32),
        compiler_params=_cparams(("arbitrary",)),
        name="moe_dispatch",
    )(last_tile, n_used, dest_flat, h)


def _experts_kernel(te_ref, nu_ref, xs_ref, wgu_ref, bgu_ref, wd_ref, bd_ref, y_ref):
    @pl.when(pl.program_id(0) >= nu_ref[0])
    def _():
        y_ref[...] = jnp.zeros_like(y_ref)

    @pl.when(pl.program_id(0) < nu_ref[0])
    def _():
        gu = jnp.dot(xs_ref[...].astype(BF16), wgu_ref[0], preferred_element_type=F32) + bgu_ref[0]
        gate = jnp.minimum(gu[:, :D_FF], SWIGLU_LIMIT)
        up = jnp.clip(gu[:, D_FF:], -SWIGLU_LIMIT, SWIGLU_LIMIT)
        hid = (up + 1.0) * gate * jax.nn.sigmoid(SWIGLU_ALPHA * gate)
        y_ref[...] = jnp.dot(hid.astype(BF16), wd_ref[0], preferred_element_type=F32) + bd_ref[0]


def _experts(tile_expert, n_used, xs, wgu, bgu, wd, bd):
    n_rows, d = xs.shape
    n_exp = wgu.shape[0]
    rows = lambda i, te, nu: (jnp.minimum(i, nu[0] - 1), 0)
    per_e = lambda i, te, nu: (te[i], 0, 0)
    return pl.pallas_call(
        _experts_kernel,
        grid_spec=pltpu.PrefetchScalarGridSpec(
            num_scalar_prefetch=2,
            grid=(n_rows // MOE_TILE,),
            in_specs=[pl.BlockSpec((MOE_TILE, d), rows),
                      pl.BlockSpec((1, d, 2 * D_FF), per_e),
                      pl.BlockSpec((1, 1, 2 * D_FF), per_e),
                      pl.BlockSpec((1, D_FF, d), per_e),
                      pl.BlockSpec((1, 1, d), per_e)],
            out_specs=pl.BlockSpec((MOE_TILE, d), lambda i, te, nu: (i, 0))),
        out_shape=jax.ShapeDtypeStruct((n_rows, d), F32),
        compiler_params=_cparams(("arbitrary",)),
        name="moe_experts",
    )(tile_expert, n_used, xs, wgu, bgu.reshape(n_exp, 1, 2 * D_FF), wd, bd.reshape(n_exp, 1, d))


COMBINE_TOK = 128


def _combine_kernel(dcur_ref, dnext_ref, w_ref, h_ref, y_ref, lg_ref, lb_ref, h2_ref, h2b_ref,
                    ybuf_ref, sems, *, tiles_per_seq, front):
    i = pl.program_id(0)
    n = pl.num_programs(0)
    slot = lax.rem(i, 2)

    def issue(d_ref, s):
        def body(r, carry):
            for kk in range(TOP_K):
                d = d_ref[r * TOP_K + kk]
                pltpu.make_async_copy(y_ref.at[pl.ds(d, 1)], ybuf_ref.at[s, kk, pl.ds(r, 1)], sems.at[s]).start()
            return carry
        lax.fori_loop(0, COMBINE_TOK, body, 0)

    @pl.when(i == 0)
    def _():
        issue(dcur_ref, 0)

    @pl.when(i + 1 < n)
    def _():
        issue(dnext_ref, 1 - slot)

    for kk in range(TOP_K):
        pltpu.make_async_copy(y_ref.at[pl.ds(0, COMBINE_TOK)], ybuf_ref.at[slot, kk], sems.at[slot]).wait()

    w = w_ref[...]
    acc = w[:, 0:1] * ybuf_ref[slot, 0]
    for kk in range(1, TOP_K):
        acc = acc + w[:, kk:kk + 1] * ybuf_ref[slot, kk]
    y = _layer_norm(DEEPNORM_ALPHA * h_ref[...] + acc, lg_ref[...], lb_ref[...])
    pos = _seq_pos(i, tiles_per_seq, COMBINE_TOK)
    y = jnp.where(pos >= front, y, 0.0)
    h2_ref[...] = y
    h2b_ref[...] = y.astype(BF16)


def _combine(dest_flat, wts, h, y, lg, lb, *, lp, front):
    t, d = h.shape
    n = t // COMBINE_TOK
    blk = COMBINE_TOK * TOP_K
    row = lambda w: pl.BlockSpec((COMBINE_TOK, w), lambda i: (i, 0))
    const = pl.BlockSpec((1, d), lambda i: (0, 0))
    return pl.pallas_call(
        functools.partial(_combine_kernel, tiles_per_seq=lp // COMBINE_TOK, front=front),
        grid=(n,),
        in_specs=[pl.BlockSpec((blk,), lambda i: (i,), memory_space=pltpu.SMEM),
                  pl.BlockSpec((blk,), lambda i: (jnp.minimum(i + 1, n - 1),), memory_space=pltpu.SMEM),
                  row(LANES), row(d), pl.BlockSpec(memory_space=pl.ANY), const, const],
        out_specs=[row(d), row(d)],
        out_shape=[jax.ShapeDtypeStruct((t, d), F32), jax.ShapeDtypeStruct((t, d), BF16)],
        scratch_shapes=[pltpu.VMEM((2, TOP_K, COMBINE_TOK, d), F32), pltpu.SemaphoreType.DMA((2,))],
        compiler_params=_cparams(("arbitrary",)),
        name="moe_combine_ln2",
    )(dest_flat, dest_flat, wts, h, y, lg.reshape(1, d), lb.reshape(1, d))


def _routing_plan(ids, rank, counts, n_tiles):
    n_exp = counts.shape[0]
    padded = (counts + MOE_TILE - 1) // MOE_TILE * MOE_TILE
    ends = jnp.cumsum(padded)
    starts = ends - padded
    dest = (jnp.take(starts, ids) + rank).reshape(-1)
    n_used = (ends[-1] // MOE_TILE).reshape(1)
    tile_start = jnp.arange(n_tiles, dtype=jnp.int32) * MOE_TILE
    tile_expert = jnp.sum((tile_start[:, None] >= ends[None, :]).astype(jnp.int32), axis=1)
    last_used_expert = jnp.max(jnp.where(padded > 0, jnp.arange(n_exp, dtype=jnp.int32), 0))
    tile_expert = jnp.minimum(tile_expert, last_used_expert)
    last_tile = jnp.where(padded > 0, ends // MOE_TILE - 1, -1)
    return dest.astype(jnp.int32), tile_expert.astype(jnp.int32), n_used.astype(jnp.int32), last_tile.astype(jnp.int32)


def _rope_tables(lp, front):
    pos = (jnp.arange(lp, dtype=jnp.int32) - front).astype(F32)
    inv = ROPE_THETA ** (-jnp.arange(0, ATTN_HEAD_DIM, 2, dtype=F32) / ATTN_HEAD_DIM)
    ang = pos[:, None] * inv[None, :]
    ang = jnp.concatenate([ang, ang, ang, ang], axis=-1)
    return jnp.cos(ang), jnp.sin(ang)


def kernel(x, meta_tokens, ln_in_g, ln_in_b, w_in, ssm_lambda_re, ssm_lambda_im, ssm_log_dt, ssm_b_re, ssm_b_im, ssm_c_re, ssm_c_im, ssm_d, ssm_w_glu, ssm_w_out, conv_w, conv_w_out, attn_lambda_q1, attn_lambda_k1, attn_lambda_q2, attn_lambda_k2, attn_subln_g, attn_w_out, gate_w, gate_b, w_o, ln1_g, ln1_b, router_w, router_b, expert_w_gu, expert_b_gu, expert_w_down, expert_b_down, ln2_g, ln2_b):
    bsz, seq, d = x.shape
    assert d == D_MODEL and seq % ATT_BLOCK == 0
    lp = seq + ATT_BLOCK
    front = ATT_BLOCK - N_META
    t = bsz * lp

    meta = jnp.broadcast_to(meta_tokens.astype(x.dtype)[None], (bsz, N_META, d))
    hcat = jnp.concatenate([jnp.zeros((bsz, front, d), x.dtype), meta, x], axis=1).reshape(t, d)
    h, hb = _ln_in(hcat, ln_in_g, ln_in_b, lp=lp, front=front)
    cos, sin = _rope_tables(lp, front)

    s3 = SSM_WIDTH + 3 * CONV_WIDTH
    s5 = s3 + 2 * ATTN_WIDTH
    for l in range(DEPTH):
        w_in_b = w_in[l].astype(BF16)
        u, cb, z = _proj_sc(hb, w_in_b[:, :s3])
        q, k = _proj_qk(hb, w_in_b[:, s3:s5], cos, sin, lp=lp)
        vt = _proj_vt(hb, w_in_b[:, s5:].T)
        g = _proj_gate(hb, gate_w[l].astype(BF16), gate_b[l])

        bdb, bdc, pw = _ssm_tables(ssm_lambda_re[l], ssm_lambda_im[l], ssm_log_dt[l], ssm_b_re[l], ssm_b_im[l],
                                   ssm_c_re[l], ssm_c_im[l])
        ys = _ssm_branch(u, bdb, bdc, pw, ssm_d[l].astype(F32), ssm_w_glu[l].astype(BF16),
                         ssm_w_out[l].astype(BF16), bsz=bsz, lp=lp)
        yc = _conv_branch(cb, z, conv_w[l].astype(F32), conv_w_out[l].astype(BF16))

        lam_init = 0.8 - 0.6 * math.exp(-0.3 * l)
        lam = (jnp.exp(jnp.sum(attn_lambda_q1[l].astype(F32) * attn_lambda_k1[l].astype(F32)))
               - jnp.exp(jnp.sum(attn_lambda_q2[l].astype(F32) * attn_lambda_k2[l].astype(F32)))
               + lam_init).reshape(1)
        g_col = jnp.broadcast_to(attn_subln_g[l].astype(F32)[:, None], (ATTN_VALUE_DIM, ATT_BLOCK))
        o = _attention(lam, q, k, vt, g_col, bsz=bsz, lp=lp, front=front, lam_init=lam_init)

        h, hb = _merge(h, g, ys, yc, o, attn_w_out[l].astype(BF16), w_o[l].astype(BF16), ln1_g[l], ln1_b[l],
                       lp=lp, front=front)

        ids, wts, rank, counts = _router(h, router_w[l].astype(F32), router_b[l].astype(F32))
        n_tiles = t * TOP_K // MOE_TILE + N_EXPERTS
        dest, tile_expert, n_used, last_tile = _routing_plan(ids[:, :TOP_K], rank[:, :TOP_K],
                                                             counts[0].astype(jnp.int32), n_tiles)
        xs = _dispatch(h, dest, last_tile, n_used, n_tiles * MOE_TILE)
        y = _experts(tile_expert, n_used, xs, expert_w_gu[l].astype(BF16), expert_b_gu[l].astype(F32),
                     expert_w_down[l].astype(BF16), expert_b_down[l].astype(F32))
        h, hb = _combine(dest, wts, h, y, ln2_g[l], ln2_b[l], lp=lp, front=front)

    return h.reshape(bsz, lp, d)[:, ATT_BLOCK:]
```

```python
import functools
import math

import jax
import jax.numpy as jnp
from jax import lax
from jax.experimental import pallas as pl
from jax.experimental.pallas import tpu as pltpu

F32 = jnp.float32
BF16 = jnp.bfloat16

D_MODEL = 1024
DEPTH = 2
CHUNK = 64
N_META = 16
SSM_WIDTH = 512
SSM_GROUP = 16
SSM_GROUPS = 32
SSM_STATE = 64
SSM_CH = SSM_GROUPS * SSM_STATE
CONV_WIDTH = 512
CONV_K = 3
ATTN_HEADS = 8
ATTN_HEAD_DIM = 64
ATTN_VALUE_DIM = 128
ATTN_WIDTH = 1024
ROPE_THETA = 10000.0
N_EXPERTS = 32
TOP_K = 4
D_FF = 1024
SWIGLU_LIMIT = 7.0
SWIGLU_ALPHA = 1.702
DEEPNORM_ALPHA = (2.0 * DEPTH) ** 0.25
LN_EPS = 1e-5
RMS_EPS = 1e-5
NEG_INF = -1e30

ATT_BLOCK = 256
LANES = 128
SUBLANES = 8
VMEM_LIMIT = 48 * 1024 * 1024


def _cparams(sem):
    return pltpu.CompilerParams(dimension_semantics=sem, vmem_limit_bytes=VMEM_LIMIT)


def _pick_tile(n, candidates):
    for c in candidates:
        if n % c == 0:
            return c
    raise ValueError(f"no tile for {n}")


def _layer_norm(x, g, b):
    mu = jnp.mean(x, axis=-1, keepdims=True)
    xc = x - mu
    var = jnp.mean(xc * xc, axis=-1, keepdims=True)
    return xc * lax.rsqrt(var + LN_EPS) * g + b


def _seq_pos(tile_idx, tiles_per_seq, tm):
    base = lax.rem(tile_idx, tiles_per_seq) * tm
    return base + lax.broadcasted_iota(jnp.int32, (tm, 1), 0)


def _ln_in_kernel(x_ref, g_ref, b_ref, h_ref, hb_ref, *, tiles_per_seq, front):
    tm = x_ref.shape[0]
    y = _layer_norm(x_ref[...], g_ref[...], b_ref[...])
    pos = _seq_pos(pl.program_id(0), tiles_per_seq, tm)
    y = jnp.where(pos >= front, y, 0.0)
    h_ref[...] = y
    hb_ref[...] = y.astype(BF16)


def _ln_in(hcat, g, b, *, lp, front):
    t, d = hcat.shape
    tm = _pick_tile(lp, (512, 256))
    return pl.pallas_call(
        functools.partial(_ln_in_kernel, tiles_per_seq=lp // tm, front=front),
        grid=(t // tm,),
        in_specs=[pl.BlockSpec((tm, d), lambda i: (i, 0)),
                  pl.BlockSpec((1, d), lambda i: (0, 0)),
                  pl.BlockSpec((1, d), lambda i: (0, 0))],
        out_specs=[pl.BlockSpec((tm, d), lambda i: (i, 0)),
                   pl.BlockSpec((tm, d), lambda i: (i, 0))],
        out_shape=[jax.ShapeDtypeStruct((t, d), F32), jax.ShapeDtypeStruct((t, d), BF16)],
        compiler_params=_cparams(("parallel",)),
        name="ln_in",
    )(hcat, g.reshape(1, d), b.reshape(1, d))


def _proj_sc_kernel(x_ref, w_ref, u_ref, cb_ref, z_ref):
    acc = jnp.dot(x_ref[...], w_ref[...], preferred_element_type=F32)
    s0, s1, s2, s3 = SSM_WIDTH, SSM_WIDTH + CONV_WIDTH, SSM_WIDTH + 2 * CONV_WIDTH, SSM_WIDTH + 3 * CONV_WIDTH
    u_ref[...] = acc[:, :s0]
    cb_ref[...] = acc[:, s0:s1]
    z_ref[...] = acc[:, s1:s2] * acc[:, s2:s3]


def _proj_sc(hb, w):
    t, d = hb.shape
    n = w.shape[1]
    tm = _pick_tile(t, (512, 256))
    out = jax.ShapeDtypeStruct((t, SSM_WIDTH), F32)
    ospec = pl.BlockSpec((tm, SSM_WIDTH), lambda i: (i, 0))
    return pl.pallas_call(
        _proj_sc_kernel,
        grid=(t // tm,),
        in_specs=[pl.BlockSpec((tm, d), lambda i: (i, 0)),
                  pl.BlockSpec((d, n), lambda i: (0, 0))],
        out_specs=[ospec, ospec, ospec],
        out_shape=[out, out, out],
        compiler_params=_cparams(("parallel",)),
        name="proj_ssm_conv",
    )(hb, w)


def _proj_qk_kernel(x_ref, w_ref, cos_ref, sin_ref, q_ref, k_ref):
    acc = jnp.dot(x_ref[...], w_ref[...], preferred_element_type=F32)
    cos = cos_ref[...]
    sin = sin_ref[...]
    lane = lax.broadcasted_iota(jnp.int32, (1, LANES), 1)
    low_half = lax.rem(lane, ATTN_HEAD_DIM) < ATTN_HEAD_DIM // 2
    nblk = ATTN_WIDTH // LANES
    for blk in range(2 * nblk):
        a = acc[:, blk * LANES:(blk + 1) * LANES]
        rot = jnp.where(low_half, -pltpu.roll(a, LANES - ATTN_HEAD_DIM // 2, 1),
                        pltpu.roll(a, ATTN_HEAD_DIM // 2, 1))
        r = a * cos + rot * sin
        if blk < nblk:
            q_ref[:, blk * LANES:(blk + 1) * LANES] = (r * (ATTN_HEAD_DIM ** -0.5 * math.log2(math.e))).astype(BF16)
        else:
            k_ref[:, (blk - nblk) * LANES:(blk - nblk + 1) * LANES] = r.astype(BF16)


def _proj_qk(hb, w, cos, sin, *, lp):
    t, d = hb.shape
    tm = _pick_tile(lp, (512, 256))
    tps = lp // tm
    out = jax.ShapeDtypeStruct((t, ATTN_WIDTH), BF16)
    return pl.pallas_call(
        _proj_qk_kernel,
        grid=(t // tm,),
        in_specs=[pl.BlockSpec((tm, d), lambda i: (i, 0)),
                  pl.BlockSpec((d, 2 * ATTN_WIDTH), lambda i: (0, 0)),
                  pl.BlockSpec((tm, LANES), lambda i: (i % tps, 0)),
                  pl.BlockSpec((tm, LANES), lambda i: (i % tps, 0))],
        out_specs=[pl.BlockSpec((tm, ATTN_WIDTH), lambda i: (i, 0)),
                   pl.BlockSpec((tm, ATTN_WIDTH), lambda i: (i, 0))],
        out_shape=[out, out],
        compiler_params=_cparams(("parallel",)),
        name="proj_qk_rope",
    )(hb, w, cos, sin)


def _proj_vt_kernel(wt_ref, x_ref, o_ref):
    vt = lax.dot_general(wt_ref[...], x_ref[...], (((1,), (1,)), ((), ())),
                         preferred_element_type=F32).astype(BF16)
    for c in range(o_ref.shape[0]):
        o_ref[c] = vt[:, c * ATT_BLOCK:(c + 1) * ATT_BLOCK]


def _proj_vt(hb, wt):
    t, d = hb.shape
    tm = _pick_tile(t, (512, 256))
    per = tm // ATT_BLOCK
    return pl.pallas_call(
        _proj_vt_kernel,
        grid=(t // tm,),
        in_specs=[pl.BlockSpec((ATTN_WIDTH, d), lambda i: (0, 0)),
                  pl.BlockSpec((tm, d), lambda i: (i, 0))],
        out_specs=pl.BlockSpec((per, ATTN_WIDTH, ATT_BLOCK), lambda i: (i, 0, 0)),
        out_shape=jax.ShapeDtypeStruct((t // ATT_BLOCK, ATTN_WIDTH, ATT_BLOCK), BF16),
        compiler_params=_cparams(("parallel",)),
        name="proj_v_t",
    )(wt, hb)


def _proj_gate_kernel(x_ref, w_ref, b_ref, o_ref):
    acc = jnp.dot(x_ref[...], w_ref[...], preferred_element_type=F32) + b_ref[...]
    o_ref[...] = jax.nn.sigmoid(acc).astype(BF16)


def _proj_gate(hb, w, b):
    t, d = hb.shape
    n = w.shape[1]
    tm = _pick_tile(t, (512, 256))
    tn = 1024
    return pl.pallas_call(
        _proj_gate_kernel,
        grid=(t // tm, n // tn),
        in_specs=[pl.BlockSpec((tm, d), lambda i, j: (i, 0)),
                  pl.BlockSpec((d, tn), lambda i, j: (0, j)),
                  pl.BlockSpec((1, tn), lambda i, j: (0, j))],
        out_specs=pl.BlockSpec((tm, tn), lambda i, j: (i, j)),
        out_shape=jax.ShapeDtypeStruct((t, n), BF16),
        compiler_params=_cparams(("parallel", "parallel")),
        name="proj_gate",
    )(hb, w, b.reshape(1, n))


SCAN_STRIP = 512


def _ssm_kernel(u_ref, bdb_ref, bdc_ref, pw_ref, dskip_ref, wglu_ref, wout_ref, y_ref, bu_ref, carry_ref):
    tm = u_ref.shape[0]

    @pl.when(pl.program_id(1) == 0)
    def _():
        carry_ref[...] = jnp.zeros_like(carry_ref)

    u = u_ref[...]
    bu_ref[...] = jnp.dot(u.astype(BF16), bdb_ref[...], preferred_element_type=F32)

    row = lax.broadcasted_iota(jnp.int32, (SUBLANES, SCAN_STRIP), 0)
    for s in range(SSM_CH // SCAN_STRIP):
        re = slice(s * SCAN_STRIP, (s + 1) * SCAN_STRIP)
        im = slice(SSM_CH + s * SCAN_STRIP, SSM_CH + (s + 1) * SCAN_STRIP)
        steps = []
        for k, d in enumerate((1, 2, 4)):
            steps.append((d, jnp.broadcast_to(pw_ref[k:k + 1, re], (SUBLANES, SCAN_STRIP)),
                          jnp.broadcast_to(pw_ref[k:k + 1, im], (SUBLANES, SCAN_STRIP))))
        pr = pw_ref[SUBLANES:2 * SUBLANES, re]
        pi = pw_ref[SUBLANES:2 * SUBLANES, im]

        def group(gi, carry):
            cr, ci = carry
            r0 = pl.multiple_of(gi * SUBLANES, SUBLANES)
            xr = bu_ref[pl.ds(r0, SUBLANES), re]
            xi = bu_ref[pl.ds(r0, SUBLANES), im]
            for d, ar, ai in steps:
                sr = jnp.where(row >= d, pltpu.roll(xr, d, 0), 0.0)
                si = jnp.where(row >= d, pltpu.roll(xi, d, 0), 0.0)
                xr, xi = xr + ar * sr - ai * si, xi + ar * si + ai * sr
            xr, xi = xr + pr * cr - pi * ci, xi + pr * ci + pi * cr
            bu_ref[pl.ds(r0, SUBLANES), re] = xr
            bu_ref[pl.ds(r0, SUBLANES), im] = xi
            last_r = jnp.broadcast_to(xr[SUBLANES - 1:SUBLANES, :], (SUBLANES, SCAN_STRIP))
            last_i = jnp.broadcast_to(xi[SUBLANES - 1:SUBLANES, :], (SUBLANES, SCAN_STRIP))
            return last_r, last_i

        cr, ci = lax.fori_loop(0, tm // SUBLANES, group, (carry_ref[:, re], carry_ref[:, im]))
        carry_ref[:, re] = cr
        carry_ref[:, im] = ci

    y = jnp.dot(bu_ref[...].astype(BF16), bdc_ref[...], preferred_element_type=F32) + dskip_ref[...] * u
    y = jax.nn.gelu(y)
    y = y * jax.nn.sigmoid(jnp.dot(y.astype(BF16), wglu_ref[...], preferred_element_type=F32))
    y_ref[...] = jnp.dot(y.astype(BF16), wout_ref[...], preferred_element_type=F32)


def _ssm_branch(u, bdb, bdc, pw, dskip, wglu, wout, *, bsz, lp):
    t = u.shape[0]
    tm = _pick_tile(lp, (256,))
    tps = lp // tm
    const = lambda shape: pl.BlockSpec(shape, lambda b, i: (0, 0))
    return pl.pallas_call(
        _ssm_kernel,
        grid=(bsz, tps),
        in_specs=[pl.BlockSpec((tm, SSM_WIDTH), lambda b, i: (b * tps + i, 0)),
                  const(bdb.shape), const(bdc.shape), const(pw.shape), const((1, SSM_WIDTH)),
                  const(wglu.shape), const(wout.shape)],
        out_specs=pl.BlockSpec((tm, D_MODEL), lambda b, i: (b * tps + i, 0)),
        out_shape=jax.ShapeDtypeStruct((t, D_MODEL), F32),
        scratch_shapes=[pltpu.VMEM((tm, 2 * SSM_CH), F32), pltpu.VMEM((SUBLANES, 2 * SSM_CH), F32)],
        compiler_params=_cparams(("arbitrary", "arbitrary")),
        name="ssm_branch",
    )(u, bdb, bdc, pw, dskip.reshape(1, SSM_WIDTH), wglu, wout)


def _ssm_tables(lam_re, lam_im, log_dt, b_re, b_im, c_re, c_im):
    lr, li = lam_re.astype(F32), lam_im.astype(F32)
    dt = jnp.exp(log_dt.astype(F32))[:, None]
    mag = jnp.exp(lr * dt)
    ar = mag * jnp.cos(li * dt)
    ai = mag * jnp.sin(li * dt)
    denom = lr * lr + li * li
    nr, ni = ar - 1.0, ai
    coef_r = (nr * lr + ni * li) / denom
    coef_i = (ni * lr - nr * li) / denom
    br, bi = b_re.astype(F32), b_im.astype(F32)
    bbar_r = coef_r[..., None] * br - coef_i[..., None] * bi
    bbar_i = coef_r[..., None] * bi + coef_i[..., None] * br
    eye = jnp.eye(SSM_GROUPS, dtype=F32)
    bd_in = lambda m: jnp.einsum('gpc,gh->gchp', m, eye).reshape(SSM_WIDTH, SSM_CH)
    bdb = jnp.concatenate([bd_in(bbar_r), bd_in(bbar_i)], axis=1)
    bd_out = lambda m: jnp.einsum('gcp,gh->gphc', m, eye).reshape(SSM_CH, SSM_WIDTH)
    bdc = jnp.concatenate([bd_out(c_re.astype(F32)), -bd_out(c_im.astype(F32))], axis=0)
    a1 = (ar.reshape(-1), ai.reshape(-1))
    cmul = lambda x, y: (x[0] * y[0] - x[1] * y[1], x[0] * y[1] + x[1] * y[0])
    pows = [a1]
    for _ in range(SUBLANES - 1):
        pows.append(cmul(pows[-1], a1))
    rows = [pows[0], pows[1], pows[3]] + [(jnp.zeros_like(a1[0]),) * 2] * (SUBLANES - 3) + pows
    pw = jnp.stack([jnp.concatenate(r) for r in rows])
    return bdb.astype(BF16), bdc.astype(BF16), pw


def _conv_kernel(cb_ref, z_ref, zprev_ref, w_ref, wout_ref, y_ref):
    tm = z_ref.shape[0]
    z = z_ref[...]
    zz = jnp.concatenate([zprev_ref[...], z], axis=0)
    w = w_ref[...]
    y = w[2:3, :] * z
    for j in range(CONV_K - 1):
        shift = CONV_K - 1 - j
        y = y + w[j:j + 1, :] * zz[SUBLANES - shift:SUBLANES - shift + tm, :]
    y = cb_ref[...] * y
    y_ref[...] = jnp.dot(y.astype(BF16), wout_ref[...], preferred_element_type=F32)


def _conv_branch(cb, z, w, wout):
    t = z.shape[0]
    tm = _pick_tile(t, (512, 256))
    per = tm // SUBLANES
    return pl.pallas_call(
        _conv_kernel,
        grid=(t // tm,),
        in_specs=[pl.BlockSpec((tm, CONV_WIDTH), lambda i: (i, 0)),
                  pl.BlockSpec((tm, CONV_WIDTH), lambda i: (i, 0)),
                  pl.BlockSpec((SUBLANES, CONV_WIDTH), lambda i: (jnp.maximum(i * per - 1, 0), 0)),
                  pl.BlockSpec((SUBLANES, CONV_WIDTH), lambda i: (0, 0)),
                  pl.BlockSpec((CONV_WIDTH, D_MODEL), lambda i: (0, 0))],
        out_specs=pl.BlockSpec((tm, D_MODEL), lambda i: (i, 0)),
        out_shape=jax.ShapeDtypeStruct((t, D_MODEL), F32),
        compiler_params=_cparams(("parallel",)),
        name="conv_branch",
    )(cb, z, z, jnp.pad(w, ((0, SUBLANES - CONV_K), (0, 0))), wout)


ATT_SUB = 3


def _attn_kernel(lam_ref, q_ref, k_ref, vt_ref, g_ref, o_ref,
                 qlo_ref, qhi_ref, m_ref, l_ref, acc_ref, *, front, lam_init):
    sb = pl.program_id(2)
    blk = ATT_BLOCK
    real0 = front + N_META
    nt = (((1,), (1,)), ((), ()))

    q = q_ref[...]
    lane = lax.broadcasted_iota(jnp.int32, q.shape, 1)
    qlo_ref[...] = jnp.where(lane < ATTN_HEAD_DIM, q, jnp.zeros_like(q))
    qhi_ref[...] = jnp.where(lane >= ATTN_HEAD_DIM, q, jnp.zeros_like(q))
    m_ref[...] = jnp.full_like(m_ref, NEG_INF)
    l_ref[...] = jnp.zeros_like(l_ref)
    acc_ref[...] = jnp.zeros_like(acc_ref)

    def pair(a, j, k, vt, masked):
        rows = slice(a * blk, (a + 1) * blk)
        scores = [lax.dot_general(k, qref[rows, :], nt, preferred_element_type=F32)
                  for qref in (qlo_ref, qhi_ref)]
        if masked:
            pk = j * blk + lax.broadcasted_iota(jnp.int32, (blk, 1), 0)
            pq = (sb * ATT_SUB + a) * blk + lax.broadcasted_iota(jnp.int32, (1, blk), 1)
            cid_k = jnp.where(pk < front, 1 << 30, jnp.where(pk < real0, 0, 1 + (pk - real0) // CHUNK))
            cid_q = jnp.where(pq < real0, 0, 1 + (pq - real0) // CHUNK)
            vis = cid_k <= cid_q
            scores = [jnp.where(vis, s, NEG_INF) for s in scores]
        for idx, s in enumerate(scores):
            m_old = m_ref[a, idx]
            m_new = jnp.maximum(m_old, jnp.max(s, axis=0, keepdims=True))
            alpha = jnp.exp2(m_old - m_new)
            p = jnp.exp2(s - m_new)
            l_ref[a, idx] = alpha * l_ref[a, idx] + jnp.sum(p, axis=0, keepdims=True)
            m_ref[a, idx] = m_new
            acc_ref[a, idx] = alpha * acc_ref[a, idx] + jnp.dot(vt, p.astype(BF16), preferred_element_type=F32)

    def kv_block(j):
        return k_ref[pl.ds(pl.multiple_of(j * blk, blk), blk), :], vt_ref[j]

    @pl.when(sb > 0)
    def _():
        k, vt = kv_block(0)
        for a in range(ATT_SUB):
            pair(a, 0, k, vt, True)

    def body(j, carry):
        k, vt = kv_block(j)
        for a in range(ATT_SUB):
            pair(a, j, k, vt, False)
        return carry

    lax.fori_loop(1, sb * ATT_SUB, body, 0)

    for dj in range(ATT_SUB):
        j = sb * ATT_SUB + dj
        k, vt = kv_block(j)
        for a in range(dj, ATT_SUB):
            pair(a, j, k, vt, True)

    lam = lam_ref[0]
    for a in range(ATT_SUB):
        o = acc_ref[a, 0] / l_ref[a, 0] - lam * (acc_ref[a, 1] / l_ref[a, 1])
        o = o * lax.rsqrt(jnp.mean(o * o, axis=0, keepdims=True) + RMS_EPS)
        o = o * g_ref[...] * (1.0 - lam_init)
        o_ref[a * blk:(a + 1) * blk, :] = o.T.astype(BF16)


def _attention(lam, q, k, vt3, g_col, *, bsz, lp, front, lam_init):
    t = q.shape[0]
    blk = ATT_BLOCK
    nb = lp // blk
    assert nb % ATT_SUB == 0
    nsb = nb // ATT_SUB
    tq = ATT_SUB * blk
    return pl.pallas_call(
        functools.partial(_attn_kernel, front=front, lam_init=lam_init),
        grid=(bsz, ATTN_HEADS, nsb),
        in_specs=[pl.BlockSpec(memory_space=pltpu.SMEM),
                  pl.BlockSpec((tq, LANES), lambda b, h, i: (b * nsb + i, h)),
                  pl.BlockSpec((lp, LANES), lambda b, h, i: (b, h)),
                  pl.BlockSpec((nb, LANES, blk), lambda b, h, i: (b, h, 0)),
                  pl.BlockSpec((LANES, blk), lambda b, h, i: (0, 0))],
        out_specs=pl.BlockSpec((tq, LANES), lambda b, h, i: (b * nsb + i, h)),
        out_shape=jax.ShapeDtypeStruct((t, ATTN_WIDTH), BF16),
        scratch_shapes=[pltpu.VMEM((tq, LANES), BF16), pltpu.VMEM((tq, LANES), BF16),
                        pltpu.VMEM((ATT_SUB, 2, 1, blk), F32), pltpu.VMEM((ATT_SUB, 2, 1, blk), F32),
                        pltpu.VMEM((ATT_SUB, 2, LANES, blk), F32)],
        compiler_params=_cparams(("parallel", "parallel", "arbitrary")),
        name="diff_attention",
    )(lam, q, k, vt3, g_col)


def _merge_kernel(h_ref, g_ref, ys_ref, yc_ref, o_ref, wao_ref, wo_ref, lg_ref, lb_ref,
                  h1_ref, h1b_ref, *, tiles_per_seq, front):
    tm = h_ref.shape[0]
    ya = jnp.dot(o_ref[...], wao_ref[...], preferred_element_type=F32)
    g = g_ref[...].astype(F32)
    merged = g[:, :D_MODEL] * ys_ref[...] + g[:, D_MODEL:2 * D_MODEL] * yc_ref[...] + g[:, 2 * D_MODEL:] * ya
    r = DEEPNORM_ALPHA * h_ref[...] + jnp.dot(merged.astype(BF16), wo_ref[...], preferred_element_type=F32)
    y = _layer_norm(r, lg_ref[...], lb_ref[...])
    pos = _seq_pos(pl.program_id(0), tiles_per_seq, tm)
    y = jnp.where(pos >= front, y, 0.0)
    h1_ref[...] = y
    h1b_ref[...] = y.astype(BF16)


def _merge(h, g, ys, yc, o, wao, wo, lg, lb, *, lp, front):
    t, d = h.shape
    tm = _pick_tile(lp, (256,))
    row = lambda w: pl.BlockSpec((tm, w), lambda i: (i, 0))
    const = lambda a, b: pl.BlockSpec((a, b), lambda i: (0, 0))
    return pl.pallas_call(
        functools.partial(_merge_kernel, tiles_per_seq=lp // tm, front=front),
        grid=(t // tm,),
        in_specs=[row(d), row(3 * d), row(d), row(d), row(ATTN_WIDTH),
                  const(ATTN_WIDTH, d), const(d, d), const(1, d), const(1, d)],
        out_specs=[row(d), row(d)],
        out_shape=[jax.ShapeDtypeStruct((t, d), F32), jax.ShapeDtypeStruct((t, d), BF16)],
        compiler_params=_cparams(("parallel",)),
        name="merge_ln1",
    )(h, g, ys, yc, o, wao, wo, lg.reshape(1, d), lb.reshape(1, d))


def _router_kernel(h_ref, w_ref, b_ref, ids_ref, wts_ref, rank_ref, cnt_ref, base_ref):
    tm = h_ref.shape[0]

    @pl.when(pl.program_id(0) == 0)
    def _():
        base_ref[...] = jnp.zeros_like(base_ref)

    logits = jnp.dot(h_ref[...], w_ref[...], preferred_element_type=F32,
                     precision=lax.Precision.HIGHEST) + b_ref[...]
    lane = lax.broadcasted_iota(jnp.int32, logits.shape, 1).astype(F32)
    work = logits
    sels, vals, firsts = [], [], []
    for _ in range(TOP_K):
        mx = jnp.max(work, axis=1, keepdims=True)
        first = jnp.min(jnp.where(work == mx, lane, float(N_EXPERTS)), axis=1, keepdims=True)
        sel = lane == first
        sels.append(sel)
        vals.append(mx)
        firsts.append(first)
        work = jnp.where(sel, -jnp.inf, work)
    exps = [jnp.exp(v - vals[0]) for v in vals]
    den = exps[0] + exps[1] + exps[2] + exps[3]

    onehot = jnp.zeros_like(logits)
    for sel in sels:
        onehot = onehot + jnp.where(sel, 1.0, 0.0)
    r = lax.broadcasted_iota(jnp.int32, (tm, tm), 0)
    c = lax.broadcasted_iota(jnp.int32, (tm, tm), 1)
    tri = jnp.where(r > c, 1.0, 0.0).astype(BF16)
    before = jnp.dot(tri, onehot.astype(BF16), preferred_element_type=F32) + base_ref[...]
    base_ref[...] = base_ref[...] + jnp.sum(onehot, axis=0, keepdims=True)
    cnt_ref[...] = base_ref[...]

    out_lane = lax.broadcasted_iota(jnp.int32, (tm, LANES), 1)
    ids = jnp.zeros((tm, LANES), F32)
    wts = jnp.zeros((tm, LANES), F32)
    rank = jnp.zeros((tm, LANES), F32)
    for kk in range(TOP_K):
        rk = jnp.sum(jnp.where(sels[kk], before, 0.0), axis=1, keepdims=True)
        ids = jnp.where(out_lane == kk, firsts[kk], ids)
        wts = jnp.where(out_lane == kk, exps[kk] / den, wts)
        rank = jnp.where(out_lane == kk, rk, rank)
    ids_ref[...] = ids.astype(jnp.int32)
    wts_ref[...] = wts
    rank_ref[...] = rank.astype(jnp.int32)


def _router(h, w, b):
    t, d = h.shape
    tm = _pick_tile(t, (512, 256))
    wide = lambda dt: jax.ShapeDtypeStruct((t, LANES), dt)
    row = pl.BlockSpec((tm, LANES), lambda i: (i, 0))
    return pl.pallas_call(
        _router_kernel,
        grid=(t // tm,),
        in_specs=[pl.BlockSpec((tm, d), lambda i: (i, 0)),
                  pl.BlockSpec((d, N_EXPERTS), lambda i: (0, 0)),
                  pl.BlockSpec((1, N_EXPERTS), lambda i: (0, 0))],
        out_specs=[row, row, row, pl.BlockSpec((1, N_EXPERTS), lambda i: (0, 0))],
        out_shape=[wide(jnp.int32), wide(F32), wide(jnp.int32), jax.ShapeDtypeStruct((1, N_EXPERTS), F32)],
        scratch_shapes=[pltpu.VMEM((1, N_EXPERTS), F32)],
        compiler_params=_cparams(("arbitrary",)),
        name="moe_router",
    )(h, w, b.reshape(1, N_EXPERTS))


MOE_TILE = 256
ROUTE_TOK = 256


def _dispatch_kernel(last_ref, nu_ref, dest_ref, h_ref, xs_ref, zero_ref, stage_ref, zsem, sems):
    i = pl.program_id(0)
    n = pl.num_programs(0)
    n_tiles = xs_ref.shape[0] // MOE_TILE

    def fill(tile):
        return pltpu.make_async_copy(zero_ref, xs_ref.at[pl.ds(tile * MOE_TILE, MOE_TILE)], zsem)

    @pl.when(i == 0)
    def _():
        zero_ref[...] = jnp.zeros_like(zero_ref)
        for e in range(N_EXPERTS):
            @pl.when(last_ref[e] >= 0)
            def _():
                fill(last_ref[e]).start()

        def start_unused(tile, carry):
            fill(tile).start()
            return carry

        def wait_unused(tile, carry):
            fill(tile).wait()
            return carry

        lax.fori_loop(nu_ref[0], n_tiles, start_unused, 0)
        for e in range(N_EXPERTS):
            @pl.when(last_ref[e] >= 0)
            def _():
                fill(last_ref[e]).wait()
        lax.fori_loop(nu_ref[0], n_tiles, wait_unused, 0)

    slot = lax.rem(i, 2)
    stage_ref[slot] = h_ref[...]

    def issue(r, carry):
        for kk in range(TOP_K):
            d = dest_ref[r * TOP_K + kk]
            pltpu.make_async_copy(stage_ref.at[slot, pl.ds(r, 1)], xs_ref.at[pl.ds(d, 1)], sems.at[slot]).start()
        return carry

    lax.fori_loop(0, ROUTE_TOK, issue, 0)

    def drain(s):
        for _ in range(TOP_K):
            pltpu.make_async_copy(stage_ref.at[s], xs_ref.at[pl.ds(0, ROUTE_TOK)], sems.at[s]).wait()

    @pl.when(i > 0)
    def _():
        drain(1 - slot)

    @pl.when(i == n - 1)
    def _():
        drain(slot)


def _dispatch(h, dest_flat, last_tile, n_used, n_rows):
    t, d = h.shape
    return pl.pallas_call(
        _dispatch_kernel,
        grid_spec=pltpu.PrefetchScalarGridSpec(
            num_scalar_prefetch=2,
            grid=(t // ROUTE_TOK,),
            in_specs=[pl.BlockSpec((ROUTE_TOK * TOP_K,), lambda i, last, nu: (i,), memory_space=pltpu.SMEM),
                      pl.BlockSpec((ROUTE_TOK, d), lambda i, last, nu: (i, 0))],
            out_specs=pl.BlockSpec(memory_space=pl.ANY),
            scratch_shapes=[pltpu.VMEM((MOE_TILE, d), F32), pltpu.VMEM((2, ROUTE_TOK, d), F32),
                            pltpu.SemaphoreType.DMA(()), pltpu.SemaphoreType.DMA((2,))]),
        out_shape=jax.ShapeDtypeStruct((n_rows, d), F32),
        compiler_params=_cparams(("arbitrary",)),
        name="moe_dispatch",
    )(last_tile, n_used, dest_flat, h)


def _experts_kernel(te_ref, nu_ref, xs_ref, wgu_ref, bgu_ref, wd_ref, bd_ref, y_ref):
    @pl.when(pl.program_id(0) >= nu_ref[0])
    def _():
        y_ref[...] = jnp.zeros_like(y_ref)

    @pl.when(pl.program_id(0) < nu_ref[0])
    def _():
        gu = jnp.dot(xs_ref[...].astype(BF16), wgu_ref[0], preferred_element_type=F32) + bgu_ref[0]
        gate = jnp.minimum(gu[:, :D_FF], SWIGLU_LIMIT)
        up = jnp.clip(gu[:, D_FF:], -SWIGLU_LIMIT, SWIGLU_LIMIT)
        hid = (up + 1.0) * gate * jax.nn.sigmoid(SWIGLU_ALPHA * gate)
        y_ref[...] = jnp.dot(hid.astype(BF16), wd_ref[0], preferred_element_type=F32) + bd_ref[0]


def _experts(tile_expert, n_used, xs, wgu, bgu, wd, bd):
    n_rows, d = xs.shape
    n_exp = wgu.shape[0]
    rows = lambda i, te, nu: (jnp.minimum(i, nu[0] - 1), 0)
    per_e = lambda i, te, nu: (te[i], 0, 0)
    return pl.pallas_call(
        _experts_kernel,
        grid_spec=pltpu.PrefetchScalarGridSpec(
            num_scalar_prefetch=2,
            grid=(n_rows // MOE_TILE,),
            in_specs=[pl.BlockSpec((MOE_TILE, d), rows),
                      pl.BlockSpec((1, d, 2 * D_FF), per_e),
                      pl.BlockSpec((1, 1, 2 * D_FF), per_e),
                      pl.BlockSpec((1, D_FF, d), per_e),
                      pl.BlockSpec((1, 1, d), per_e)],
            out_specs=pl.BlockSpec((MOE_TILE, d), lambda i, te, nu: (i, 0))),
        out_shape=jax.ShapeDtypeStruct((n_rows, d), F32),
        compiler_params=_cparams(("arbitrary",)),
        name="moe_experts",
    )(tile_expert, n_used, xs, wgu, bgu.reshape(n_exp, 1, 2 * D_FF), wd, bd.reshape(n_exp, 1, d))


COMBINE_TOK = 128


def _combine_kernel(dcur_ref, dnext_ref, w_ref, h_ref, y_ref, lg_ref, lb_ref, h2_ref, h2b_ref,
                    ybuf_ref, sems, *, tiles_per_seq, front):
    i = pl.program_id(0)
    n = pl.num_programs(0)
    slot = lax.rem(i, 2)

    def issue(d_ref, s):
        def body(r, carry):
            for kk in range(TOP_K):
                d = d_ref[r * TOP_K + kk]
                pltpu.make_async_copy(y_ref.at[pl.ds(d, 1)], ybuf_ref.at[s, kk, pl.ds(r, 1)], sems.at[s]).start()
            return carry
        lax.fori_loop(0, COMBINE_TOK, body, 0)

    @pl.when(i == 0)
    def _():
        issue(dcur_ref, 0)

    @pl.when(i + 1 < n)
    def _():
        issue(dnext_ref, 1 - slot)

    for kk in range(TOP_K):
        pltpu.make_async_copy(y_ref.at[pl.ds(0, COMBINE_TOK)], ybuf_ref.at[slot, kk], sems.at[slot]).wait()

    w = w_ref[...]
    acc = w[:, 0:1] * ybuf_ref[slot, 0]
    for kk in range(1, TOP_K):
        acc = acc + w[:, kk:kk + 1] * ybuf_ref[slot, kk]
    y = _layer_norm(DEEPNORM_ALPHA * h_ref[...] + acc, lg_ref[...], lb_ref[...])
    pos = _seq_pos(i, tiles_per_seq, COMBINE_TOK)
    y = jnp.where(pos >= front, y, 0.0)
    h2_ref[...] = y
    h2b_ref[...] = y.astype(BF16)


def _combine(dest_flat, wts, h, y, lg, lb, *, lp, front):
    t, d = h.shape
    n = t // COMBINE_TOK
    blk = COMBINE_TOK * TOP_K
    row = lambda w: pl.BlockSpec((COMBINE_TOK, w), lambda i: (i, 0))
    const = pl.BlockSpec((1, d), lambda i: (0, 0))
    return pl.pallas_call(
        functools.partial(_combine_kernel, tiles_per_seq=lp // COMBINE_TOK, front=front),
        grid=(n,),
        in_specs=[pl.BlockSpec((blk,), lambda i: (i,), memory_space=pltpu.SMEM),
                  pl.BlockSpec((blk,), lambda i: (jnp.minimum(i + 1, n - 1),), memory_space=pltpu.SMEM),
                  row(LANES), row(d), pl.BlockSpec(memory_space=pl.ANY), const, const],
        out_specs=[row(d), row(d)],
        out_shape=[jax.ShapeDtypeStruct((t, d), F32), jax.ShapeDtypeStruct((t, d), BF16)],
        scratch_shapes=[pltpu.VMEM((2, TOP_K, COMBINE_TOK, d), F32), pltpu.SemaphoreType.DMA((2,))],
        compiler_params=_cparams(("arbitrary",)),
        name="moe_combine_ln2",
    )(dest_flat, dest_flat, wts, h, y, lg.reshape(1, d), lb.reshape(1, d))


def _routing_plan(ids, rank, counts, n_tiles):
    n_exp = counts.shape[0]
    padded = (counts + MOE_TILE - 1) // MOE_TILE * MOE_TILE
    ends = jnp.cumsum(padded)
    starts = ends - padded
    dest = (jnp.take(starts, ids) + rank).reshape(-1)
    n_used = (ends[-1] // MOE_TILE).reshape(1)
    tile_start = jnp.arange(n_tiles, dtype=jnp.int32) * MOE_TILE
    tile_expert = jnp.sum((tile_start[:, None] >= ends[None, :]).astype(jnp.int32), axis=1)
    last_used_expert = jnp.max(jnp.where(padded > 0, jnp.arange(n_exp, dtype=jnp.int32), 0))
    tile_expert = jnp.minimum(tile_expert, last_used_expert)
    last_tile = jnp.where(padded > 0, ends // MOE_TILE - 1, -1)
    return dest.astype(jnp.int32), tile_expert.astype(jnp.int32), n_used.astype(jnp.int32), last_tile.astype(jnp.int32)


def _rope_tables(lp, front):
    pos = (jnp.arange(lp, dtype=jnp.int32) - front).astype(F32)
    inv = ROPE_THETA ** (-jnp.arange(0, ATTN_HEAD_DIM, 2, dtype=F32) / ATTN_HEAD_DIM)
    ang = pos[:, None] * inv[None, :]
    ang = jnp.concatenate([ang, ang, ang, ang], axis=-1)
    return jnp.cos(ang), jnp.sin(ang)


def kernel(x, meta_tokens, ln_in_g, ln_in_b, w_in, ssm_lambda_re, ssm_lambda_im, ssm_log_dt, ssm_b_re, ssm_b_im, ssm_c_re, ssm_c_im, ssm_d, ssm_w_glu, ssm_w_out, conv_w, conv_w_out, attn_lambda_q1, attn_lambda_k1, attn_lambda_q2, attn_lambda_k2, attn_subln_g, attn_w_out, gate_w, gate_b, w_o, ln1_g, ln1_b, router_w, router_b, expert_w_gu, expert_b_gu, expert_w_down, expert_b_down, ln2_g, ln2_b):
    bsz, seq, d = x.shape
    assert d == D_MODEL and seq % ATT_BLOCK == 0
    lp = seq + ATT_BLOCK
    front = ATT_BLOCK - N_META
    t = bsz * lp

    meta = jnp.broadcast_to(meta_tokens.astype(x.dtype)[None], (bsz, N_META, d))
    hcat = jnp.concatenate([jnp.zeros((bsz, front, d), x.dtype), meta, x], axis=1).reshape(t, d)
    h, hb = _ln_in(hcat, ln_in_g, ln_in_b, lp=lp, front=front)
    cos, sin = _rope_tables(lp, front)

    s3 = SSM_WIDTH + 3 * CONV_WIDTH
    s5 = s3 + 2 * ATTN_WIDTH
    for l in range(DEPTH):
        w_in_b = w_in[l].astype(BF16)
        u, cb, z = _proj_sc(hb, w_in_b[:, :s3])
        q, k = _proj_qk(hb, w_in_b[:, s3:s5], cos, sin, lp=lp)
        vt = _proj_vt(hb, w_in_b[:, s5:].T)
        g = _proj_gate(hb, gate_w[l].astype(BF16), gate_b[l])

        bdb, bdc, pw = _ssm_tables(ssm_lambda_re[l], ssm_lambda_im[l], ssm_log_dt[l], ssm_b_re[l], ssm_b_im[l],
                                   ssm_c_re[l], ssm_c_im[l])
        ys = _ssm_branch(u, bdb, bdc, pw, ssm_d[l].astype(F32), ssm_w_glu[l].astype(BF16),
                         ssm_w_out[l].astype(BF16), bsz=bsz, lp=lp)
        yc = _conv_branch(cb, z, conv_w[l].astype(F32), conv_w_out[l].astype(BF16))

        lam_init = 0.8 - 0.6 * math.exp(-0.3 * l)
        lam = (jnp.exp(jnp.sum(attn_lambda_q1[l].astype(F32) * attn_lambda_k1[l].astype(F32)))
               - jnp.exp(jnp.sum(attn_lambda_q2[l].astype(F32) * attn_lambda_k2[l].astype(F32)))
               + lam_init).reshape(1)
        g_col = jnp.broadcast_to(attn_subln_g[l].astype(F32)[:, None], (ATTN_VALUE_DIM, ATT_BLOCK))
        o = _attention(lam, q, k, vt, g_col, bsz=bsz, lp=lp, front=front, lam_init=lam_init)

        h, hb = _merge(h, g, ys, yc, o, attn_w_out[l].astype(BF16), w_o[l].astype(BF16), ln1_g[l], ln1_b[l],
                       lp=lp, front=front)

        ids, wts, rank, counts = _router(h, router_w[l].astype(F32), router_b[l].astype(F32))
        n_tiles = t * TOP_K // MOE_TILE + N_EXPERTS
        dest, tile_expert, n_used, last_tile = _routing_plan(ids[:, :TOP_K], rank[:, :TOP_K],
                                                             counts[0].astype(jnp.int32), n_tiles)
        xs = _dispatch(h, dest, last_tile, n_used, n_tiles * MOE_TILE)
        y = _experts(tile_expert, n_used, xs, expert_w_gu[l].astype(BF16), expert_b_gu[l].astype(F32),
                     expert_w_down[l].astype(BF16), expert_b_down[l].astype(F32))
        h, hb = _combine(dest, wts, h, y, ln2_g[l], ln2_b[l], lp=lp, front=front)

    return h.reshape(bsz, lp, d)[:, ATT_BLOCK:]
```

```python
import functools
import math

import jax
import jax.numpy as jnp
from jax import lax
from jax.experimental import pallas as pl
from jax.experimental.pallas import tpu as pltpu

F32 = jnp.float32
BF16 = jnp.bfloat16

D_MODEL = 1024
DEPTH = 2
CHUNK = 64
N_META = 16
SSM_WIDTH = 512
SSM_GROUP = 16
SSM_GROUPS = 32
SSM_STATE = 64
SSM_CH = SSM_GROUPS * SSM_STATE
CONV_WIDTH = 512
CONV_K = 3
ATTN_HEADS = 8
ATTN_HEAD_DIM = 64
ATTN_VALUE_DIM = 128
ATTN_WIDTH = 1024
ROPE_THETA = 10000.0
N_EXPERTS = 32
TOP_K = 4
D_FF = 1024
SWIGLU_LIMIT = 7.0
SWIGLU_ALPHA = 1.702
DEEPNORM_ALPHA = (2.0 * DEPTH) ** 0.25
LN_EPS = 1e-5
RMS_EPS = 1e-5
NEG_INF = -1e30

ATT_BLOCK = 256
LANES = 128
SUBLANES = 8
VMEM_LIMIT = 48 * 1024 * 1024


def _cparams(sem):
    return pltpu.CompilerParams(dimension_semantics=sem, vmem_limit_bytes=VMEM_LIMIT)


def _pick_tile(n, candidates):
    for c in candidates:
        if n % c == 0:
            return c
    raise ValueError(f"no tile for {n}")


def _layer_norm(x, g, b):
    mu = jnp.mean(x, axis=-1, keepdims=True)
    xc = x - mu
    var = jnp.mean(xc * xc, axis=-1, keepdims=True)
    return xc * lax.rsqrt(var + LN_EPS) * g + b


def _seq_pos(tile_idx, tiles_per_seq, tm):
    base = lax.rem(tile_idx, tiles_per_seq) * tm
    return base + lax.broadcasted_iota(jnp.int32, (tm, 1), 0)


def _ln_in_kernel(x_ref, g_ref, b_ref, h_ref, hb_ref, *, tiles_per_seq, front):
    tm = x_ref.shape[0]
    y = _layer_norm(x_ref[...], g_ref[...], b_ref[...])
    pos = _seq_pos(pl.program_id(0), tiles_per_seq, tm)
    y = jnp.where(pos >= front, y, 0.0)
    h_ref[...] = y
    hb_ref[...] = y.astype(BF16)


def _ln_in(hcat, g, b, *, lp, front):
    t, d = hcat.shape
    tm = _pick_tile(lp, (512, 256))
    return pl.pallas_call(
        functools.partial(_ln_in_kernel, tiles_per_seq=lp // tm, front=front),
        grid=(t // tm,),
        in_specs=[pl.BlockSpec((tm, d), lambda i: (i, 0)),
                  pl.BlockSpec((1, d), lambda i: (0, 0)),
                  pl.BlockSpec((1, d), lambda i: (0, 0))],
        out_specs=[pl.BlockSpec((tm, d), lambda i: (i, 0)),
                   pl.BlockSpec((tm, d), lambda i: (i, 0))],
        out_shape=[jax.ShapeDtypeStruct((t, d), F32), jax.ShapeDtypeStruct((t, d), BF16)],
        compiler_params=_cparams(("parallel",)),
        name="ln_in",
    )(hcat, g.reshape(1, d), b.reshape(1, d))


def _proj_sc_kernel(x_ref, w_ref, u_ref, cb_ref, z_ref):
    acc = jnp.dot(x_ref[...], w_ref[...], preferred_element_type=F32)
    s0, s1, s2, s3 = SSM_WIDTH, SSM_WIDTH + CONV_WIDTH, SSM_WIDTH + 2 * CONV_WIDTH, SSM_WIDTH + 3 * CONV_WIDTH
    u_ref[...] = acc[:, :s0]
    cb_ref[...] = acc[:, s0:s1]
    z_ref[...] = acc[:, s1:s2] * acc[:, s2:s3]


def _proj_sc(hb, w):
    t, d = hb.shape
    n = w.shape[1]
    tm = _pick_tile(t, (512, 256))
    out = jax.ShapeDtypeStruct((t, SSM_WIDTH), F32)
    ospec = pl.BlockSpec((tm, SSM_WIDTH), lambda i: (i, 0))
    return pl.pallas_call(
        _proj_sc_kernel,
        grid=(t // tm,),
        in_specs=[pl.BlockSpec((tm, d), lambda i: (i, 0)),
                  pl.BlockSpec((d, n), lambda i: (0, 0))],
        out_specs=[ospec, ospec, ospec],
        out_shape=[out, out, out],
        compiler_params=_cparams(("parallel",)),
        name="proj_ssm_conv",
    )(hb, w)


def _proj_qk_kernel(x_ref, w_ref, cos_ref, sin_ref, q_ref, k_ref):
    acc = jnp.dot(x_ref[...], w_ref[...], preferred_element_type=F32)
    cos = cos_ref[...]
    sin = sin_ref[...]
    lane = lax.broadcasted_iota(jnp.int32, (1, LANES), 1)
    low_half = lax.rem(lane, ATTN_HEAD_DIM) < ATTN_HEAD_DIM // 2
    nblk = ATTN_WIDTH // LANES
    for blk in range(2 * nblk):
        a = acc[:, blk * LANES:(blk + 1) * LANES]
        rot = jnp.where(low_half, -pltpu.roll(a, LANES - ATTN_HEAD_DIM // 2, 1),
                        pltpu.roll(a, ATTN_HEAD_DIM // 2, 1))
        r = a * cos + rot * sin
        if blk < nblk:
            q_ref[:, blk * LANES:(blk + 1) * LANES] = (r * (ATTN_HEAD_DIM ** -0.5 * math.log2(math.e))).astype(BF16)
        else:
            k_ref[:, (blk - nblk) * LANES:(blk - nblk + 1) * LANES] = r.astype(BF16)


def _proj_qk(hb, w, cos, sin, *, lp):
    t, d = hb.shape
    tm = _pick_tile(lp, (512, 256))
    tps = lp // tm
    out = jax.ShapeDtypeStruct((t, ATTN_WIDTH), BF16)
    return pl.pallas_call(
        _proj_qk_kernel,
        grid=(t // tm,),
        in_specs=[pl.BlockSpec((tm, d), lambda i: (i, 0)),
                  pl.BlockSpec((d, 2 * ATTN_WIDTH), lambda i: (0, 0)),
                  pl.BlockSpec((tm, LANES), lambda i: (i % tps, 0)),
                  pl.BlockSpec((tm, LANES), lambda i: (i % tps, 0))],
        out_specs=[pl.BlockSpec((tm, ATTN_WIDTH), lambda i: (i, 0)),
                   pl.BlockSpec((tm, ATTN_WIDTH), lambda i: (i, 0))],
        out_shape=[out, out],
        compiler_params=_cparams(("parallel",)),
        name="proj_qk_rope",
    )(hb, w, cos, sin)


def _proj_vt_kernel(wt_ref, x_ref, o_ref):
    vt = lax.dot_general(wt_ref[...], x_ref[...], (((1,), (1,)), ((), ())),
                         preferred_element_type=F32).astype(BF16)
    for c in range(o_ref.shape[0]):
        o_ref[c] = vt[:, c * ATT_BLOCK:(c + 1) * ATT_BLOCK]


def _proj_vt(hb, wt):
    t, d = hb.shape
    tm = _pick_tile(t, (512, 256))
    per = tm // ATT_BLOCK
    return pl.pallas_call(
        _proj_vt_kernel,
        grid=(t // tm,),
        in_specs=[pl.BlockSpec((ATTN_WIDTH, d), lambda i: (0, 0)),
                  pl.BlockSpec((tm, d), lambda i: (i, 0))],
        out_specs=pl.BlockSpec((per, ATTN_WIDTH, ATT_BLOCK), lambda i: (i, 0, 0)),
        out_shape=jax.ShapeDtypeStruct((t // ATT_BLOCK, ATTN_WIDTH, ATT_BLOCK), BF16),
        compiler_params=_cparams(("parallel",)),
        name="proj_v_t",
    )(wt, hb)


def _proj_gate_kernel(x_ref, w_ref, b_ref, o_ref):
    acc = jnp.dot(x_ref[...], w_ref[...], preferred_element_type=F32) + b_ref[...]
    o_ref[...] = jax.nn.sigmoid(acc).astype(BF16)


def _proj_gate(hb, w, b):
    t, d = hb.shape
    n = w.shape[1]
    tm = _pick_tile(t, (512, 256))
    tn = 1024
    return pl.pallas_call(
        _proj_gate_kernel,
        grid=(t // tm, n // tn),
        in_specs=[pl.BlockSpec((tm, d), lambda i, j: (i, 0)),
                  pl.BlockSpec((d, tn), lambda i, j: (0, j)),
                  pl.BlockSpec((1, tn), lambda i, j: (0, j))],
        out_specs=pl.BlockSpec((tm, tn), lambda i, j: (i, j)),
        out_shape=jax.ShapeDtypeStruct((t, n), BF16),
        compiler_params=_cparams(("parallel", "parallel")),
        name="proj_gate",
    )(hb, w, b.reshape(1, n))


SCAN_STRIP = 512


def _ssm_kernel(u_ref, bdb_ref, bdc_ref, pw_ref, dskip_ref, wglu_ref, wout_ref, y_ref, bu_ref, carry_ref):
    tm = u_ref.shape[0]

    @pl.when(pl.program_id(1) == 0)
    def _():
        carry_ref[...] = jnp.zeros_like(carry_ref)

    u = u_ref[...]
    bu_ref[...] = jnp.dot(u.astype(BF16), bdb_ref[...], preferred_element_type=F32)

    row = lax.broadcasted_iota(jnp.int32, (SUBLANES, SCAN_STRIP), 0)
    for s in range(SSM_CH // SCAN_STRIP):
        re = slice(s * SCAN_STRIP, (s + 1) * SCAN_STRIP)
        im = slice(SSM_CH + s * SCAN_STRIP, SSM_CH + (s + 1) * SCAN_STRIP)
        steps = []
        for k, d in enumerate((1, 2, 4)):
            steps.append((d, jnp.broadcast_to(pw_ref[k:k + 1, re], (SUBLANES, SCAN_STRIP)),
                          jnp.broadcast_to(pw_ref[k:k + 1, im], (SUBLANES, SCAN_STRIP))))
        pr = pw_ref[SUBLANES:2 * SUBLANES, re]
        pi = pw_ref[SUBLANES:2 * SUBLANES, im]

        def group(gi, carry):
            cr, ci = carry
            r0 = pl.multiple_of(gi * SUBLANES, SUBLANES)
            xr = bu_ref[pl.ds(r0, SUBLANES), re]
            xi = bu_ref[pl.ds(r0, SUBLANES), im]
            for d, ar, ai in steps:
                sr = jnp.where(row >= d, pltpu.roll(xr, d, 0), 0.0)
                si = jnp.where(row >= d, pltpu.roll(xi, d, 0), 0.0)
                xr, xi = xr + ar * sr - ai * si, xi + ar * si + ai * sr
            xr, xi = xr + pr * cr - pi * ci, xi + pr * ci + pi * cr
            bu_ref[pl.ds(r0, SUBLANES), re] = xr
            bu_ref[pl.ds(r0, SUBLANES), im] = xi
            last_r = jnp.broadcast_to(xr[SUBLANES - 1:SUBLANES, :], (SUBLANES, SCAN_STRIP))
            last_i = jnp.broadcast_to(xi[SUBLANES - 1:SUBLANES, :], (SUBLANES, SCAN_STRIP))
            return last_r, last_i

        cr, ci = lax.fori_loop(0, tm // SUBLANES, group, (carry_ref[:, re], carry_ref[:, im]))
        carry_ref[:, re] = cr
        carry_ref[:, im] = ci

    y = jnp.dot(bu_ref[...].astype(BF16), bdc_ref[...], preferred_element_type=F32) + dskip_ref[...] * u
    y = jax.nn.gelu(y)
    y = y * jax.nn.sigmoid(jnp.dot(y.astype(BF16), wglu_ref[...], preferred_element_type=F32))
    y_ref[...] = jnp.dot(y.astype(BF16), wout_ref[...], preferred_element_type=F32)


def _ssm_branch(u, bdb, bdc, pw, dskip, wglu, wout, *, bsz, lp):
    t = u.shape[0]
    tm = _pick_tile(lp, (256,))
    tps = lp // tm
    const = lambda shape: pl.BlockSpec(shape, lambda b, i: (0, 0))
    return pl.pallas_call(
        _ssm_kernel,
        grid=(bsz, tps),
        in_specs=[pl.BlockSpec((tm, SSM_WIDTH), lambda b, i: (b * tps + i, 0)),
                  const(bdb.shape), const(bdc.shape), const(pw.shape), const((1, SSM_WIDTH)),
                  const(wglu.shape), const(wout.shape)],
        out_specs=pl.BlockSpec((tm, D_MODEL), lambda b, i: (b * tps + i, 0)),
        out_shape=jax.ShapeDtypeStruct((t, D_MODEL), F32),
        scratch_shapes=[pltpu.VMEM((tm, 2 * SSM_CH), F32), pltpu.VMEM((SUBLANES, 2 * SSM_CH), F32)],
        compiler_params=_cparams(("arbitrary", "arbitrary")),
        name="ssm_branch",
    )(u, bdb, bdc, pw, dskip.reshape(1, SSM_WIDTH), wglu, wout)


def _ssm_tables(lam_re, lam_im, log_dt, b_re, b_im, c_re, c_im):
    lr, li = lam_re.astype(F32), lam_im.astype(F32)
    dt = jnp.exp(log_dt.astype(F32))[:, None]
    mag = jnp.exp(lr * dt)
    ar = mag * jnp.cos(li * dt)
    ai = mag * jnp.sin(li * dt)
    denom = lr * lr + li * li
    nr, ni = ar - 1.0, ai
    coef_r = (nr * lr + ni * li) / denom
    coef_i = (ni * lr - nr * li) / denom
    br, bi = b_re.astype(F32), b_im.astype(F32)
    bbar_r = coef_r[..., None] * br - coef_i[..., None] * bi
    bbar_i = coef_r[..., None] * bi + coef_i[..., None] * br
    eye = jnp.eye(SSM_GROUPS, dtype=F32)
    bd_in = lambda m: jnp.einsum('gpc,gh->gchp', m, eye).reshape(SSM_WIDTH, SSM_CH)
    bdb = jnp.concatenate([bd_in(bbar_r), bd_in(bbar_i)], axis=1)
    bd_out = lambda m: jnp.einsum('gcp,gh->gphc', m, eye).reshape(SSM_CH, SSM_WIDTH)
    bdc = jnp.concatenate([bd_out(c_re.astype(F32)), -bd_out(c_im.astype(F32))], axis=0)
    a1 = (ar.reshape(-1), ai.reshape(-1))
    cmul = lambda x, y: (x[0] * y[0] - x[1] * y[1], x[0] * y[1] + x[1] * y[0])
    pows = [a1]
    for _ in range(SUBLANES - 1):
        pows.append(cmul(pows[-1], a1))
    rows = [pows[0], pows[1], pows[3]] + [(jnp.zeros_like(a1[0]),) * 2] * (SUBLANES - 3) + pows
    pw = jnp.stack([jnp.concatenate(r) for r in rows])
    return bdb.astype(BF16), bdc.astype(BF16), pw


def _conv_kernel(cb_ref, z_ref, zprev_ref, w_ref, wout_ref, y_ref):
    tm = z_ref.shape[0]
    z = z_ref[...]
    zz = jnp.concatenate([zprev_ref[...], z], axis=0)
    w = w_ref[...]
    y = w[2:3, :] * z
    for j in range(CONV_K - 1):
        shift = CONV_K - 1 - j
        y = y + w[j:j + 1, :] * zz[SUBLANES - shift:SUBLANES - shift + tm, :]
    y = cb_ref[...] * y
    y_ref[...] = jnp.dot(y.astype(BF16), wout_ref[...], preferred_element_type=F32)


def _conv_branch(cb, z, w, wout):
    t = z.shape[0]
    tm = _pick_tile(t, (512, 256))
    per = tm // SUBLANES
    return pl.pallas_call(
        _conv_kernel,
        grid=(t // tm,),
        in_specs=[pl.BlockSpec((tm, CONV_WIDTH), lambda i: (i, 0)),
                  pl.BlockSpec((tm, CONV_WIDTH), lambda i: (i, 0)),
                  pl.BlockSpec((SUBLANES, CONV_WIDTH), lambda i: (jnp.maximum(i * per - 1, 0), 0)),
                  pl.BlockSpec((SUBLANES, CONV_WIDTH), lambda i: (0, 0)),
                  pl.BlockSpec((CONV_WIDTH, D_MODEL), lambda i: (0, 0))],
        out_specs=pl.BlockSpec((tm, D_MODEL), lambda i: (i, 0)),
        out_shape=jax.ShapeDtypeStruct((t, D_MODEL), F32),
        compiler_params=_cparams(("parallel",)),
        name="conv_branch",
    )(cb, z, z, jnp.pad(w, ((0, SUBLANES - CONV_K), (0, 0))), wout)


ATT_SUB = 3


def _attn_kernel(lam_ref, q_ref, k_ref, vt_ref, g_ref, o_ref,
                 qlo_ref, qhi_ref, m_ref, l_ref, acc_ref, s_ref, *, front, lam_init):
    sb = pl.program_id(2)
    blk = ATT_BLOCK
    real0 = front + N_META
    nt = (((1,), (1,)), ((), ()))

    q = q_ref[...]
    lane = lax.broadcasted_iota(jnp.int32, q.shape, 1)
    qlo_ref[...] = jnp.where(lane < ATTN_HEAD_DIM, q, jnp.zeros_like(q))
    qhi_ref[...] = jnp.where(lane >= ATTN_HEAD_DIM, q, jnp.zeros_like(q))
    m_ref[...] = jnp.full_like(m_ref, NEG_INF)
    l_ref[...] = jnp.zeros_like(l_ref)
    acc_ref[...] = jnp.zeros_like(acc_ref)

    def qk(a, k):
        rows = slice(a * blk, (a + 1) * blk)
        return [lax.dot_general(k, qref[rows, :], nt, preferred_element_type=F32) for qref in (qlo_ref, qhi_ref)]

    def consume(a, j, scores, vt, masked):
        if masked:
            pk = j * blk + lax.broadcasted_iota(jnp.int32, (blk, 1), 0)
            pq = (sb * ATT_SUB + a) * blk + lax.broadcasted_iota(jnp.int32, (1, blk), 1)
            cid_k = jnp.where(pk < front, 1 << 30, jnp.where(pk < real0, 0, 1 + (pk - real0) // CHUNK))
            cid_q = jnp.where(pq < real0, 0, 1 + (pq - real0) // CHUNK)
            vis = cid_k <= cid_q
            scores = [jnp.where(vis, s, NEG_INF) for s in scores]
        for idx, s in enumerate(scores):
            m_old = m_ref[a, idx]
            m_new = jnp.maximum(m_old, jnp.max(s, axis=0, keepdims=True))
            alpha = jnp.exp2(m_old - m_new)
            p = jnp.exp2(s - m_new)
            l_ref[a, idx] = alpha * l_ref[a, idx] + jnp.sum(p, axis=0, keepdims=True)
            m_ref[a, idx] = m_new
            acc_ref[a, idx] = alpha * acc_ref[a, idx] + jnp.dot(vt, p.astype(BF16), preferred_element_type=F32)

    def pair(a, j, k, vt, masked):
        consume(a, j, qk(a, k), vt, masked)

    def k_block(j):
        return k_ref[pl.ds(pl.multiple_of(j * blk, blk), blk), :]

    def kv_block(j):
        return k_block(j), vt_ref[j]

    @pl.when(sb > 0)
    def _():
        k, vt = kv_block(0)
        for a in range(ATT_SUB):
            pair(a, 0, k, vt, True)

    def qk_all(k, buf):
        for a in range(ATT_SUB):
            for idx, s in enumerate(qk(a, k)):
                s_ref[buf, a, idx] = s

    def consume_all(j, buf, vt):
        for a in range(ATT_SUB):
            consume(a, j, [s_ref[buf, a, 0], s_ref[buf, a, 1]], vt, False)

    n_full = jnp.maximum(sb * ATT_SUB - 1, 0)
    n_trips = n_full // 2

    @pl.when(n_trips > 0)
    def _():
        qk_all(k_block(1), 0)

    def body(jj, carry):
        j = 1 + 2 * jj
        k1, vt1 = kv_block(j + 1)
        qk_all(k1, 1)
        consume_all(j, 0, vt_ref[j])
        qk_all(k_block(jnp.minimum(j + 2, n_full)), 0)
        consume_all(j + 1, 1, vt1)
        return carry

    lax.fori_loop(0, n_trips, body, 0)

    @pl.when(n_full % 2 == 1)
    def _():
        k, vt = kv_block(n_full)
        for a in range(ATT_SUB):
            pair(a, n_full, k, vt, False)

    for dj in range(ATT_SUB):
        j = sb * ATT_SUB + dj
        k, vt = kv_block(j)
        for a in range(dj, ATT_SUB):
            pair(a, j, k, vt, True)

    lam = lam_ref[0]
    for a in range(ATT_SUB):
        o = acc_ref[a, 0] / l_ref[a, 0] - lam * (acc_ref[a, 1] / l_ref[a, 1])
        o = o * lax.rsqrt(jnp.mean(o * o, axis=0, keepdims=True) + RMS_EPS)
        o = o * g_ref[...] * (1.0 - lam_init)
        o_ref[a * blk:(a + 1) * blk, :] = o.T.astype(BF16)


def _attention(lam, q, k, vt3, g_col, *, bsz, lp, front, lam_init):
    t = q.shape[0]
    blk = ATT_BLOCK
    nb = lp // blk
    assert nb % ATT_SUB == 0
    nsb = nb // ATT_SUB
    tq = ATT_SUB * blk
    return pl.pallas_call(
        functools.partial(_attn_kernel, front=front, lam_init=lam_init),
        grid=(bsz, ATTN_HEADS, nsb),
        in_specs=[pl.BlockSpec(memory_space=pltpu.SMEM),
                  pl.BlockSpec((tq, LANES), lambda b, h, i: (b * nsb + i, h)),
                  pl.BlockSpec((lp, LANES), lambda b, h, i: (b, h)),
                  pl.BlockSpec((nb, LANES, blk), lambda b, h, i: (b, h, 0)),
                  pl.BlockSpec((LANES, blk), lambda b, h, i: (0, 0))],
        out_specs=pl.BlockSpec((tq, LANES), lambda b, h, i: (b * nsb + i, h)),
        out_shape=jax.ShapeDtypeStruct((t, ATTN_WIDTH), BF16),
        scratch_shapes=[pltpu.VMEM((tq, LANES), BF16), pltpu.VMEM((tq, LANES), BF16),
                        pltpu.VMEM((ATT_SUB, 2, 1, blk), F32), pltpu.VMEM((ATT_SUB, 2, 1, blk), F32),
                        pltpu.VMEM((ATT_SUB, 2, LANES, blk), F32),
                        pltpu.VMEM((2, ATT_SUB, 2, blk, blk), F32)],
        compiler_params=_cparams(("parallel", "parallel", "arbitrary")),
        name="diff_attention",
    )(lam, q, k, vt3, g_col)


def _merge_kernel(h_ref, g_ref, ys_ref, yc_ref, o_ref, wao_ref, wo_ref, lg_ref, lb_ref,
                  h1_ref, h1b_ref, *, tiles_per_seq, front):
    tm = h_ref.shape[0]
    ya = jnp.dot(o_ref[...], wao_ref[...], preferred_element_type=F32)
    g = g_ref[...].astype(F32)
    merged = g[:, :D_MODEL] * ys_ref[...] + g[:, D_MODEL:2 * D_MODEL] * yc_ref[...] + g[:, 2 * D_MODEL:] * ya
    r = DEEPNORM_ALPHA * h_ref[...] + jnp.dot(merged.astype(BF16), wo_ref[...], preferred_element_type=F32)
    y = _layer_norm(r, lg_ref[...], lb_ref[...])
    pos = _seq_pos(pl.program_id(0), tiles_per_seq, tm)
    y = jnp.where(pos >= front, y, 0.0)
    h1_ref[...] = y
    h1b_ref[...] = y.astype(BF16)


def _merge(h, g, ys, yc, o, wao, wo, lg, lb, *, lp, front):
    t, d = h.shape
    tm = _pick_tile(lp, (256,))
    row = lambda w: pl.BlockSpec((tm, w), lambda i: (i, 0))
    const = lambda a, b: pl.BlockSpec((a, b), lambda i: (0, 0))
    return pl.pallas_call(
        functools.partial(_merge_kernel, tiles_per_seq=lp // tm, front=front),
        grid=(t // tm,),
        in_specs=[row(d), row(3 * d), row(d), row(d), row(ATTN_WIDTH),
                  const(ATTN_WIDTH, d), const(d, d), const(1, d), const(1, d)],
        out_specs=[row(d), row(d)],
        out_shape=[jax.ShapeDtypeStruct((t, d), F32), jax.ShapeDtypeStruct((t, d), BF16)],
        compiler_params=_cparams(("parallel",)),
        name="merge_ln1",
    )(h, g, ys, yc, o, wao, wo, lg.reshape(1, d), lb.reshape(1, d))


def _router_kernel(h_ref, w_ref, b_ref, ids_ref, wts_ref, rank_ref, cnt_ref, base_ref):
    tm = h_ref.shape[0]

    @pl.when(pl.program_id(0) == 0)
    def _():
        base_ref[...] = jnp.zeros_like(base_ref)

    logits = jnp.dot(h_ref[...], w_ref[...], preferred_element_type=F32,
                     precision=lax.Precision.HIGHEST) + b_ref[...]
    lane = lax.broadcasted_iota(jnp.int32, logits.shape, 1).astype(F32)
    work = logits
    sels, vals, firsts = [], [], []
    for _ in range(TOP_K):
        mx = jnp.max(work, axis=1, keepdims=True)
        first = jnp.min(jnp.where(work == mx, lane, float(N_EXPERTS)), axis=1, keepdims=True)
        sel = lane == first
        sels.append(sel)
        vals.append(mx)
        firsts.append(first)
        work = jnp.where(sel, -jnp.inf, work)
    exps = [jnp.exp(v - vals[0]) for v in vals]
    den = exps[0] + exps[1] + exps[2] + exps[3]

    onehot = jnp.zeros_like(logits)
    for sel in sels:
        onehot = onehot + jnp.where(sel, 1.0, 0.0)
    r = lax.broadcasted_iota(jnp.int32, (tm, tm), 0)
    c = lax.broadcasted_iota(jnp.int32, (tm, tm), 1)
    tri = jnp.where(r > c, 1.0, 0.0).astype(BF16)
    before = jnp.dot(tri, onehot.astype(BF16), preferred_element_type=F32) + base_ref[...]
    base_ref[...] = base_ref[...] + jnp.sum(onehot, axis=0, keepdims=True)
    cnt_ref[...] = base_ref[...]

    out_lane = lax.broadcasted_iota(jnp.int32, (tm, LANES), 1)
    ids = jnp.zeros((tm, LANES), F32)
    wts = jnp.zeros((tm, LANES), F32)
    rank = jnp.zeros((tm, LANES), F32)
    for kk in range(TOP_K):
        rk = jnp.sum(jnp.where(sels[kk], before, 0.0), axis=1, keepdims=True)
        ids = jnp.where(out_lane == kk, firsts[kk], ids)
        wts = jnp.where(out_lane == kk, exps[kk] / den, wts)
        rank = jnp.where(out_lane == kk, rk, rank)
    ids_ref[...] = ids.astype(jnp.int32)
    wts_ref[...] = wts
    rank_ref[...] = rank.astype(jnp.int32)


def _router(h, w, b):
    t, d = h.shape
    tm = _pick_tile(t, (512, 256))
    wide = lambda dt: jax.ShapeDtypeStruct((t, LANES), dt)
    row = pl.BlockSpec((tm, LANES), lambda i: (i, 0))
    return pl.pallas_call(
        _router_kernel,
        grid=(t // tm,),
        in_specs=[pl.BlockSpec((tm, d), lambda i: (i, 0)),
                  pl.BlockSpec((d, N_EXPERTS), lambda i: (0, 0)),
                  pl.BlockSpec((1, N_EXPERTS), lambda i: (0, 0))],
        out_specs=[row, row, row, pl.BlockSpec((1, N_EXPERTS), lambda i: (0, 0))],
        out_shape=[wide(jnp.int32), wide(F32), wide(jnp.int32), jax.ShapeDtypeStruct((1, N_EXPERTS), F32)],
        scratch_shapes=[pltpu.VMEM((1, N_EXPERTS), F32)],
        compiler_params=_cparams(("arbitrary",)),
        name="moe_router",
    )(h, w, b.reshape(1, N_EXPERTS))


MOE_TILE = 256
ROUTE_TOK = 256


def _dispatch_kernel(last_ref, nu_ref, dest_ref, h_ref, xs_ref, zero_ref, stage_ref, zsem, sems):
    i = pl.program_id(0)
    n = pl.num_programs(0)
    n_tiles = xs_ref.shape[0] // MOE_TILE

    def fill(tile):
        return pltpu.make_async_copy(zero_ref, xs_ref.at[pl.ds(tile * MOE_TILE, MOE_TILE)], zsem)

    @pl.when(i == 0)
    def _():
        zero_ref[...] = jnp.zeros_like(zero_ref)
        for e in range(N_EXPERTS):
            @pl.when(last_ref[e] >= 0)
            def _():
                fill(last_ref[e]).start()

        def start_unused(tile, carry):
            fill(tile).start()
            return carry

        def wait_unused(tile, carry):
            fill(tile).wait()
            return carry

        lax.fori_loop(nu_ref[0], n_tiles, start_unused, 0)
        for e in range(N_EXPERTS):
            @pl.when(last_ref[e] >= 0)
            def _():
                fill(last_ref[e]).wait()
        lax.fori_loop(nu_ref[0], n_tiles, wait_unused, 0)

    slot = lax.rem(i, 2)
    stage_ref[slot] = h_ref[...]

    def issue(r, carry):
        for kk in range(TOP_K):
            d = dest_ref[r * TOP_K + kk]
            pltpu.make_async_copy(stage_ref.at[slot, pl.ds(r, 1)], xs_ref.at[pl.ds(d, 1)], sems.at[slot]).start()
        return carry

    lax.fori_loop(0, ROUTE_TOK, issue, 0)

    def drain(s):
        for _ in range(TOP_K):
            pltpu.make_async_copy(stage_ref.at[s], xs_ref.at[pl.ds(0, ROUTE_TOK)], sems.at[s]).wait()

    @pl.when(i > 0)
    def _():
        drain(1 - slot)

    @pl.when(i == n - 1)
    def _():
        drain(slot)


def _dispatch(h, dest_flat, last_tile, n_used, n_rows):
    t, d = h.shape
    return pl.pallas_call(
        _dispatch_kernel,
        grid_spec=pltpu.PrefetchScalarGridSpec(
            num_scalar_prefetch=2,
            grid=(t // ROUTE_TOK,),
            in_specs=[pl.BlockSpec((ROUTE_TOK * TOP_K,), lambda i, last, nu: (i,), memory_space=pltpu.SMEM),
                      pl.BlockSpec((ROUTE_TOK, d), lambda i, last, nu: (i, 0))],
            out_specs=pl.BlockSpec(memory_space=pl.ANY),
            scratch_shapes=[pltpu.VMEM((MOE_TILE, d), F32), pltpu.VMEM((2, ROUTE_TOK, d), F32),
                            pltpu.SemaphoreType.DMA(()), pltpu.SemaphoreType.DMA((2,))]),
        out_shape=jax.ShapeDtypeStruct((n_rows, d), F32),
        compiler_params=_cparams(("arbitrary",)),
        name="moe_dispatch",
    )(last_tile, n_used, dest_flat, h)


def _experts_kernel(te_ref, nu_ref, xs_ref, wgu_ref, bgu_ref, wd_ref, bd_ref, y_ref):
    @pl.when(pl.program_id(0) >= nu_ref[0])
    def _():
        y_ref[...] = jnp.zeros_like(y_ref)

    @pl.when(pl.program_id(0) < nu_ref[0])
    def _():
        gu = jnp.dot(xs_ref[...].astype(BF16), wgu_ref[0], preferred_element_type=F32) + bgu_ref[0]
        gate = jnp.minimum(gu[:, :D_FF], SWIGLU_LIMIT)
        up = jnp.clip(gu[:, D_FF:], -SWIGLU_LIMIT, SWIGLU_LIMIT)
        hid = (up + 1.0) * gate * jax.nn.sigmoid(SWIGLU_ALPHA * gate)
        y_ref[...] = jnp.dot(hid.astype(BF16), wd_ref[0], preferred_element_type=F32) + bd_ref[0]


def _experts(tile_expert, n_used, xs, wgu, bgu, wd, bd):
    n_rows, d = xs.shape
    n_exp = wgu.shape[0]
    rows = lambda i, te, nu: (jnp.minimum(i, nu[0] - 1), 0)
    per_e = lambda i, te, nu: (te[i], 0, 0)
    return pl.pallas_call(
        _experts_kernel,
        grid_spec=pltpu.PrefetchScalarGridSpec(
            num_scalar_prefetch=2,
            grid=(n_rows // MOE_TILE,),
            in_specs=[pl.BlockSpec((MOE_TILE, d), rows),
                      pl.BlockSpec((1, d, 2 * D_FF), per_e),
                      pl.BlockSpec((1, 1, 2 * D_FF), per_e),
                      pl.BlockSpec((1, D_FF, d), per_e),
                      pl.BlockSpec((1, 1, d), per_e)],
            out_specs=pl.BlockSpec((MOE_TILE, d), lambda i, te, nu: (i, 0))),
        out_shape=jax.ShapeDtypeStruct((n_rows, d), F32),
        compiler_params=_cparams(("arbitrary",)),
        name="moe_experts",
    )(tile_expert, n_used, xs, wgu, bgu.reshape(n_exp, 1, 2 * D_FF), wd, bd.reshape(n_exp, 1, d))


COMBINE_TOK = 128


def _combine_kernel(dcur_ref, dnext_ref, w_ref, h_ref, y_ref, lg_ref, lb_ref, h2_ref, h2b_ref,
                    ybuf_ref, sems, *, tiles_per_seq, front):
    i = pl.program_id(0)
    n = pl.num_programs(0)
    slot = lax.rem(i, 2)

    def issue(d_ref, s):
        def body(r, carry):
            for kk in range(TOP_K):
                d = d_ref[r * TOP_K + kk]
                pltpu.make_async_copy(y_ref.at[pl.ds(d, 1)], ybuf_ref.at[s, kk, pl.ds(r, 1)], sems.at[s]).start()
            return carry
        lax.fori_loop(0, COMBINE_TOK, body, 0)

    @pl.when(i == 0)
    def _():
        issue(dcur_ref, 0)

    @pl.when(i + 1 < n)
    def _():
        issue(dnext_ref, 1 - slot)

    for kk in range(TOP_K):
        pltpu.make_async_copy(y_ref.at[pl.ds(0, COMBINE_TOK)], ybuf_ref.at[slot, kk], sems.at[slot]).wait()

    w = w_ref[...]
    acc = w[:, 0:1] * ybuf_ref[slot, 0]
    for kk in range(1, TOP_K):
        acc = acc + w[:, kk:kk + 1] * ybuf_ref[slot, kk]
    y = _layer_norm(DEEPNORM_ALPHA * h_ref[...] + acc, lg_ref[...], lb_ref[...])
    pos = _seq_pos(i, tiles_per_seq, COMBINE_TOK)
    y = jnp.where(pos >= front, y, 0.0)
    h2_ref[...] = y
    h2b_ref[...] = y.astype(BF16)


def _combine(dest_flat, wts, h, y, lg, lb, *, lp, front):
    t, d = h.shape
    n = t // COMBINE_TOK
    blk = COMBINE_TOK * TOP_K
    row = lambda w: pl.BlockSpec((COMBINE_TOK, w), lambda i: (i, 0))
    const = pl.BlockSpec((1, d), lambda i: (0, 0))
    return pl.pallas_call(
        functools.partial(_combine_kernel, tiles_per_seq=lp // COMBINE_TOK, front=front),
        grid=(n,),
        in_specs=[pl.BlockSpec((blk,), lambda i: (i,), memory_space=pltpu.SMEM),
                  pl.BlockSpec((blk,), lambda i: (jnp.minimum(i + 1, n - 1),), memory_space=pltpu.SMEM),
                  row(LANES), row(d), pl.BlockSpec(memory_space=pl.ANY), const, const],
        out_specs=[row(d), row(d)],
        out_shape=[jax.ShapeDtypeStruct((t, d), F32), jax.ShapeDtypeStruct((t, d), BF16)],
        scratch_shapes=[pltpu.VMEM((2, TOP_K, COMBINE_TOK, d), F32), pltpu.SemaphoreType.DMA((2,))],
        compiler_params=_cparams(("arbitrary",)),
        name="moe_combine_ln2",
    )(dest_flat, dest_flat, wts, h, y, lg.reshape(1, d), lb.reshape(1, d))


def _routing_plan(ids, rank, counts, n_tiles):
    n_exp = counts.shape[0]
    padded = (counts + MOE_TILE - 1) // MOE_TILE * MOE_TILE
    ends = jnp.cumsum(padded)
    starts = ends - padded
    dest = (jnp.take(starts, ids) + rank).reshape(-1)
    n_used = (ends[-1] // MOE_TILE).reshape(1)
    tile_start = jnp.arange(n_tiles, dtype=jnp.int32) * MOE_TILE
    tile_expert = jnp.sum((tile_start[:, None] >= ends[None, :]).astype(jnp.int32), axis=1)
    last_used_expert = jnp.max(jnp.where(padded > 0, jnp.arange(n_exp, dtype=jnp.int32), 0))
    tile_expert = jnp.minimum(tile_expert, last_used_expert)
    last_tile = jnp.where(padded > 0, ends // MOE_TILE - 1, -1)
    return dest.astype(jnp.int32), tile_expert.astype(jnp.int32), n_used.astype(jnp.int32), last_tile.astype(jnp.int32)


def _rope_tables(lp, front):
    pos = (jnp.arange(lp, dtype=jnp.int32) - front).astype(F32)
    inv = ROPE_THETA ** (-jnp.arange(0, ATTN_HEAD_DIM, 2, dtype=F32) / ATTN_HEAD_DIM)
    ang = pos[:, None] * inv[None, :]
    ang = jnp.concatenate([ang, ang, ang, ang], axis=-1)
    return jnp.cos(ang), jnp.sin(ang)


def kernel(x, meta_tokens, ln_in_g, ln_in_b, w_in, ssm_lambda_re, ssm_lambda_im, ssm_log_dt, ssm_b_re, ssm_b_im, ssm_c_re, ssm_c_im, ssm_d, ssm_w_glu, ssm_w_out, conv_w, conv_w_out, attn_lambda_q1, attn_lambda_k1, attn_lambda_q2, attn_lambda_k2, attn_subln_g, attn_w_out, gate_w, gate_b, w_o, ln1_g, ln1_b, router_w, router_b, expert_w_gu, expert_b_gu, expert_w_down, expert_b_down, ln2_g, ln2_b):
    bsz, seq, d = x.shape
    assert d == D_MODEL and seq % ATT_BLOCK == 0
    lp = seq + ATT_BLOCK
    front = ATT_BLOCK - N_META
    t = bsz * lp

    meta = jnp.broadcast_to(meta_tokens.astype(x.dtype)[None], (bsz, N_META, d))
    hcat = jnp.concatenate([jnp.zeros((bsz, front, d), x.dtype), meta, x], axis=1).reshape(t, d)
    h, hb = _ln_in(hcat, ln_in_g, ln_in_b, lp=lp, front=front)
    cos, sin = _rope_tables(lp, front)

    s3 = SSM_WIDTH + 3 * CONV_WIDTH
    s5 = s3 + 2 * ATTN_WIDTH
    for l in range(DEPTH):
        w_in_b = w_in[l].astype(BF16)
        u, cb, z = _proj_sc(hb, w_in_b[:, :s3])
        q, k = _proj_qk(hb, w_in_b[:, s3:s5], cos, sin, lp=lp)
        vt = _proj_vt(hb, w_in_b[:, s5:].T)
        g = _proj_gate(hb, gate_w[l].astype(BF16), gate_b[l])

        bdb, bdc, pw = _ssm_tables(ssm_lambda_re[l], ssm_lambda_im[l], ssm_log_dt[l], ssm_b_re[l], ssm_b_im[l],
                                   ssm_c_re[l], ssm_c_im[l])
        ys = _ssm_branch(u, bdb, bdc, pw, ssm_d[l].astype(F32), ssm_w_glu[l].astype(BF16),
                         ssm_w_out[l].astype(BF16), bsz=bsz, lp=lp)
        yc = _conv_branch(cb, z, conv_w[l].astype(F32), conv_w_out[l].astype(BF16))

        lam_init = 0.8 - 0.6 * math.exp(-0.3 * l)
        lam = (jnp.exp(jnp.sum(attn_lambda_q1[l].astype(F32) * attn_lambda_k1[l].astype(F32)))
               - jnp.exp(jnp.sum(attn_lambda_q2[l].astype(F32) * attn_lambda_k2[l].astype(F32)))
               + lam_init).reshape(1)
        g_col = jnp.broadcast_to(attn_subln_g[l].astype(F32)[:, None], (ATTN_VALUE_DIM, ATT_BLOCK))
        o = _attention(lam, q, k, vt, g_col, bsz=bsz, lp=lp, front=front, lam_init=lam_init)

        h, hb = _merge(h, g, ys, yc, o, attn_w_out[l].astype(BF16), w_o[l].astype(BF16), ln1_g[l], ln1_b[l],
                       lp=lp, front=front)

        ids, wts, rank, counts = _router(h, router_w[l].astype(F32), router_b[l].astype(F32))
        n_tiles = t * TOP_K // MOE_TILE + N_EXPERTS
        dest, tile_expert, n_used, last_tile = _routing_plan(ids[:, :TOP_K], rank[:, :TOP_K],
                                                             counts[0].astype(jnp.int32), n_tiles)
        xs = _dispatch(h, dest, last_tile, n_used, n_tiles * MOE_TILE)
        y = _experts(tile_expert, n_used, xs, expert_w_gu[l].astype(BF16), expert_b_gu[l].astype(F32),
                     expert_w_down[l].astype(BF16), expert_b_down[l].astype(F32))
        h, hb = _combine(dest, wts, h, y, ln2_g[l], ln2_b[l], lp=lp, front=front)

    return h.reshape(bsz, lp, d)[:, ATT_BLOCK:]
```

```python
import functools
import math

import jax
import jax.numpy as jnp
from jax import lax
from jax.experimental import pallas as pl
from jax.experimental.pallas import tpu as pltpu

F32 = jnp.float32
BF16 = jnp.bfloat16

D_MODEL = 1024
DEPTH = 2
CHUNK = 64
N_META = 16
SSM_WIDTH = 512
SSM_GROUP = 16
SSM_GROUPS = 32
SSM_STATE = 64
SSM_CH = SSM_GROUPS * SSM_STATE
CONV_WIDTH = 512
CONV_K = 3
ATTN_HEADS = 8
ATTN_HEAD_DIM = 64
ATTN_VALUE_DIM = 128
ATTN_WIDTH = 1024
ROPE_THETA = 10000.0
N_EXPERTS = 32
TOP_K = 4
D_FF = 1024
SWIGLU_LIMIT = 7.0
SWIGLU_ALPHA = 1.702
DEEPNORM_ALPHA = (2.0 * DEPTH) ** 0.25
LN_EPS = 1e-5
RMS_EPS = 1e-5
NEG_INF = -1e30

ATT_BLOCK = 256
LANES = 128
SUBLANES = 8
VMEM_LIMIT = 48 * 1024 * 1024


def _cparams(sem):
    return pltpu.CompilerParams(dimension_semantics=sem, vmem_limit_bytes=VMEM_LIMIT)


def _pick_tile(n, candidates):
    for c in candidates:
        if n % c == 0:
            return c
    raise ValueError(f"no tile for {n}")


def _layer_norm(x, g, b):
    mu = jnp.mean(x, axis=-1, keepdims=True)
    xc = x - mu
    var = jnp.mean(xc * xc, axis=-1, keepdims=True)
    return xc * lax.rsqrt(var + LN_EPS) * g + b


def _seq_pos(tile_idx, tiles_per_seq, tm):
    base = lax.rem(tile_idx, tiles_per_seq) * tm
    return base + lax.broadcasted_iota(jnp.int32, (tm, 1), 0)


def _ln_in_kernel(x_ref, g_ref, b_ref, h_ref, hb_ref, *, tiles_per_seq, front):
    tm = x_ref.shape[0]
    y = _layer_norm(x_ref[...], g_ref[...], b_ref[...])
    pos = _seq_pos(pl.program_id(0), tiles_per_seq, tm)
    y = jnp.where(pos >= front, y, 0.0)
    h_ref[...] = y
    hb_ref[...] = y.astype(BF16)


def _ln_in(hcat, g, b, *, lp, front):
    t, d = hcat.shape
    tm = _pick_tile(lp, (512, 256))
    return pl.pallas_call(
        functools.partial(_ln_in_kernel, tiles_per_seq=lp // tm, front=front),
        grid=(t // tm,),
        in_specs=[pl.BlockSpec((tm, d), lambda i: (i, 0)),
                  pl.BlockSpec((1, d), lambda i: (0, 0)),
                  pl.BlockSpec((1, d), lambda i: (0, 0))],
        out_specs=[pl.BlockSpec((tm, d), lambda i: (i, 0)),
                   pl.BlockSpec((tm, d), lambda i: (i, 0))],
        out_shape=[jax.ShapeDtypeStruct((t, d), F32), jax.ShapeDtypeStruct((t, d), BF16)],
        compiler_params=_cparams(("parallel",)),
        name="ln_in",
    )(hcat, g.reshape(1, d), b.reshape(1, d))


def _proj_sc_kernel(x_ref, w_ref, u_ref, cb_ref, z_ref):
    acc = jnp.dot(x_ref[...], w_ref[...], preferred_element_type=F32)
    s0, s1, s2, s3 = SSM_WIDTH, SSM_WIDTH + CONV_WIDTH, SSM_WIDTH + 2 * CONV_WIDTH, SSM_WIDTH + 3 * CONV_WIDTH
    u_ref[...] = acc[:, :s0]
    cb_ref[...] = acc[:, s0:s1]
    z_ref[...] = acc[:, s1:s2] * acc[:, s2:s3]


def _proj_sc(hb, w):
    t, d = hb.shape
    n = w.shape[1]
    tm = _pick_tile(t, (512, 256))
    out = jax.ShapeDtypeStruct((t, SSM_WIDTH), F32)
    ospec = pl.BlockSpec((tm, SSM_WIDTH), lambda i: (i, 0))
    return pl.pallas_call(
        _proj_sc_kernel,
        grid=(t // tm,),
        in_specs=[pl.BlockSpec((tm, d), lambda i: (i, 0)),
                  pl.BlockSpec((d, n), lambda i: (0, 0))],
        out_specs=[ospec, ospec, ospec],
        out_shape=[out, out, out],
        compiler_params=_cparams(("parallel",)),
        name="proj_ssm_conv",
    )(hb, w)


def _proj_qk_kernel(x_ref, w_ref, cos_ref, sin_ref, q_ref, k_ref):
    acc = jnp.dot(x_ref[...], w_ref[...], preferred_element_type=F32)
    cos = cos_ref[...]
    sin = sin_ref[...]
    lane = lax.broadcasted_iota(jnp.int32, (1, LANES), 1)
    low_half = lax.rem(lane, ATTN_HEAD_DIM) < ATTN_HEAD_DIM // 2
    nblk = ATTN_WIDTH // LANES
    for blk in range(2 * nblk):
        a = acc[:, blk * LANES:(blk + 1) * LANES]
        rot = jnp.where(low_half, -pltpu.roll(a, LANES - ATTN_HEAD_DIM // 2, 1),
                        pltpu.roll(a, ATTN_HEAD_DIM // 2, 1))
        r = a * cos + rot * sin
        if blk < nblk:
            q_ref[:, blk * LANES:(blk + 1) * LANES] = (r * (ATTN_HEAD_DIM ** -0.5 * math.log2(math.e))).astype(BF16)
        else:
            k_ref[:, (blk - nblk) * LANES:(blk - nblk + 1) * LANES] = r.astype(BF16)


def _proj_qk(hb, w, cos, sin, *, lp):
    t, d = hb.shape
    tm = _pick_tile(lp, (512, 256))
    tps = lp // tm
    out = jax.ShapeDtypeStruct((t, ATTN_WIDTH), BF16)
    return pl.pallas_call(
        _proj_qk_kernel,
        grid=(t // tm,),
        in_specs=[pl.BlockSpec((tm, d), lambda i: (i, 0)),
                  pl.BlockSpec((d, 2 * ATTN_WIDTH), lambda i: (0, 0)),
                  pl.BlockSpec((tm, LANES), lambda i: (i % tps, 0)),
                  pl.BlockSpec((tm, LANES), lambda i: (i % tps, 0))],
        out_specs=[pl.BlockSpec((tm, ATTN_WIDTH), lambda i: (i, 0)),
                   pl.BlockSpec((tm, ATTN_WIDTH), lambda i: (i, 0))],
        out_shape=[out, out],
        compiler_params=_cparams(("parallel",)),
        name="proj_qk_rope",
    )(hb, w, cos, sin)


def _proj_vt_kernel(wt_ref, x_ref, o_ref):
    vt = lax.dot_general(wt_ref[...], x_ref[...], (((1,), (1,)), ((), ())),
                         preferred_element_type=F32).astype(BF16)
    for c in range(o_ref.shape[0]):
        o_ref[c] = vt[:, c * ATT_BLOCK:(c + 1) * ATT_BLOCK]


def _proj_vt(hb, wt):
    t, d = hb.shape
    tm = _pick_tile(t, (512, 256))
    per = tm // ATT_BLOCK
    return pl.pallas_call(
        _proj_vt_kernel,
        grid=(t // tm,),
        in_specs=[pl.BlockSpec((ATTN_WIDTH, d), lambda i: (0, 0)),
                  pl.BlockSpec((tm, d), lambda i: (i, 0))],
        out_specs=pl.BlockSpec((per, ATTN_WIDTH, ATT_BLOCK), lambda i: (i, 0, 0)),
        out_shape=jax.ShapeDtypeStruct((t // ATT_BLOCK, ATTN_WIDTH, ATT_BLOCK), BF16),
        compiler_params=_cparams(("parallel",)),
        name="proj_v_t",
    )(wt, hb)


def _proj_gate_kernel(x_ref, w_ref, b_ref, o_ref):
    acc = jnp.dot(x_ref[...], w_ref[...], preferred_element_type=F32) + b_ref[...]
    o_ref[...] = jax.nn.sigmoid(acc).astype(BF16)


def _proj_gate(hb, w, b):
    t, d = hb.shape
    n = w.shape[1]
    tm = _pick_tile(t, (512, 256))
    tn = 1024
    return pl.pallas_call(
        _proj_gate_kernel,
        grid=(t // tm, n // tn),
        in_specs=[pl.BlockSpec((tm, d), lambda i, j: (i, 0)),
                  pl.BlockSpec((d, tn), lambda i, j: (0, j)),
                  pl.BlockSpec((1, tn), lambda i, j: (0, j))],
        out_specs=pl.BlockSpec((tm, tn), lambda i, j: (i, j)),
        out_shape=jax.ShapeDtypeStruct((t, n), BF16),
        compiler_params=_cparams(("parallel", "parallel")),
        name="proj_gate",
    )(hb, w, b.reshape(1, n))


SCAN_STRIP = 512


def _ssm_kernel(u_ref, bdb_ref, bdc_ref, pw_ref, dskip_ref, wglu_ref, wout_ref, y_ref, bu_ref, carry_ref):
    tm = u_ref.shape[0]

    @pl.when(pl.program_id(1) == 0)
    def _():
        carry_ref[...] = jnp.zeros_like(carry_ref)

    u = u_ref[...]
    bu_ref[...] = jnp.dot(u.astype(BF16), bdb_ref[...], preferred_element_type=F32)

    row = lax.broadcasted_iota(jnp.int32, (SUBLANES, SCAN_STRIP), 0)
    for s in range(SSM_CH // SCAN_STRIP):
        re = slice(s * SCAN_STRIP, (s + 1) * SCAN_STRIP)
        im = slice(SSM_CH + s * SCAN_STRIP, SSM_CH + (s + 1) * SCAN_STRIP)
        steps = []
        for k, d in enumerate((1, 2, 4)):
            steps.append((d, jnp.broadcast_to(pw_ref[k:k + 1, re], (SUBLANES, SCAN_STRIP)),
                          jnp.broadcast_to(pw_ref[k:k + 1, im], (SUBLANES, SCAN_STRIP))))
        pr = pw_ref[SUBLANES:2 * SUBLANES, re]
        pi = pw_ref[SUBLANES:2 * SUBLANES, im]

        def group(gi, carry):
            cr, ci = carry
            r0 = pl.multiple_of(gi * SUBLANES, SUBLANES)
            xr = bu_ref[pl.ds(r0, SUBLANES), re]
            xi = bu_ref[pl.ds(r0, SUBLANES), im]
            for d, ar, ai in steps:
                sr = jnp.where(row >= d, pltpu.roll(xr, d, 0), 0.0)
                si = jnp.where(row >= d, pltpu.roll(xi, d, 0), 0.0)
                xr, xi = xr + ar * sr - ai * si, xi + ar * si + ai * sr
            xr, xi = xr + pr * cr - pi * ci, xi + pr * ci + pi * cr
            bu_ref[pl.ds(r0, SUBLANES), re] = xr
            bu_ref[pl.ds(r0, SUBLANES), im] = xi
            last_r = jnp.broadcast_to(xr[SUBLANES - 1:SUBLANES, :], (SUBLANES, SCAN_STRIP))
            last_i = jnp.broadcast_to(xi[SUBLANES - 1:SUBLANES, :], (SUBLANES, SCAN_STRIP))
            return last_r, last_i

        cr, ci = lax.fori_loop(0, tm // SUBLANES, group, (carry_ref[:, re], carry_ref[:, im]))
        carry_ref[:, re] = cr
        carry_ref[:, im] = ci

    y = jnp.dot(bu_ref[...].astype(BF16), bdc_ref[...], preferred_element_type=F32) + dskip_ref[...] * u
    y = jax.nn.gelu(y)
    y = y * jax.nn.sigmoid(jnp.dot(y.astype(BF16), wglu_ref[...], preferred_element_type=F32))
    y_ref[...] = jnp.dot(y.astype(BF16), wout_ref[...], preferred_element_type=F32)


def _ssm_branch(u, bdb, bdc, pw, dskip, wglu, wout, *, bsz, lp):
    t = u.shape[0]
    tm = _pick_tile(lp, (256,))
    tps = lp // tm
    const = lambda shape: pl.BlockSpec(shape, lambda b, i: (0, 0))
    return pl.pallas_call(
        _ssm_kernel,
        grid=(bsz, tps),
        in_specs=[pl.BlockSpec((tm, SSM_WIDTH), lambda b, i: (b * tps + i, 0)),
                  const(bdb.shape), const(bdc.shape), const(pw.shape), const((1, SSM_WIDTH)),
                  const(wglu.shape), const(wout.shape)],
        out_specs=pl.BlockSpec((tm, D_MODEL), lambda b, i: (b * tps + i, 0)),
        out_shape=jax.ShapeDtypeStruct((t, D_MODEL), F32),
        scratch_shapes=[pltpu.VMEM((tm, 2 * SSM_CH), F32), pltpu.VMEM((SUBLANES, 2 * SSM_CH), F32)],
        compiler_params=_cparams(("arbitrary", "arbitrary")),
        name="ssm_branch",
    )(u, bdb, bdc, pw, dskip.reshape(1, SSM_WIDTH), wglu, wout)


def _ssm_tables(lam_re, lam_im, log_dt, b_re, b_im, c_re, c_im):
    lr, li = lam_re.astype(F32), lam_im.astype(F32)
    dt = jnp.exp(log_dt.astype(F32))[:, None]
    mag = jnp.exp(lr * dt)
    ar = mag * jnp.cos(li * dt)
    ai = mag * jnp.sin(li * dt)
    denom = lr * lr + li * li
    nr, ni = ar - 1.0, ai
    coef_r = (nr * lr + ni * li) / denom
    coef_i = (ni * lr - nr * li) / denom
    br, bi = b_re.astype(F32), b_im.astype(F32)
    bbar_r = coef_r[..., None] * br - coef_i[..., None] * bi
    bbar_i = coef_r[..., None] * bi + coef_i[..., None] * br
    eye = jnp.eye(SSM_GROUPS, dtype=F32)
    bd_in = lambda m: jnp.einsum('gpc,gh->gchp', m, eye).reshape(SSM_WIDTH, SSM_CH)
    bdb = jnp.concatenate([bd_in(bbar_r), bd_in(bbar_i)], axis=1)
    bd_out = lambda m: jnp.einsum('gcp,gh->gphc', m, eye).reshape(SSM_CH, SSM_WIDTH)
    bdc = jnp.concatenate([bd_out(c_re.astype(F32)), -bd_out(c_im.astype(F32))], axis=0)
    a1 = (ar.reshape(-1), ai.reshape(-1))
    cmul = lambda x, y: (x[0] * y[0] - x[1] * y[1], x[0] * y[1] + x[1] * y[0])
    pows = [a1]
    for _ in range(SUBLANES - 1):
        pows.append(cmul(pows[-1], a1))
    rows = [pows[0], pows[1], pows[3]] + [(jnp.zeros_like(a1[0]),) * 2] * (SUBLANES - 3) + pows
    pw = jnp.stack([jnp.concatenate(r) for r in rows])
    return bdb.astype(BF16), bdc.astype(BF16), pw


def _conv_kernel(cb_ref, z_ref, zprev_ref, w_ref, wout_ref, y_ref):
    tm = z_ref.shape[0]
    z = z_ref[...]
    zz = jnp.concatenate([zprev_ref[...], z], axis=0)
    w = w_ref[...]
    y = w[2:3, :] * z
    for j in range(CONV_K - 1):
        shift = CONV_K - 1 - j
        y = y + w[j:j + 1, :] * zz[SUBLANES - shift:SUBLANES - shift + tm, :]
    y = cb_ref[...] * y
    y_ref[...] = jnp.dot(y.astype(BF16), wout_ref[...], preferred_element_type=F32)


def _conv_branch(cb, z, w, wout):
    t = z.shape[0]
    tm = _pick_tile(t, (512, 256))
    per = tm // SUBLANES
    return pl.pallas_call(
        _conv_kernel,
        grid=(t // tm,),
        in_specs=[pl.BlockSpec((tm, CONV_WIDTH), lambda i: (i, 0)),
                  pl.BlockSpec((tm, CONV_WIDTH), lambda i: (i, 0)),
                  pl.BlockSpec((SUBLANES, CONV_WIDTH), lambda i: (jnp.maximum(i * per - 1, 0), 0)),
                  pl.BlockSpec((SUBLANES, CONV_WIDTH), lambda i: (0, 0)),
                  pl.BlockSpec((CONV_WIDTH, D_MODEL), lambda i: (0, 0))],
        out_specs=pl.BlockSpec((tm, D_MODEL), lambda i: (i, 0)),
        out_shape=jax.ShapeDtypeStruct((t, D_MODEL), F32),
        compiler_params=_cparams(("parallel",)),
        name="conv_branch",
    )(cb, z, z, jnp.pad(w, ((0, SUBLANES - CONV_K), (0, 0))), wout)


ATT_SUB = 3


def _attn_kernel(lam_ref, q_ref, k_ref, vt_ref, g_ref, o_ref,
                 qlo_ref, qhi_ref, m_ref, l_ref, acc_ref, s_ref, *, front, lam_init):
    sb = pl.program_id(2)
    blk = ATT_BLOCK
    real0 = front + N_META
    nt = (((1,), (1,)), ((), ()))

    q = q_ref[...]
    lane = lax.broadcasted_iota(jnp.int32, q.shape, 1)
    qlo_ref[...] = jnp.where(lane < ATTN_HEAD_DIM, q, jnp.zeros_like(q))
    qhi_ref[...] = jnp.where(lane >= ATTN_HEAD_DIM, q, jnp.zeros_like(q))
    m_ref[...] = jnp.full_like(m_ref, NEG_INF)
    l_ref[...] = jnp.zeros_like(l_ref)
    acc_ref[...] = jnp.zeros_like(acc_ref)

    def qk(a, k):
        rows = slice(a * blk, (a + 1) * blk)
        return [lax.dot_general(k, qref[rows, :], nt, preferred_element_type=F32) for qref in (qlo_ref, qhi_ref)]

    def consume(a, j, scores, vt, masked):
        if masked:
            pk = j * blk + lax.broadcasted_iota(jnp.int32, (blk, 1), 0)
            pq = (sb * ATT_SUB + a) * blk + lax.broadcasted_iota(jnp.int32, (1, blk), 1)
            cid_k = jnp.where(pk < front, 1 << 30, jnp.where(pk < real0, 0, 1 + (pk - real0) // CHUNK))
            cid_q = jnp.where(pq < real0, 0, 1 + (pq - real0) // CHUNK)
            vis = cid_k <= cid_q
            scores = [jnp.where(vis, s, NEG_INF) for s in scores]
        for idx, s in enumerate(scores):
            m_old = m_ref[a, idx]
            m_new = jnp.maximum(m_old, jnp.max(s, axis=0, keepdims=True))
            alpha = jnp.exp2(m_old - m_new)
            p = jnp.exp2(s - m_new)
            l_ref[a, idx] = alpha * l_ref[a, idx] + jnp.sum(p, axis=0, keepdims=True)
            m_ref[a, idx] = m_new
            acc_ref[a, idx] = alpha * acc_ref[a, idx] + jnp.dot(vt, p.astype(BF16), preferred_element_type=F32)

    def k_block(j):
        return k_ref[pl.ds(pl.multiple_of(j * blk, blk), blk), :]

    def kv_block(j):
        return k_block(j), vt_ref[j]

    def qk_all(k, buf):
        for a in range(ATT_SUB):
            for idx, s in enumerate(qk(a, k)):
                s_ref[buf, a, idx] = s

    def consume_all(j, buf, vt, masked):
        for a in range(ATT_SUB):
            consume(a, j, [s_ref[buf, a, 0], s_ref[buf, a, 1]], vt, masked)

    @pl.when(sb > 0)
    def _():
        k, vt = kv_block(0)
        qk_all(k, 0)
        consume_all(0, 0, vt, True)

    n_full = jnp.maximum(sb * ATT_SUB - 1, 0)
    n_trips = n_full // 2

    @pl.when(n_trips > 0)
    def _():
        qk_all(k_block(1), 0)

    def body(jj, carry):
        j = 1 + 2 * jj
        k1, vt1 = kv_block(j + 1)
        qk_all(k1, 1)
        consume_all(j, 0, vt_ref[j], False)
        qk_all(k_block(jnp.minimum(j + 2, n_full)), 0)
        consume_all(j + 1, 1, vt1, False)
        return carry

    lax.fori_loop(0, n_trips, body, 0)

    @pl.when(n_full % 2 == 1)
    def _():
        k, vt = kv_block(n_full)
        qk_all(k, 0)
        consume_all(n_full, 0, vt, False)

    band = [(dj, a) for dj in range(ATT_SUB) for a in range(dj, ATT_SUB)]
    assert len(band) <= 2 * ATT_SUB
    for n_pair, (dj, a) in enumerate(band):
        for idx, s in enumerate(qk(a, k_block(sb * ATT_SUB + dj))):
            s_ref[n_pair // ATT_SUB, n_pair % ATT_SUB, idx] = s
    for n_pair, (dj, a) in enumerate(band):
        j = sb * ATT_SUB + dj
        slot = (n_pair // ATT_SUB, n_pair % ATT_SUB)
        consume(a, j, [s_ref[slot[0], slot[1], 0], s_ref[slot[0], slot[1], 1]], vt_ref[j], True)

    lam = lam_ref[0]
    for a in range(ATT_SUB):
        o = acc_ref[a, 0] / l_ref[a, 0] - lam * (acc_ref[a, 1] / l_ref[a, 1])
        o = o * lax.rsqrt(jnp.mean(o * o, axis=0, keepdims=True) + RMS_EPS)
        o = o * g_ref[...] * (1.0 - lam_init)
        o_ref[a * blk:(a + 1) * blk, :] = o.T.astype(BF16)


def _attention(lam, q, k, vt3, g_col, *, bsz, lp, front, lam_init):
    t = q.shape[0]
    blk = ATT_BLOCK
    nb = lp // blk
    assert nb % ATT_SUB == 0
    nsb = nb // ATT_SUB
    tq = ATT_SUB * blk
    return pl.pallas_call(
        functools.partial(_attn_kernel, front=front, lam_init=lam_init),
        grid=(bsz, ATTN_HEADS, nsb),
        in_specs=[pl.BlockSpec(memory_space=pltpu.SMEM),
                  pl.BlockSpec((tq, LANES), lambda b, h, i: (b * nsb + i, h)),
                  pl.BlockSpec((lp, LANES), lambda b, h, i: (b, h)),
                  pl.BlockSpec((nb, LANES, blk), lambda b, h, i: (b, h, 0)),
                  pl.BlockSpec((LANES, blk), lambda b, h, i: (0, 0))],
        out_specs=pl.BlockSpec((tq, LANES), lambda b, h, i: (b * nsb + i, h)),
        out_shape=jax.ShapeDtypeStruct((t, ATTN_WIDTH), BF16),
        scratch_shapes=[pltpu.VMEM((tq, LANES), BF16), pltpu.VMEM((tq, LANES), BF16),
                        pltpu.VMEM((ATT_SUB, 2, 1, blk), F32), pltpu.VMEM((ATT_SUB, 2, 1, blk), F32),
                        pltpu.VMEM((ATT_SUB, 2, LANES, blk), F32),
                        pltpu.VMEM((2, ATT_SUB, 2, blk, blk), F32)],
        compiler_params=_cparams(("parallel", "parallel", "arbitrary")),
        name="diff_attention",
    )(lam, q, k, vt3, g_col)


def _merge_kernel(h_ref, g_ref, ys_ref, yc_ref, o_ref, wao_ref, wo_ref, lg_ref, lb_ref,
                  h1_ref, h1b_ref, *, tiles_per_seq, front):
    tm = h_ref.shape[0]
    ya = jnp.dot(o_ref[...], wao_ref[...], preferred_element_type=F32)
    g = g_ref[...].astype(F32)
    merged = g[:, :D_MODEL] * ys_ref[...] + g[:, D_MODEL:2 * D_MODEL] * yc_ref[...] + g[:, 2 * D_MODEL:] * ya
    r = DEEPNORM_ALPHA * h_ref[...] + jnp.dot(merged.astype(BF16), wo_ref[...], preferred_element_type=F32)
    y = _layer_norm(r, lg_ref[...], lb_ref[...])
    pos = _seq_pos(pl.program_id(0), tiles_per_seq, tm)
    y = jnp.where(pos >= front, y, 0.0)
    h1_ref[...] = y
    h1b_ref[...] = y.astype(BF16)


def _merge(h, g, ys, yc, o, wao, wo, lg, lb, *, lp, front):
    t, d = h.shape
    tm = _pick_tile(lp, (256,))
    row = lambda w: pl.BlockSpec((tm, w), lambda i: (i, 0))
    const = lambda a, b: pl.BlockSpec((a, b), lambda i: (0, 0))
    return pl.pallas_call(
        functools.partial(_merge_kernel, tiles_per_seq=lp // tm, front=front),
        grid=(t // tm,),
        in_specs=[row(d), row(3 * d), row(d), row(d), row(ATTN_WIDTH),
                  const(ATTN_WIDTH, d), const(d, d), const(1, d), const(1, d)],
        out_specs=[row(d), row(d)],
        out_shape=[jax.ShapeDtypeStruct((t, d), F32), jax.ShapeDtypeStruct((t, d), BF16)],
        compiler_params=_cparams(("parallel",)),
        name="merge_ln1",
    )(h, g, ys, yc, o, wao, wo, lg.reshape(1, d), lb.reshape(1, d))


def _router_kernel(h_ref, w_ref, b_ref, ids_ref, wts_ref, rank_ref, cnt_ref, base_ref):
    tm = h_ref.shape[0]

    @pl.when(pl.program_id(0) == 0)
    def _():
        base_ref[...] = jnp.zeros_like(base_ref)

    logits = jnp.dot(h_ref[...], w_ref[...], preferred_element_type=F32,
                     precision=lax.Precision.HIGHEST) + b_ref[...]
    lane = lax.broadcasted_iota(jnp.int32, logits.shape, 1).astype(F32)
    work = logits
    sels, vals, firsts = [], [], []
    for _ in range(TOP_K):
        mx = jnp.max(work, axis=1, keepdims=True)
        first = jnp.min(jnp.where(work == mx, lane, float(N_EXPERTS)), axis=1, keepdims=True)
        sel = lane == first
        sels.append(sel)
        vals.append(mx)
        firsts.append(first)
        work = jnp.where(sel, -jnp.inf, work)
    exps = [jnp.exp(v - vals[0]) for v in vals]
    den = exps[0] + exps[1] + exps[2] + exps[3]

    onehot = jnp.zeros_like(logits)
    for sel in sels:
        onehot = onehot + jnp.where(sel, 1.0, 0.0)
    r = lax.broadcasted_iota(jnp.int32, (tm, tm), 0)
    c = lax.broadcasted_iota(jnp.int32, (tm, tm), 1)
    tri = jnp.where(r > c, 1.0, 0.0).astype(BF16)
    before = jnp.dot(tri, onehot.astype(BF16), preferred_element_type=F32) + base_ref[...]
    base_ref[...] = base_ref[...] + jnp.sum(onehot, axis=0, keepdims=True)
    cnt_ref[...] = base_ref[...]

    out_lane = lax.broadcasted_iota(jnp.int32, (tm, LANES), 1)
    ids = jnp.zeros((tm, LANES), F32)
    wts = jnp.zeros((tm, LANES), F32)
    rank = jnp.zeros((tm, LANES), F32)
    for kk in range(TOP_K):
        rk = jnp.sum(jnp.where(sels[kk], before, 0.0), axis=1, keepdims=True)
        ids = jnp.where(out_lane == kk, firsts[kk], ids)
        wts = jnp.where(out_lane == kk, exps[kk] / den, wts)
        rank = jnp.where(out_lane == kk, rk, rank)
    ids_ref[...] = ids.astype(jnp.int32)
    wts_ref[...] = wts
    rank_ref[...] = rank.astype(jnp.int32)


def _router(h, w, b):
    t, d = h.shape
    tm = _pick_tile(t, (512, 256))
    wide = lambda dt: jax.ShapeDtypeStruct((t, LANES), dt)
    row = pl.BlockSpec((tm, LANES), lambda i: (i, 0))
    return pl.pallas_call(
        _router_kernel,
        grid=(t // tm,),
        in_specs=[pl.BlockSpec((tm, d), lambda i: (i, 0)),
                  pl.BlockSpec((d, N_EXPERTS), lambda i: (0, 0)),
                  pl.BlockSpec((1, N_EXPERTS), lambda i: (0, 0))],
        out_specs=[row, row, row, pl.BlockSpec((1, N_EXPERTS), lambda i: (0, 0))],
        out_shape=[wide(jnp.int32), wide(F32), wide(jnp.int32), jax.ShapeDtypeStruct((1, N_EXPERTS), F32)],
        scratch_shapes=[pltpu.VMEM((1, N_EXPERTS), F32)],
        compiler_params=_cparams(("arbitrary",)),
        name="moe_router",
    )(h, w, b.reshape(1, N_EXPERTS))


MOE_TILE = 256
ROUTE_TOK = 256


def _dispatch_kernel(last_ref, nu_ref, dest_ref, h_ref, xs_ref, zero_ref, stage_ref, zsem, sems):
    i = pl.program_id(0)
    n = pl.num_programs(0)
    n_tiles = xs_ref.shape[0] // MOE_TILE

    def fill(tile):
        return pltpu.make_async_copy(zero_ref, xs_ref.at[pl.ds(tile * MOE_TILE, MOE_TILE)], zsem)

    @pl.when(i == 0)
    def _():
        zero_ref[...] = jnp.zeros_like(zero_ref)
        for e in range(N_EXPERTS):
            @pl.when(last_ref[e] >= 0)
            def _():
                fill(last_ref[e]).start()

        def start_unused(tile, carry):
            fill(tile).start()
            return carry

        def wait_unused(tile, carry):
            fill(tile).wait()
            return carry

        lax.fori_loop(nu_ref[0], n_tiles, start_unused, 0)
        for e in range(N_EXPERTS):
            @pl.when(last_ref[e] >= 0)
            def _():
                fill(last_ref[e]).wait()
        lax.fori_loop(nu_ref[0], n_tiles, wait_unused, 0)

    slot = lax.rem(i, 2)
    stage_ref[slot] = h_ref[...]

    def issue(r, carry):
        for kk in range(TOP_K):
            d = dest_ref[r * TOP_K + kk]
            pltpu.make_async_copy(stage_ref.at[slot, pl.ds(r, 1)], xs_ref.at[pl.ds(d, 1)],
                                  sems.at[slot]).start(priority=kk % 2)
        return carry

    lax.fori_loop(0, ROUTE_TOK, issue, 0)

    def drain(s):
        for _ in range(TOP_K):
            pltpu.make_async_copy(stage_ref.at[s], xs_ref.at[pl.ds(0, ROUTE_TOK)], sems.at[s]).wait()

    @pl.when(i > 0)
    def _():
        drain(1 - slot)

    @pl.when(i == n - 1)
    def _():
        drain(slot)


def _dispatch(h, dest_flat, last_tile, n_used, n_rows):
    t, d = h.shape
    return pl.pallas_call(
        _dispatch_kernel,
        grid_spec=pltpu.PrefetchScalarGridSpec(
            num_scalar_prefetch=2,
            grid=(t // ROUTE_TOK,),
            in_specs=[pl.BlockSpec((ROUTE_TOK * TOP_K,), lambda i, last, nu: (i,), memory_space=pltpu.SMEM),
                      pl.BlockSpec((ROUTE_TOK, d), lambda i, last, nu: (i, 0))],
            out_specs=pl.BlockSpec(memory_space=pl.ANY),
            scratch_shapes=[pltpu.VMEM((MOE_TILE, d), F32), pltpu.VMEM((2, ROUTE_TOK, d), F32),
                            pltpu.SemaphoreType.DMA(()), pltpu.SemaphoreType.DMA((2,))]),
        out_shape=jax.ShapeDtypeStruct((n_rows, d), F32),
        compiler_params=_cparams(("arbitrary",)),
        name="moe_dispatch",
    )(last_tile, n_used, dest_flat, h)


def _experts_kernel(te_ref, nu_ref, xs_ref, wgu_ref, bgu_ref, wd_ref, bd_ref, y_ref, wgu_b_ref, wd_b_ref):
    i = pl.program_id(0)

    @pl.when(jnp.logical_or(i == 0, te_ref[i] != te_ref[jnp.maximum(i - 1, 0)]))
    def _():
        wgu_b_ref[...] = wgu_ref[0].astype(BF16)
        wd_b_ref[...] = wd_ref[0].astype(BF16)

    @pl.when(i >= nu_ref[0])
    def _():
        y_ref[...] = jnp.zeros_like(y_ref)

    @pl.when(i < nu_ref[0])
    def _():
        gu = jnp.dot(xs_ref[...].astype(BF16), wgu_b_ref[...], preferred_element_type=F32) + bgu_ref[0]
        gate = jnp.minimum(gu[:, :D_FF], SWIGLU_LIMIT)
        up = jnp.clip(gu[:, D_FF:], -SWIGLU_LIMIT, SWIGLU_LIMIT)
        hid = (up + 1.0) * gate * jax.nn.sigmoid(SWIGLU_ALPHA * gate)
        y_ref[...] = jnp.dot(hid.astype(BF16), wd_b_ref[...], preferred_element_type=F32) + bd_ref[0]


def _experts(tile_expert, n_used, xs, wgu, bgu, wd, bd, *, layer):
    n_rows, d = xs.shape
    n_exp = wgu.shape[1]
    rows = lambda i, te, nu: (jnp.minimum(i, nu[0] - 1), 0)
    per_e = lambda i, te, nu: (te[i], 0, 0)
    per_le = lambda i, te, nu: (layer, te[i], 0, 0)
    return pl.pallas_call(
        _experts_kernel,
        grid_spec=pltpu.PrefetchScalarGridSpec(
            num_scalar_prefetch=2,
            grid=(n_rows // MOE_TILE,),
            in_specs=[pl.BlockSpec((MOE_TILE, d), rows),
                      pl.BlockSpec((None, 1, d, 2 * D_FF), per_le),
                      pl.BlockSpec((1, 1, 2 * D_FF), per_e),
                      pl.BlockSpec((None, 1, D_FF, d), per_le),
                      pl.BlockSpec((1, 1, d), per_e)],
            out_specs=pl.BlockSpec((MOE_TILE, d), lambda i, te, nu: (i, 0)),
            scratch_shapes=[pltpu.VMEM((d, 2 * D_FF), BF16), pltpu.VMEM((D_FF, d), BF16)]),
        out_shape=jax.ShapeDtypeStruct((n_rows, d), F32),
        compiler_params=_cparams(("arbitrary",)),
        name="moe_experts",
    )(tile_expert, n_used, xs, wgu, bgu.reshape(n_exp, 1, 2 * D_FF), wd, bd.reshape(n_exp, 1, d))


COMBINE_TOK = 128


def _combine_kernel(dcur_ref, dnext_ref, w_ref, h_ref, y_ref, lg_ref, lb_ref, h2_ref, h2b_ref,
                    ybuf_ref, sems, *, tiles_per_seq, front):
    i = pl.program_id(0)
    n = pl.num_programs(0)
    slot = lax.rem(i, 2)

    def issue(d_ref, s):
        def body(r, carry):
            for kk in range(TOP_K):
                d = d_ref[r * TOP_K + kk]
                pltpu.make_async_copy(y_ref.at[pl.ds(d, 1)], ybuf_ref.at[s, kk, pl.ds(r, 1)],
                                      sems.at[s]).start(priority=kk % 2)
            return carry
        lax.fori_loop(0, COMBINE_TOK, body, 0)

    @pl.when(i == 0)
    def _():
        issue(dcur_ref, 0)

    @pl.when(i + 1 < n)
    def _():
        issue(dnext_ref, 1 - slot)

    for kk in range(TOP_K):
        pltpu.make_async_copy(y_ref.at[pl.ds(0, COMBINE_TOK)], ybuf_ref.at[slot, kk], sems.at[slot]).wait()

    w = w_ref[...]
    acc = w[:, 0:1] * ybuf_ref[slot, 0]
    for kk in range(1, TOP_K):
        acc = acc + w[:, kk:kk + 1] * ybuf_ref[slot, kk]
    y = _layer_norm(DEEPNORM_ALPHA * h_ref[...] + acc, lg_ref[...], lb_ref[...])
    pos = _seq_pos(i, tiles_per_seq, COMBINE_TOK)
    y = jnp.where(pos >= front, y, 0.0)
    h2_ref[...] = y
    h2b_ref[...] = y.astype(BF16)


def _combine(dest_flat, wts, h, y, lg, lb, *, lp, front):
    t, d = h.shape
    n = t // COMBINE_TOK
    blk = COMBINE_TOK * TOP_K
    row = lambda w: pl.BlockSpec((COMBINE_TOK, w), lambda i: (i, 0))
    const = pl.BlockSpec((1, d), lambda i: (0, 0))
    return pl.pallas_call(
        functools.partial(_combine_kernel, tiles_per_seq=lp // COMBINE_TOK, front=front),
        grid=(n,),
        in_specs=[pl.BlockSpec((blk,), lambda i: (i,), memory_space=pltpu.SMEM),
                  pl.BlockSpec((blk,), lambda i: (jnp.minimum(i + 1, n - 1),), memory_space=pltpu.SMEM),
                  row(LANES), row(d), pl.BlockSpec(memory_space=pl.ANY), const, const],
        out_specs=[row(d), row(d)],
        out_shape=[jax.ShapeDtypeStruct((t, d), F32), jax.ShapeDtypeStruct((t, d), BF16)],
        scratch_shapes=[pltpu.VMEM((2, TOP_K, COMBINE_TOK, d), F32), pltpu.SemaphoreType.DMA((2,))],
        compiler_params=_cparams(("arbitrary",)),
        name="moe_combine_ln2",
    )(dest_flat, dest_flat, wts, h, y, lg.reshape(1, d), lb.reshape(1, d))


def _routing_plan(ids, rank, counts, n_tiles):
    n_exp = counts.shape[0]
    padded = (counts + MOE_TILE - 1) // MOE_TILE * MOE_TILE
    ends = jnp.cumsum(padded)
    starts = ends - padded
    dest = (jnp.take(starts, ids) + rank).reshape(-1)
    n_used = (ends[-1] // MOE_TILE).reshape(1)
    tile_start = jnp.arange(n_tiles, dtype=jnp.int32) * MOE_TILE
    tile_expert = jnp.sum((tile_start[:, None] >= ends[None, :]).astype(jnp.int32), axis=1)
    last_used_expert = jnp.max(jnp.where(padded > 0, jnp.arange(n_exp, dtype=jnp.int32), 0))
    tile_expert = jnp.minimum(tile_expert, last_used_expert)
    last_tile = jnp.where(padded > 0, ends // MOE_TILE - 1, -1)
    return dest.astype(jnp.int32), tile_expert.astype(jnp.int32), n_used.astype(jnp.int32), last_tile.astype(jnp.int32)


def _rope_tables(lp, front):
    pos = (jnp.arange(lp, dtype=jnp.int32) - front).astype(F32)
    inv = ROPE_THETA ** (-jnp.arange(0, ATTN_HEAD_DIM, 2, dtype=F32) / ATTN_HEAD_DIM)
    ang = pos[:, None] * inv[None, :]
    ang = jnp.concatenate([ang, ang, ang, ang], axis=-1)
    return jnp.cos(ang), jnp.sin(ang)


def kernel(x, meta_tokens, ln_in_g, ln_in_b, w_in, ssm_lambda_re, ssm_lambda_im, ssm_log_dt, ssm_b_re, ssm_b_im, ssm_c_re, ssm_c_im, ssm_d, ssm_w_glu, ssm_w_out, conv_w, conv_w_out, attn_lambda_q1, attn_lambda_k1, attn_lambda_q2, attn_lambda_k2, attn_subln_g, attn_w_out, gate_w, gate_b, w_o, ln1_g, ln1_b, router_w, router_b, expert_w_gu, expert_b_gu, expert_w_down, expert_b_down, ln2_g, ln2_b):
    bsz, seq, d = x.shape
    assert d == D_MODEL and seq % ATT_BLOCK == 0
    lp = seq + ATT_BLOCK
    front = ATT_BLOCK - N_META
    t = bsz * lp

    meta = jnp.broadcast_to(meta_tokens.astype(x.dtype)[None], (bsz, N_META, d))
    hcat = jnp.concatenate([jnp.zeros((bsz, front, d), x.dtype), meta, x], axis=1).reshape(t, d)
    h, hb = _ln_in(hcat, ln_in_g, ln_in_b, lp=lp, front=front)
    cos, sin = _rope_tables(lp, front)

    s3 = SSM_WIDTH + 3 * CONV_WIDTH
    s5 = s3 + 2 * ATTN_WIDTH
    for l in range(DEPTH):
        w_in_b = w_in[l].astype(BF16)
        u, cb, z = _proj_sc(hb, w_in_b[:, :s3])
        q, k = _proj_qk(hb, w_in_b[:, s3:s5], cos, sin, lp=lp)
        vt = _proj_vt(hb, w_in_b[:, s5:].T)
        g = _proj_gate(hb, gate_w[l].astype(BF16), gate_b[l])

        bdb, bdc, pw = _ssm_tables(ssm_lambda_re[l], ssm_lambda_im[l], ssm_log_dt[l], ssm_b_re[l], ssm_b_im[l],
                                   ssm_c_re[l], ssm_c_im[l])
        ys = _ssm_branch(u, bdb, bdc, pw, ssm_d[l].astype(F32), ssm_w_glu[l].astype(BF16),
                         ssm_w_out[l].astype(BF16), bsz=bsz, lp=lp)
        yc = _conv_branch(cb, z, conv_w[l].astype(F32), conv_w_out[l].astype(BF16))

        lam_init = 0.8 - 0.6 * math.exp(-0.3 * l)
        lam = (jnp.exp(jnp.sum(attn_lambda_q1[l].astype(F32) * attn_lambda_k1[l].astype(F32)))
               - jnp.exp(jnp.sum(attn_lambda_q2[l].astype(F32) * attn_lambda_k2[l].astype(F32)))
               + lam_init).reshape(1)
        g_col = jnp.broadcast_to(attn_subln_g[l].astype(F32)[:, None], (ATTN_VALUE_DIM, ATT_BLOCK))
        o = _attention(lam, q, k, vt, g_col, bsz=bsz, lp=lp, front=front, lam_init=lam_init)

        h, hb = _merge(h, g, ys, yc, o, attn_w_out[l].astype(BF16), w_o[l].astype(BF16), ln1_g[l], ln1_b[l],
                       lp=lp, front=front)

        ids, wts, rank, counts = _router(h, router_w[l].astype(F32), router_b[l].astype(F32))
        n_tiles = t * TOP_K // MOE_TILE + N_EXPERTS
        dest, tile_expert, n_used, last_tile = _routing_plan(ids[:, :TOP_K], rank[:, :TOP_K],
                                                             counts[0].astype(jnp.int32), n_tiles)
        xs = _dispatch(h, dest, last_tile, n_used, n_tiles * MOE_TILE)
        y = _experts(tile_expert, n_used, xs, expert_w_gu.astype(F32), expert_b_gu[l].astype(F32),
                     expert_w_down.astype(F32), expert_b_down[l].astype(F32), layer=l)
        h, hb = _combine(dest, wts, h, y, ln2_g[l], ln2_b[l], lp=lp, front=front)

    return h.reshape(bsz, lp, d)[:, ATT_BLOCK:]
```

```python
import functools
import math

import jax
import jax.numpy as jnp
from jax import lax
from jax.experimental import pallas as pl
from jax.experimental.pallas import tpu as pltpu

F32 = jnp.float32
BF16 = jnp.bfloat16

D_MODEL = 1024
DEPTH = 2
CHUNK = 64
N_META = 16
SSM_WIDTH = 512
SSM_GROUP = 16
SSM_GROUPS = 32
SSM_STATE = 64
SSM_CH = SSM_GROUPS * SSM_STATE
CONV_WIDTH = 512
CONV_K = 3
ATTN_HEADS = 8
ATTN_HEAD_DIM = 64
ATTN_VALUE_DIM = 128
ATTN_WIDTH = 1024
ROPE_THETA = 10000.0
N_EXPERTS = 32
TOP_K = 4
D_FF = 1024
SWIGLU_LIMIT = 7.0
SWIGLU_ALPHA = 1.702
DEEPNORM_ALPHA = (2.0 * DEPTH) ** 0.25
LN_EPS = 1e-5
RMS_EPS = 1e-5
NEG_INF = -1e30

ATT_BLOCK = 256
LANES = 128
SUBLANES = 8
VMEM_LIMIT = 48 * 1024 * 1024


def _cparams(sem):
    return pltpu.CompilerParams(dimension_semantics=sem, vmem_limit_bytes=VMEM_LIMIT)


def _pick_tile(n, candidates):
    for c in candidates:
        if n % c == 0:
            return c
    raise ValueError(f"no tile for {n}")


def _layer_norm(x, g, b):
    mu = jnp.mean(x, axis=-1, keepdims=True)
    xc = x - mu
    var = jnp.mean(xc * xc, axis=-1, keepdims=True)
    return xc * lax.rsqrt(var + LN_EPS) * g + b


def _seq_pos(tile_idx, tiles_per_seq, tm):
    base = lax.rem(tile_idx, tiles_per_seq) * tm
    return base + lax.broadcasted_iota(jnp.int32, (tm, 1), 0)


def _ln_in_kernel(x_ref, g_ref, b_ref, h_ref, hb_ref, *, tiles_per_seq, front):
    tm = x_ref.shape[0]
    y = _layer_norm(x_ref[...], g_ref[...], b_ref[...])
    pos = _seq_pos(pl.program_id(0), tiles_per_seq, tm)
    y = jnp.where(pos >= front, y, 0.0)
    h_ref[...] = y
    hb_ref[...] = y.astype(BF16)


def _ln_in(hcat, g, b, *, lp, front):
    t, d = hcat.shape
    tm = _pick_tile(lp, (512, 256))
    return pl.pallas_call(
        functools.partial(_ln_in_kernel, tiles_per_seq=lp // tm, front=front),
        grid=(t // tm,),
        in_specs=[pl.BlockSpec((tm, d), lambda i: (i, 0)),
                  pl.BlockSpec((1, d), lambda i: (0, 0)),
                  pl.BlockSpec((1, d), lambda i: (0, 0))],
        out_specs=[pl.BlockSpec((tm, d), lambda i: (i, 0)),
                   pl.BlockSpec((tm, d), lambda i: (i, 0))],
        out_shape=[jax.ShapeDtypeStruct((t, d), F32), jax.ShapeDtypeStruct((t, d), BF16)],
        compiler_params=_cparams(("parallel",)),
        name="ln_in",
    )(hcat, g.reshape(1, d), b.reshape(1, d))


def _proj_sc_kernel(x_ref, w_ref, u_ref, cb_ref, z_ref):
    acc = jnp.dot(x_ref[...], w_ref[...], preferred_element_type=F32)
    s0, s1, s2, s3 = SSM_WIDTH, SSM_WIDTH + CONV_WIDTH, SSM_WIDTH + 2 * CONV_WIDTH, SSM_WIDTH + 3 * CONV_WIDTH
    u_ref[...] = acc[:, :s0]
    cb_ref[...] = acc[:, s0:s1]
    z_ref[...] = acc[:, s1:s2] * acc[:, s2:s3]


def _proj_sc(hb, w):
    t, d = hb.shape
    n = w.shape[1]
    tm = _pick_tile(t, (704, 512, 256))
    out = jax.ShapeDtypeStruct((t, SSM_WIDTH), F32)
    ospec = pl.BlockSpec((tm, SSM_WIDTH), lambda i: (i, 0))
    return pl.pallas_call(
        _proj_sc_kernel,
        grid=(t // tm,),
        in_specs=[pl.BlockSpec((tm, d), lambda i: (i, 0)),
                  pl.BlockSpec((d, n), lambda i: (0, 0))],
        out_specs=[ospec, ospec, ospec],
        out_shape=[out, out, out],
        compiler_params=_cparams(("parallel",)),
        name="proj_ssm_conv",
    )(hb, w)


def _proj_qk_kernel(x_ref, w_ref, cos_ref, sin_ref, q_ref, k_ref):
    acc = jnp.dot(x_ref[...], w_ref[...], preferred_element_type=F32)
    cos = cos_ref[...]
    sin = sin_ref[...]
    lane = lax.broadcasted_iota(jnp.int32, (1, LANES), 1)
    low_half = lax.rem(lane, ATTN_HEAD_DIM) < ATTN_HEAD_DIM // 2
    nblk = ATTN_WIDTH // LANES
    for blk in range(2 * nblk):
        a = acc[:, blk * LANES:(blk + 1) * LANES]
        rot = jnp.where(low_half, -pltpu.roll(a, LANES - ATTN_HEAD_DIM // 2, 1),
                        pltpu.roll(a, ATTN_HEAD_DIM // 2, 1))
        r = a * cos + rot * sin
        if blk < nblk:
            q_ref[:, blk * LANES:(blk + 1) * LANES] = (r * (ATTN_HEAD_DIM ** -0.5 * math.log2(math.e))).astype(BF16)
        else:
            k_ref[:, (blk - nblk) * LANES:(blk - nblk + 1) * LANES] = r.astype(BF16)


def _proj_qk(hb, w, cos, sin, *, lp):
    t, d = hb.shape
    tm = _pick_tile(lp, (704, 512, 256))
    tps = lp // tm
    out = jax.ShapeDtypeStruct((t, ATTN_WIDTH), BF16)
    return pl.pallas_call(
        _proj_qk_kernel,
        grid=(t // tm,),
        in_specs=[pl.BlockSpec((tm, d), lambda i: (i, 0)),
                  pl.BlockSpec((d, 2 * ATTN_WIDTH), lambda i: (0, 0)),
                  pl.BlockSpec((tm, LANES), lambda i: (i % tps, 0)),
                  pl.BlockSpec((tm, LANES), lambda i: (i % tps, 0))],
        out_specs=[pl.BlockSpec((tm, ATTN_WIDTH), lambda i: (i, 0)),
                   pl.BlockSpec((tm, ATTN_WIDTH), lambda i: (i, 0))],
        out_shape=[out, out],
        compiler_params=_cparams(("parallel",)),
        name="proj_qk_rope",
    )(hb, w, cos, sin)


def _proj_vt_kernel(wt_ref, x_ref, o_ref):
    vt = lax.dot_general(wt_ref[...], x_ref[...], (((1,), (1,)), ((), ())),
                         preferred_element_type=F32).astype(BF16)
    for c in range(o_ref.shape[0]):
        o_ref[c] = vt[:, c * ATT_BLOCK:(c + 1) * ATT_BLOCK]


def _proj_vt(hb, wt):
    t, d = hb.shape
    tm = _pick_tile(t, (512, 256))
    per = tm // ATT_BLOCK
    return pl.pallas_call(
        _proj_vt_kernel,
        grid=(t // tm,),
        in_specs=[pl.BlockSpec((ATTN_WIDTH, d), lambda i: (0, 0)),
                  pl.BlockSpec((tm, d), lambda i: (i, 0))],
        out_specs=pl.BlockSpec((per, ATTN_WIDTH, ATT_BLOCK), lambda i: (i, 0, 0)),
        out_shape=jax.ShapeDtypeStruct((t // ATT_BLOCK, ATTN_WIDTH, ATT_BLOCK), BF16),
        compiler_params=_cparams(("parallel",)),
        name="proj_v_t",
    )(wt, hb)


def _proj_gate_kernel(x_ref, w_ref, b_ref, o_ref):
    acc = jnp.dot(x_ref[...], w_ref[...], preferred_element_type=F32) + b_ref[...]
    o_ref[...] = jax.nn.sigmoid(acc).astype(BF16)


def _proj_gate(hb, w, b):
    t, d = hb.shape
    n = w.shape[1]
    tm = _pick_tile(t, (1408, 512, 256))
    tn = 1024
    return pl.pallas_call(
        _proj_gate_kernel,
        grid=(t // tm, n // tn),
        in_specs=[pl.BlockSpec((tm, d), lambda i, j: (i, 0)),
                  pl.BlockSpec((d, tn), lambda i, j: (0, j)),
                  pl.BlockSpec((1, tn), lambda i, j: (0, j))],
        out_specs=pl.BlockSpec((tm, tn), lambda i, j: (i, j)),
        out_shape=jax.ShapeDtypeStruct((t, n), BF16),
        compiler_params=_cparams(("parallel", "parallel")),
        name="proj_gate",
    )(hb, w, b.reshape(1, n))


SCAN_STRIP = 512
SSM_SUPER = 2


def _ssm_kernel(u_ref, bdb_ref, bdc_ref, pw_ref, dskip_ref, wglu_ref, wout_ref, y_ref, bu_ref, carry_ref):
    tm = u_ref.shape[0]

    @pl.when(pl.program_id(1) == 0)
    def _():
        carry_ref[...] = jnp.zeros_like(carry_ref)

    u = u_ref[...]
    ub = u.astype(BF16)
    sgw = SSM_WIDTH // SSM_SUPER
    sgc = SSM_CH // SSM_SUPER
    for sg in range(SSM_SUPER):
        bu = jnp.dot(ub[:, sg * sgw:(sg + 1) * sgw], bdb_ref[sg], preferred_element_type=F32)
        bu_ref[:, sg * sgc:(sg + 1) * sgc] = bu[:, :sgc]
        bu_ref[:, SSM_CH + sg * sgc:SSM_CH + (sg + 1) * sgc] = bu[:, sgc:]

    for s in range(SSM_CH // SCAN_STRIP):
        re = slice(s * SCAN_STRIP, (s + 1) * SCAN_STRIP)
        im = slice(SSM_CH + s * SCAN_STRIP, SSM_CH + (s + 1) * SCAN_STRIP)
        steps = []
        for k, d in enumerate((1, 2, 4)):
            rows = slice(k * SUBLANES, (k + 1) * SUBLANES)
            steps.append((d, pw_ref[rows, re], pw_ref[rows, im]))
        pr = pw_ref[3 * SUBLANES:4 * SUBLANES, re]
        pi = pw_ref[3 * SUBLANES:4 * SUBLANES, im]

        def group(gi, carry):
            cr, ci = carry
            r0 = pl.multiple_of(gi * SUBLANES, SUBLANES)
            xr = bu_ref[pl.ds(r0, SUBLANES), re]
            xi = bu_ref[pl.ds(r0, SUBLANES), im]
            for d, ar, ai in steps:
                sr = pltpu.roll(xr, d, 0)
                si = pltpu.roll(xi, d, 0)
                xr, xi = xr + ar * sr - ai * si, xi + ar * si + ai * sr
            xr, xi = xr + pr * cr - pi * ci, xi + pr * ci + pi * cr
            bu_ref[pl.ds(r0, SUBLANES), re] = xr
            bu_ref[pl.ds(r0, SUBLANES), im] = xi
            last_r = jnp.broadcast_to(xr[SUBLANES - 1:SUBLANES, :], (SUBLANES, SCAN_STRIP))
            last_i = jnp.broadcast_to(xi[SUBLANES - 1:SUBLANES, :], (SUBLANES, SCAN_STRIP))
            return last_r, last_i

        cr, ci = lax.fori_loop(0, tm // SUBLANES, group, (carry_ref[:, re], carry_ref[:, im]))
        carry_ref[:, re] = cr
        carry_ref[:, im] = ci

    ys = []
    for sg in range(SSM_SUPER):
        xr = bu_ref[:, sg * sgc:(sg + 1) * sgc].astype(BF16)
        xi = bu_ref[:, SSM_CH + sg * sgc:SSM_CH + (sg + 1) * sgc].astype(BF16)
        ys.append(jnp.dot(xr, bdc_ref[sg, 0], preferred_element_type=F32)
                  + jnp.dot(xi, bdc_ref[sg, 1], preferred_element_type=F32))
    y = jnp.concatenate(ys, axis=1) + dskip_ref[...] * u
    y = jax.nn.gelu(y)
    y = y * jax.nn.sigmoid(jnp.dot(y.astype(BF16), wglu_ref[...], preferred_element_type=F32))
    y_ref[...] = jnp.dot(y.astype(BF16), wout_ref[...], preferred_element_type=F32)


def _ssm_branch(u, bdb, bdc, pw, dskip, wglu, wout, *, bsz, lp):
    t = u.shape[0]
    tm = _pick_tile(lp, (256,))
    tps = lp // tm
    const = lambda shape: pl.BlockSpec(shape, lambda b, i: (0,) * len(shape))
    return pl.pallas_call(
        _ssm_kernel,
        grid=(bsz, tps),
        in_specs=[pl.BlockSpec((tm, SSM_WIDTH), lambda b, i: (b * tps + i, 0)),
                  const(bdb.shape), const(bdc.shape), const(pw.shape), const((1, SSM_WIDTH)),
                  const(wglu.shape), const(wout.shape)],
        out_specs=pl.BlockSpec((tm, D_MODEL), lambda b, i: (b * tps + i, 0)),
        out_shape=jax.ShapeDtypeStruct((t, D_MODEL), F32),
        scratch_shapes=[pltpu.VMEM((tm, 2 * SSM_CH), F32), pltpu.VMEM((SUBLANES, 2 * SSM_CH), F32)],
        compiler_params=_cparams(("arbitrary", "arbitrary")),
        name="ssm_branch",
    )(u, bdb, bdc, pw, dskip.reshape(1, SSM_WIDTH), wglu, wout)


def _ssm_tables(lam_re, lam_im, log_dt, b_re, b_im, c_re, c_im):
    lr, li = lam_re.astype(F32), lam_im.astype(F32)
    dt = jnp.exp(log_dt.astype(F32))[:, None]
    mag = jnp.exp(lr * dt)
    ar = mag * jnp.cos(li * dt)
    ai = mag * jnp.sin(li * dt)
    denom = lr * lr + li * li
    nr, ni = ar - 1.0, ai
    coef_r = (nr * lr + ni * li) / denom
    coef_i = (ni * lr - nr * li) / denom
    br, bi = b_re.astype(F32), b_im.astype(F32)
    bbar_r = coef_r[..., None] * br - coef_i[..., None] * bi
    bbar_i = coef_r[..., None] * bi + coef_i[..., None] * br
    gps = SSM_GROUPS // SSM_SUPER
    eye = jnp.eye(gps, dtype=F32)
    sgw, sgc = SSM_WIDTH // SSM_SUPER, SSM_CH // SSM_SUPER
    split = lambda m: m.reshape((SSM_SUPER, gps) + m.shape[1:])
    bd_in = lambda m: jnp.einsum('sgpc,gh->sgchp', split(m), eye).reshape(SSM_SUPER, sgw, sgc)
    bdb = jnp.concatenate([bd_in(bbar_r), bd_in(bbar_i)], axis=2)
    bd_out = lambda m: jnp.einsum('sgcp,gh->sgphc', split(m), eye).reshape(SSM_SUPER, sgc, sgw)
    bdc = jnp.stack([bd_out(c_re.astype(F32)), -bd_out(c_im.astype(F32))], axis=1)
    a1 = (ar.reshape(-1), ai.reshape(-1))
    cmul = lambda x, y: (x[0] * y[0] - x[1] * y[1], x[0] * y[1] + x[1] * y[0])
    pows = [a1]
    for _ in range(SUBLANES - 1):
        pows.append(cmul(pows[-1], a1))
    row = jnp.arange(SUBLANES)[:, None]
    both = lambda p: jnp.concatenate(p)[None, :]
    tables = [jnp.where(row >= d, both(pows[d - 1]), 0.0) for d in (1, 2, 4)]
    tables.append(jnp.concatenate([both(p) for p in pows], axis=0))
    pw = jnp.concatenate(tables, axis=0)
    return bdb.astype(BF16), bdc.astype(BF16), pw


def _conv_kernel(cb_ref, z_ref, zprev_ref, w_ref, wout_ref, y_ref):
    tm = z_ref.shape[0]
    z = z_ref[...]
    zz = jnp.concatenate([zprev_ref[...], z], axis=0)
    w = w_ref[...]
    y = w[2:3, :] * z
    for j in range(CONV_K - 1):
        shift = CONV_K - 1 - j
        y = y + w[j:j + 1, :] * zz[SUBLANES - shift:SUBLANES - shift + tm, :]
    y = cb_ref[...] * y
    y_ref[...] = jnp.dot(y.astype(BF16), wout_ref[...], preferred_element_type=F32)


def _conv_branch(cb, z, w, wout):
    t = z.shape[0]
    tm = _pick_tile(t, (512, 256))
    per = tm // SUBLANES
    return pl.pallas_call(
        _conv_kernel,
        grid=(t // tm,),
        in_specs=[pl.BlockSpec((tm, CONV_WIDTH), lambda i: (i, 0)),
                  pl.BlockSpec((tm, CONV_WIDTH), lambda i: (i, 0)),
                  pl.BlockSpec((SUBLANES, CONV_WIDTH), lambda i: (jnp.maximum(i * per - 1, 0), 0)),
                  pl.BlockSpec((SUBLANES, CONV_WIDTH), lambda i: (0, 0)),
                  pl.BlockSpec((CONV_WIDTH, D_MODEL), lambda i: (0, 0))],
        out_specs=pl.BlockSpec((tm, D_MODEL), lambda i: (i, 0)),
        out_shape=jax.ShapeDtypeStruct((t, D_MODEL), F32),
        compiler_params=_cparams(("parallel",)),
        name="conv_branch",
    )(cb, z, z, jnp.pad(w, ((0, SUBLANES - CONV_K), (0, 0))), wout)


ATT_SUB = 3


def _attn_kernel(lam_ref, q_ref, k_ref, vt_ref, g_ref, o_ref,
                 qlo_ref, qhi_ref, m_ref, l_ref, acc_ref, s_ref, *, front, lam_init):
    sb = pl.program_id(2)
    blk = ATT_BLOCK
    real0 = front + N_META
    nt = (((1,), (1,)), ((), ()))

    q = q_ref[...]
    lane = lax.broadcasted_iota(jnp.int32, q.shape, 1)
    qlo_ref[...] = jnp.where(lane < ATTN_HEAD_DIM, q, jnp.zeros_like(q))
    qhi_ref[...] = jnp.where(lane >= ATTN_HEAD_DIM, q, jnp.zeros_like(q))
    m_ref[...] = jnp.full_like(m_ref, NEG_INF)
    l_ref[...] = jnp.zeros_like(l_ref)
    acc_ref[...] = jnp.zeros_like(acc_ref)

    def qk(a, k):
        rows = slice(a * blk, (a + 1) * blk)
        return [lax.dot_general(k, qref[rows, :], nt, preferred_element_type=F32) for qref in (qlo_ref, qhi_ref)]

    def consume(a, j, scores, vt, masked):
        if masked:
            pk = j * blk + lax.broadcasted_iota(jnp.int32, (blk, 1), 0)
            pq = (sb * ATT_SUB + a) * blk + lax.broadcasted_iota(jnp.int32, (1, blk), 1)
            cid_k = jnp.where(pk < front, 1 << 30, jnp.where(pk < real0, 0, 1 + (pk - real0) // CHUNK))
            cid_q = jnp.where(pq < real0, 0, 1 + (pq - real0) // CHUNK)
            vis = cid_k <= cid_q
            scores = [jnp.where(vis, s, NEG_INF) for s in scores]
        for idx, s in enumerate(scores):
            m_old = m_ref[a, idx]
            m_new = jnp.maximum(m_old, jnp.max(s, axis=0, keepdims=True))
            alpha = jnp.exp2(m_old - m_new)
            p = jnp.exp2(s - m_new)
            l_ref[a, idx] = alpha * l_ref[a, idx] + jnp.sum(p, axis=0, keepdims=True)
            m_ref[a, idx] = m_new
            acc_ref[a, idx] = alpha * acc_ref[a, idx] + jnp.dot(vt, p.astype(BF16), preferred_element_type=F32)

    def k_block(j):
        return k_ref[pl.ds(pl.multiple_of(j * blk, blk), blk), :]

    def kv_block(j):
        return k_block(j), vt_ref[j]

    def qk_all(k, buf):
        for a in range(ATT_SUB):
            for idx, s in enumerate(qk(a, k)):
                s_ref[buf, a, idx] = s

    def consume_all(j, buf, vt, masked):
        for a in range(ATT_SUB):
            consume(a, j, [s_ref[buf, a, 0], s_ref[buf, a, 1]], vt, masked)

    @pl.when(sb > 0)
    def _():
        k, vt = kv_block(0)
        qk_all(k, 0)
        consume_all(0, 0, vt, True)

    n_full = jnp.maximum(sb * ATT_SUB - 1, 0)
    n_trips = n_full // 2

    @pl.when(n_trips > 0)
    def _():
        qk_all(k_block(1), 0)

    def body(jj, carry):
        j = 1 + 2 * jj
        k1, vt1 = kv_block(j + 1)
        qk_all(k1, 1)
        consume_all(j, 0, vt_ref[j], False)
        qk_all(k_block(jnp.minimum(j + 2, n_full)), 0)
        consume_all(j + 1, 1, vt1, False)
        return carry

    lax.fori_loop(0, n_trips, body, 0)

    @pl.when(n_full % 2 == 1)
    def _():
        k, vt = kv_block(n_full)
        qk_all(k, 0)
        consume_all(n_full, 0, vt, False)

    band = [(dj, a) for dj in range(ATT_SUB) for a in range(dj, ATT_SUB)]
    assert len(band) <= 2 * ATT_SUB
    for n_pair, (dj, a) in enumerate(band):
        for idx, s in enumerate(qk(a, k_block(sb * ATT_SUB + dj))):
            s_ref[n_pair // ATT_SUB, n_pair % ATT_SUB, idx] = s
    for n_pair, (dj, a) in enumerate(band):
        j = sb * ATT_SUB + dj
        slot = (n_pair // ATT_SUB, n_pair % ATT_SUB)
        consume(a, j, [s_ref[slot[0], slot[1], 0], s_ref[slot[0], slot[1], 1]], vt_ref[j], True)

    lam = lam_ref[0]
    for a in range(ATT_SUB):
        o = acc_ref[a, 0] / l_ref[a, 0] - lam * (acc_ref[a, 1] / l_ref[a, 1])
        o = o * lax.rsqrt(jnp.mean(o * o, axis=0, keepdims=True) + RMS_EPS)
        o = o * g_ref[...] * (1.0 - lam_init)
        o_ref[a * blk:(a + 1) * blk, :] = o.T.astype(BF16)


def _attention(lam, q, k, vt3, g_col, *, bsz, lp, front, lam_init):
    t = q.shape[0]
    blk = ATT_BLOCK
    nb = lp // blk
    assert nb % ATT_SUB == 0
    nsb = nb // ATT_SUB
    tq = ATT_SUB * blk
    return pl.pallas_call(
        functools.partial(_attn_kernel, front=front, lam_init=lam_init),
        grid=(bsz, ATTN_HEADS, nsb),
        in_specs=[pl.BlockSpec(memory_space=pltpu.SMEM),
                  pl.BlockSpec((tq, LANES), lambda b, h, i: (b * nsb + i, h)),
                  pl.BlockSpec((lp, LANES), lambda b, h, i: (b, h)),
                  pl.BlockSpec((nb, LANES, blk), lambda b, h, i: (b, h, 0)),
                  pl.BlockSpec((LANES, blk), lambda b, h, i: (0, 0))],
        out_specs=pl.BlockSpec((tq, LANES), lambda b, h, i: (b * nsb + i, h)),
        out_shape=jax.ShapeDtypeStruct((t, ATTN_WIDTH), BF16),
        scratch_shapes=[pltpu.VMEM((tq, LANES), BF16), pltpu.VMEM((tq, LANES), BF16),
                        pltpu.VMEM((ATT_SUB, 2, 1, blk), F32), pltpu.VMEM((ATT_SUB, 2, 1, blk), F32),
                        pltpu.VMEM((ATT_SUB, 2, LANES, blk), F32),
                        pltpu.VMEM((2, ATT_SUB, 2, blk, blk), F32)],
        compiler_params=_cparams(("parallel", "parallel", "arbitrary")),
        name="diff_attention",
    )(lam, q, k, vt3, g_col)


def _merge_kernel(h_ref, g_ref, ys_ref, yc_ref, o_ref, wao_ref, wo_ref, lg_ref, lb_ref,
                  h1_ref, h1b_ref, *, tiles_per_seq, front):
    tm = h_ref.shape[0]
    ya = jnp.dot(o_ref[...], wao_ref[...], preferred_element_type=F32)
    g = g_ref[...].astype(F32)
    merged = g[:, :D_MODEL] * ys_ref[...] + g[:, D_MODEL:2 * D_MODEL] * yc_ref[...] + g[:, 2 * D_MODEL:] * ya
    r = DEEPNORM_ALPHA * h_ref[...] + jnp.dot(merged.astype(BF16), wo_ref[...], preferred_element_type=F32)
    y = _layer_norm(r, lg_ref[...], lb_ref[...])
    pos = _seq_pos(pl.program_id(0), tiles_per_seq, tm)
    y = jnp.where(pos >= front, y, 0.0)
    h1_ref[...] = y
    h1b_ref[...] = y.astype(BF16)


def _merge(h, g, ys, yc, o, wao, wo, lg, lb, *, lp, front):
    t, d = h.shape
    tm = _pick_tile(lp, (256,))
    row = lambda w: pl.BlockSpec((tm, w), lambda i: (i, 0))
    const = lambda a, b: pl.BlockSpec((a, b), lambda i: (0, 0))
    return pl.pallas_call(
        functools.partial(_merge_kernel, tiles_per_seq=lp // tm, front=front),
        grid=(t // tm,),
        in_specs=[row(d), row(3 * d), row(d), row(d), row(ATTN_WIDTH),
                  const(ATTN_WIDTH, d), const(d, d), const(1, d), const(1, d)],
        out_specs=[row(d), row(d)],
        out_shape=[jax.ShapeDtypeStruct((t, d), F32), jax.ShapeDtypeStruct((t, d), BF16)],
        compiler_params=_cparams(("parallel",)),
        name="merge_ln1",
    )(h, g, ys, yc, o, wao, wo, lg.reshape(1, d), lb.reshape(1, d))


def _router_kernel(h_ref, w_ref, b_ref, ids_ref, wts_ref, rank_ref, cnt_ref, base_ref):
    tm = h_ref.shape[0]

    @pl.when(pl.program_id(0) == 0)
    def _():
        base_ref[...] = jnp.zeros_like(base_ref)

    logits = jnp.dot(h_ref[...], w_ref[...], preferred_element_type=F32,
                     precision=lax.Precision.HIGHEST) + b_ref[...]
    lane = lax.broadcasted_iota(jnp.int32, logits.shape, 1).astype(F32)
    work = logits
    sels, vals, firsts = [], [], []
    for _ in range(TOP_K):
        mx = jnp.max(work, axis=1, keepdims=True)
        first = jnp.min(jnp.where(work == mx, lane, float(N_EXPERTS)), axis=1, keepdims=True)
        sel = lane == first
        sels.append(sel)
        vals.append(mx)
        firsts.append(first)
        work = jnp.where(sel, -jnp.inf, work)
    exps = [jnp.exp(v - vals[0]) for v in vals]
    den = exps[0] + exps[1] + exps[2] + exps[3]

    onehot = jnp.zeros_like(logits)
    for sel in sels:
        onehot = onehot + jnp.where(sel, 1.0, 0.0)
    r = lax.broadcasted_iota(jnp.int32, (tm, tm), 0)
    c = lax.broadcasted_iota(jnp.int32, (tm, tm), 1)
    tri = jnp.where(r > c, 1.0, 0.0).astype(BF16)
    before = jnp.dot(tri, onehot.astype(BF16), preferred_element_type=F32) + base_ref[...]
    base_ref[...] = base_ref[...] + jnp.sum(onehot, axis=0, keepdims=True)
    cnt_ref[...] = base_ref[...]

    out_lane = lax.broadcasted_iota(jnp.int32, (tm, LANES), 1)
    ids = jnp.zeros((tm, LANES), F32)
    wts = jnp.zeros((tm, LANES), F32)
    rank = jnp.zeros((tm, LANES), F32)
    for kk in range(TOP_K):
        rk = jnp.sum(jnp.where(sels[kk], before, 0.0), axis=1, keepdims=True)
        ids = jnp.where(out_lane == kk, firsts[kk], ids)
        wts = jnp.where(out_lane == kk, exps[kk] / den, wts)
        rank = jnp.where(out_lane == kk, rk, rank)
    ids_ref[...] = ids.astype(jnp.int32)
    wts_ref[...] = wts
    rank_ref[...] = rank.astype(jnp.int32)


def _router(h, w, b):
    t, d = h.shape
    tm = _pick_tile(t, (512, 256))
    wide = lambda dt: jax.ShapeDtypeStruct((t, LANES), dt)
    row = pl.BlockSpec((tm, LANES), lambda i: (i, 0))
    return pl.pallas_call(
        _router_kernel,
        grid=(t // tm,),
        in_specs=[pl.BlockSpec((tm, d), lambda i: (i, 0)),
                  pl.BlockSpec((d, N_EXPERTS), lambda i: (0, 0)),
                  pl.BlockSpec((1, N_EXPERTS), lambda i: (0, 0))],
        out_specs=[row, row, row, pl.BlockSpec((1, N_EXPERTS), lambda i: (0, 0))],
        out_shape=[wide(jnp.int32), wide(F32), wide(jnp.int32), jax.ShapeDtypeStruct((1, N_EXPERTS), F32)],
        scratch_shapes=[pltpu.VMEM((1, N_EXPERTS), F32)],
        compiler_params=_cparams(("arbitrary",)),
        name="moe_router",
    )(h, w, b.reshape(1, N_EXPERTS))


MOE_TILE = 256
ROUTE_TOK = 256


def _dispatch_kernel(last_ref, nu_ref, dest_ref, h_ref, xs_ref, zero_ref, stage_ref, zsem, sems):
    i = pl.program_id(0)
    n = pl.num_programs(0)
    n_tiles = xs_ref.shape[0] // MOE_TILE

    def fill(tile):
        return pltpu.make_async_copy(zero_ref, xs_ref.at[pl.ds(tile * MOE_TILE, MOE_TILE)], zsem)

    @pl.when(i == 0)
    def _():
        zero_ref[...] = jnp.zeros_like(zero_ref)
        for e in range(N_EXPERTS):
            @pl.when(last_ref[e] >= 0)
            def _():
                fill(last_ref[e]).start()

        def start_unused(tile, carry):
            fill(tile).start()
            return carry

        def wait_unused(tile, carry):
            fill(tile).wait()
            return carry

        lax.fori_loop(nu_ref[0], n_tiles, start_unused, 0)
        for e in range(N_EXPERTS):
            @pl.when(last_ref[e] >= 0)
            def _():
                fill(last_ref[e]).wait()
        lax.fori_loop(nu_ref[0], n_tiles, wait_unused, 0)

    slot = lax.rem(i, 2)
    stage_ref[slot] = h_ref[...]

    def issue(r8, carry):
        base = pl.multiple_of(r8 * SUBLANES, SUBLANES)
        for rr in range(SUBLANES):
            for kk in range(TOP_K):
                d = dest_ref[(base + rr) * TOP_K + kk]
                pltpu.make_async_copy(stage_ref.at[slot, pl.ds(base + rr, 1)], xs_ref.at[pl.ds(d, 1)],
                                      sems.at[slot]).start(priority=kk % 2)
        return carry

    lax.fori_loop(0, ROUTE_TOK // SUBLANES, issue, 0)

    def drain(s):
        for _ in range(TOP_K):
            pltpu.make_async_copy(stage_ref.at[s], xs_ref.at[pl.ds(0, ROUTE_TOK)], sems.at[s]).wait()

    @pl.when(i > 0)
    def _():
        drain(1 - slot)

    @pl.when(i == n - 1)
    def _():
        drain(slot)


def _dispatch(h, dest_flat, last_tile, n_used, n_rows):
    t, d = h.shape
    return pl.pallas_call(
        _dispatch_kernel,
        grid_spec=pltpu.PrefetchScalarGridSpec(
            num_scalar_prefetch=2,
            grid=(t // ROUTE_TOK,),
            in_specs=[pl.BlockSpec((ROUTE_TOK * TOP_K,), lambda i, last, nu: (i,), memory_space=pltpu.SMEM),
                      pl.BlockSpec((ROUTE_TOK, d), lambda i, last, nu: (i, 0))],
            out_specs=pl.BlockSpec(memory_space=pl.ANY),
            scratch_shapes=[pltpu.VMEM((MOE_TILE, d), F32), pltpu.VMEM((2, ROUTE_TOK, d), F32),
                            pltpu.SemaphoreType.DMA(()), pltpu.SemaphoreType.DMA((2,))]),
        out_shape=jax.ShapeDtypeStruct((n_rows, d), F32),
        compiler_params=_cparams(("arbitrary",)),
        name="moe_dispatch",
    )(last_tile, n_used, dest_flat, h)


def _experts_kernel(te_ref, nu_ref, xs_ref, wgu_ref, bgu_ref, wd_ref, bd_ref, y_ref, wgu_b_ref, wd_b_ref):
    i = pl.program_id(0)

    @pl.when(jnp.logical_or(i == 0, te_ref[i] != te_ref[jnp.maximum(i - 1, 0)]))
    def _():
        wgu_b_ref[...] = wgu_ref[0].astype(BF16)
        wd_b_ref[...] = wd_ref[0].astype(BF16)

    @pl.when(i >= nu_ref[0])
    def _():
        y_ref[...] = jnp.zeros_like(y_ref)

    @pl.when(i < nu_ref[0])
    def _():
        gu = jnp.dot(xs_ref[...].astype(BF16), wgu_b_ref[...], preferred_element_type=F32) + bgu_ref[0]
        gate = jnp.minimum(gu[:, :D_FF], SWIGLU_LIMIT)
        up = jnp.clip(gu[:, D_FF:], -SWIGLU_LIMIT, SWIGLU_LIMIT)
        hid = (up + 1.0) * gate * jax.nn.sigmoid(SWIGLU_ALPHA * gate)
        y_ref[...] = jnp.dot(hid.astype(BF16), wd_b_ref[...], preferred_element_type=F32) + bd_ref[0]


def _experts(tile_expert, n_used, xs, wgu, bgu, wd, bd, *, layer):
    n_rows, d = xs.shape
    n_exp = wgu.shape[1]
    rows = lambda i, te, nu: (jnp.minimum(i, nu[0] - 1), 0)
    per_e = lambda i, te, nu: (te[i], 0, 0)
    per_le = lambda i, te, nu: (layer, te[i], 0, 0)
    return pl.pallas_call(
        _experts_kernel,
        grid_spec=pltpu.PrefetchScalarGridSpec(
            num_scalar_prefetch=2,
            grid=(n_rows // MOE_TILE,),
            in_specs=[pl.BlockSpec((MOE_TILE, d), rows),
                      pl.BlockSpec((None, 1, d, 2 * D_FF), per_le),
                      pl.BlockSpec((1, 1, 2 * D_FF), per_e),
                      pl.BlockSpec((None, 1, D_FF, d), per_le),
                      pl.BlockSpec((1, 1, d), per_e)],
            out_specs=pl.BlockSpec((MOE_TILE, d), lambda i, te, nu: (i, 0)),
            scratch_shapes=[pltpu.VMEM((d, 2 * D_FF), BF16), pltpu.VMEM((D_FF, d), BF16)]),
        out_shape=jax.ShapeDtypeStruct((n_rows, d), F32),
        compiler_params=_cparams(("arbitrary",)),
        name="moe_experts",
    )(tile_expert, n_used, xs, wgu, bgu.reshape(n_exp, 1, 2 * D_FF), wd, bd.reshape(n_exp, 1, d))


COMBINE_TOK = 128


def _combine_kernel(dcur_ref, dnext_ref, w_ref, h_ref, y_ref, lg_ref, lb_ref, h2_ref, h2b_ref,
                    ybuf_ref, sems, *, tiles_per_seq, front):
    i = pl.program_id(0)
    n = pl.num_programs(0)
    slot = lax.rem(i, 2)

    def issue(d_ref, s):
        def body(r8, carry):
            base = pl.multiple_of(r8 * SUBLANES, SUBLANES)
            for rr in range(SUBLANES):
                for kk in range(TOP_K):
                    d = d_ref[(base + rr) * TOP_K + kk]
                    pltpu.make_async_copy(y_ref.at[pl.ds(d, 1)], ybuf_ref.at[s, kk, pl.ds(base + rr, 1)],
                                          sems.at[s]).start(priority=kk % 2)
            return carry
        lax.fori_loop(0, COMBINE_TOK // SUBLANES, body, 0)

    @pl.when(i == 0)
    def _():
        issue(dcur_ref, 0)

    @pl.when(i + 1 < n)
    def _():
        issue(dnext_ref, 1 - slot)

    for kk in range(TOP_K):
        pltpu.make_async_copy(y_ref.at[pl.ds(0, COMBINE_TOK)], ybuf_ref.at[slot, kk], sems.at[slot]).wait()

    w = w_ref[...]
    acc = w[:, 0:1] * ybuf_ref[slot, 0]
    for kk in range(1, TOP_K):
        acc = acc + w[:, kk:kk + 1] * ybuf_ref[slot, kk]
    y = _layer_norm(DEEPNORM_ALPHA * h_ref[...] + acc, lg_ref[...], lb_ref[...])
    pos = _seq_pos(i, tiles_per_seq, COMBINE_TOK)
    y = jnp.where(pos >= front, y, 0.0)
    h2_ref[...] = y
    h2b_ref[...] = y.astype(BF16)


def _combine(dest_flat, wts, h, y, lg, lb, *, lp, front, drop_prefix=False):
    t, d = h.shape
    n = t // COMBINE_TOK
    blk = COMBINE_TOK * TOP_K
    row = lambda w: pl.BlockSpec((COMBINE_TOK, w), lambda i: (i, 0))
    const = pl.BlockSpec((1, d), lambda i: (0, 0))
    out_rows, out_f32 = t, row(d)
    if drop_prefix:
        tps = lp // COMBINE_TOK
        skip = (front + N_META) // COMBINE_TOK
        out_rows = t - (t // lp) * (front + N_META)
        out_f32 = pl.BlockSpec((COMBINE_TOK, d),
                               lambda i: ((i // tps) * (tps - skip) + jnp.maximum(i % tps - skip, 0), 0))
    return pl.pallas_call(
        functools.partial(_combine_kernel, tiles_per_seq=lp // COMBINE_TOK, front=front),
        grid=(n,),
        in_specs=[pl.BlockSpec((blk,), lambda i: (i,), memory_space=pltpu.SMEM),
                  pl.BlockSpec((blk,), lambda i: (jnp.minimum(i + 1, n - 1),), memory_space=pltpu.SMEM),
                  row(LANES), row(d), pl.BlockSpec(memory_space=pl.ANY), const, const],
        out_specs=[out_f32, row(d)],
        out_shape=[jax.ShapeDtypeStruct((out_rows, d), F32), jax.ShapeDtypeStruct((t, d), BF16)],
        scratch_shapes=[pltpu.VMEM((2, TOP_K, COMBINE_TOK, d), F32), pltpu.SemaphoreType.DMA((2,))],
        compiler_params=_cparams(("arbitrary",)),
        name="moe_combine_ln2",
    )(dest_flat, dest_flat, wts, h, y, lg.reshape(1, d), lb.reshape(1, d))


def _routing_plan(ids, rank, counts, n_tiles):
    n_exp = counts.shape[0]
    padded = (counts + MOE_TILE - 1) // MOE_TILE * MOE_TILE
    ends = jnp.cumsum(padded)
    starts = ends - padded
    dest = (jnp.take(starts, ids) + rank).reshape(-1)
    n_used = (ends[-1] // MOE_TILE).reshape(1)
    tile_start = jnp.arange(n_tiles, dtype=jnp.int32) * MOE_TILE
    tile_expert = jnp.sum((tile_start[:, None] >= ends[None, :]).astype(jnp.int32), axis=1)
    last_used_expert = jnp.max(jnp.where(padded > 0, jnp.arange(n_exp, dtype=jnp.int32), 0))
    tile_expert = jnp.minimum(tile_expert, last_used_expert)
    last_tile = jnp.where(padded > 0, ends // MOE_TILE - 1, -1)
    return dest.astype(jnp.int32), tile_expert.astype(jnp.int32), n_used.astype(jnp.int32), last_tile.astype(jnp.int32)


def _rope_tables(lp, front):
    pos = (jnp.arange(lp, dtype=jnp.int32) - front).astype(F32)
    inv = ROPE_THETA ** (-jnp.arange(0, ATTN_HEAD_DIM, 2, dtype=F32) / ATTN_HEAD_DIM)
    ang = pos[:, None] * inv[None, :]
    ang = jnp.concatenate([ang, ang, ang, ang], axis=-1)
    return jnp.cos(ang), jnp.sin(ang)


def kernel(x, meta_tokens, ln_in_g, ln_in_b, w_in, ssm_lambda_re, ssm_lambda_im, ssm_log_dt, ssm_b_re, ssm_b_im, ssm_c_re, ssm_c_im, ssm_d, ssm_w_glu, ssm_w_out, conv_w, conv_w_out, attn_lambda_q1, attn_lambda_k1, attn_lambda_q2, attn_lambda_k2, attn_subln_g, attn_w_out, gate_w, gate_b, w_o, ln1_g, ln1_b, router_w, router_b, expert_w_gu, expert_b_gu, expert_w_down, expert_b_down, ln2_g, ln2_b):
    bsz, seq, d = x.shape
    assert d == D_MODEL and seq % ATT_BLOCK == 0
    lp = seq + ATT_BLOCK
    front = ATT_BLOCK - N_META
    t = bsz * lp

    meta = jnp.broadcast_to(meta_tokens.astype(x.dtype)[None], (bsz, N_META, d))
    hcat = jnp.concatenate([jnp.zeros((bsz, front, d), x.dtype), meta, x], axis=1).reshape(t, d)
    h, hb = _ln_in(hcat, ln_in_g, ln_in_b, lp=lp, front=front)
    cos, sin = _rope_tables(lp, front)

    s3 = SSM_WIDTH + 3 * CONV_WIDTH
    s5 = s3 + 2 * ATTN_WIDTH
    for l in range(DEPTH):
        w_in_b = w_in[l].astype(BF16)
        u, cb, z = _proj_sc(hb, w_in_b[:, :s3])
        q, k = _proj_qk(hb, w_in_b[:, s3:s5], cos, sin, lp=lp)
        vt = _proj_vt(hb, w_in_b[:, s5:].T)
        g = _proj_gate(hb, gate_w[l].astype(BF16), gate_b[l])

        bdb, bdc, pw = _ssm_tables(ssm_lambda_re[l], ssm_lambda_im[l], ssm_log_dt[l], ssm_b_re[l], ssm_b_im[l],
                                   ssm_c_re[l], ssm_c_im[l])
        ys = _ssm_branch(u, bdb, bdc, pw, ssm_d[l].astype(F32), ssm_w_glu[l].astype(BF16),
                         ssm_w_out[l].astype(BF16), bsz=bsz, lp=lp)
        yc = _conv_branch(cb, z, conv_w[l].astype(F32), conv_w_out[l].astype(BF16))

        lam_init = 0.8 - 0.6 * math.exp(-0.3 * l)
        lam = (jnp.exp(jnp.sum(attn_lambda_q1[l].astype(F32) * attn_lambda_k1[l].astype(F32)))
               - jnp.exp(jnp.sum(attn_lambda_q2[l].astype(F32) * attn_lambda_k2[l].astype(F32)))
               + lam_init).reshape(1)
        g_col = jnp.broadcast_to(attn_subln_g[l].astype(F32)[:, None], (ATTN_VALUE_DIM, ATT_BLOCK))
        o = _attention(lam, q, k, vt, g_col, bsz=bsz, lp=lp, front=front, lam_init=lam_init)

        h, hb = _merge(h, g, ys, yc, o, attn_w_out[l].astype(BF16), w_o[l].astype(BF16), ln1_g[l], ln1_b[l],
                       lp=lp, front=front)

        ids, wts, rank, counts = _router(h, router_w[l].astype(F32), router_b[l].astype(F32))
        n_tiles = t * TOP_K // MOE_TILE + N_EXPERTS
        dest, tile_expert, n_used, last_tile = _routing_plan(ids[:, :TOP_K], rank[:, :TOP_K],
                                                             counts[0].astype(jnp.int32), n_tiles)
        xs = _dispatch(h, dest, last_tile, n_used, n_tiles * MOE_TILE)
        y = _experts(tile_expert, n_used, xs, expert_w_gu.astype(F32), expert_b_gu[l].astype(F32),
                     expert_w_down.astype(F32), expert_b_down[l].astype(F32), layer=l)
        h, hb = _combine(dest, wts, h, y, ln2_g[l], ln2_b[l], lp=lp, front=front, drop_prefix=(l == DEPTH - 1))

    return h.reshape(bsz, seq, d)
```

```python
import functools
import math

import jax
import jax.numpy as jnp
from jax import lax
from jax.experimental import pallas as pl
from jax.experimental.pallas import tpu as pltpu

F32 = jnp.float32
BF16 = jnp.bfloat16

D_MODEL = 1024
DEPTH = 2
CHUNK = 64
N_META = 16
SSM_WIDTH = 512
SSM_GROUP = 16
SSM_GROUPS = 32
SSM_STATE = 64
SSM_CH = SSM_GROUPS * SSM_STATE
CONV_WIDTH = 512
CONV_K = 3
ATTN_HEADS = 8
ATTN_HEAD_DIM = 64
ATTN_VALUE_DIM = 128
ATTN_WIDTH = 1024
ROPE_THETA = 10000.0
N_EXPERTS = 32
TOP_K = 4
D_FF = 1024
SWIGLU_LIMIT = 7.0
SWIGLU_ALPHA = 1.702
DEEPNORM_ALPHA = (2.0 * DEPTH) ** 0.25
LN_EPS = 1e-5
RMS_EPS = 1e-5
NEG_INF = -1e30

ATT_BLOCK = 256
LANES = 128
SUBLANES = 8
VMEM_LIMIT = 48 * 1024 * 1024


def _cparams(sem):
    return pltpu.CompilerParams(dimension_semantics=sem, vmem_limit_bytes=VMEM_LIMIT)


def _pick_tile(n, candidates):
    for c in candidates:
        if n % c == 0:
            return c
    raise ValueError(f"no tile for {n}")


def _layer_norm(x, g, b):
    mu = jnp.mean(x, axis=-1, keepdims=True)
    xc = x - mu
    var = jnp.mean(xc * xc, axis=-1, keepdims=True)
    return xc * lax.rsqrt(var + LN_EPS) * g + b


def _seq_pos(tile_idx, tiles_per_seq, tm):
    base = lax.rem(tile_idx, tiles_per_seq) * tm
    return base + lax.broadcasted_iota(jnp.int32, (tm, 1), 0)


def _ln_in_kernel(x_ref, g_ref, b_ref, h_ref, hb_ref, *, tiles_per_seq, front):
    tm = x_ref.shape[0]
    y = _layer_norm(x_ref[...], g_ref[...], b_ref[...])
    pos = _seq_pos(pl.program_id(0), tiles_per_seq, tm)
    y = jnp.where(pos >= front, y, 0.0)
    h_ref[...] = y
    hb_ref[...] = y.astype(BF16)


def _ln_in(hcat, g, b, *, lp, front):
    t, d = hcat.shape
    tm = _pick_tile(lp, (512, 256))
    return pl.pallas_call(
        functools.partial(_ln_in_kernel, tiles_per_seq=lp // tm, front=front),
        grid=(t // tm,),
        in_specs=[pl.BlockSpec((tm, d), lambda i: (i, 0)),
                  pl.BlockSpec((1, d), lambda i: (0, 0)),
                  pl.BlockSpec((1, d), lambda i: (0, 0))],
        out_specs=[pl.BlockSpec((tm, d), lambda i: (i, 0)),
                   pl.BlockSpec((tm, d), lambda i: (i, 0))],
        out_shape=[jax.ShapeDtypeStruct((t, d), F32), jax.ShapeDtypeStruct((t, d), BF16)],
        compiler_params=_cparams(("parallel",)),
        name="ln_in",
    )(hcat, g.reshape(1, d), b.reshape(1, d))


def _proj_sc_kernel(x_ref, w_ref, u_ref, cb_ref, z_ref):
    acc = jnp.dot(x_ref[...], w_ref[...], preferred_element_type=F32)
    s0, s1, s2, s3 = SSM_WIDTH, SSM_WIDTH + CONV_WIDTH, SSM_WIDTH + 2 * CONV_WIDTH, SSM_WIDTH + 3 * CONV_WIDTH
    u_ref[...] = acc[:, :s0]
    cb_ref[...] = acc[:, s0:s1]
    z_ref[...] = acc[:, s1:s2] * acc[:, s2:s3]


def _proj_sc(hb, w):
    t, d = hb.shape
    n = w.shape[1]
    tm = _pick_tile(t, (704, 512, 256))
    out = jax.ShapeDtypeStruct((t, SSM_WIDTH), F32)
    ospec = pl.BlockSpec((tm, SSM_WIDTH), lambda i: (i, 0))
    return pl.pallas_call(
        _proj_sc_kernel,
        grid=(t // tm,),
        in_specs=[pl.BlockSpec((tm, d), lambda i: (i, 0)),
                  pl.BlockSpec((d, n), lambda i: (0, 0))],
        out_specs=[ospec, ospec, ospec],
        out_shape=[out, out, out],
        compiler_params=_cparams(("parallel",)),
        name="proj_ssm_conv",
    )(hb, w)


def _proj_qk_kernel(x_ref, w_ref, cos_ref, sin_ref, q_ref, k_ref):
    acc = jnp.dot(x_ref[...], w_ref[...], preferred_element_type=F32)
    cos = cos_ref[...]
    sin = sin_ref[...]
    lane = lax.broadcasted_iota(jnp.int32, (1, LANES), 1)
    low_half = lax.rem(lane, ATTN_HEAD_DIM) < ATTN_HEAD_DIM // 2
    nblk = ATTN_WIDTH // LANES
    for blk in range(2 * nblk):
        a = acc[:, blk * LANES:(blk + 1) * LANES]
        rot = jnp.where(low_half, -pltpu.roll(a, LANES - ATTN_HEAD_DIM // 2, 1),
                        pltpu.roll(a, ATTN_HEAD_DIM // 2, 1))
        r = a * cos + rot * sin
        if blk < nblk:
            q_ref[:, blk * LANES:(blk + 1) * LANES] = (r * (ATTN_HEAD_DIM ** -0.5 * math.log2(math.e))).astype(BF16)
        else:
            k_ref[:, (blk - nblk) * LANES:(blk - nblk + 1) * LANES] = r.astype(BF16)


def _proj_qk(hb, w, cos, sin, *, lp):
    t, d = hb.shape
    tm = _pick_tile(lp, (704, 512, 256))
    tps = lp // tm
    out = jax.ShapeDtypeStruct((t, ATTN_WIDTH), BF16)
    return pl.pallas_call(
        _proj_qk_kernel,
        grid=(t // tm,),
        in_specs=[pl.BlockSpec((tm, d), lambda i: (i, 0)),
                  pl.BlockSpec((d, 2 * ATTN_WIDTH), lambda i: (0, 0)),
                  pl.BlockSpec((tm, LANES), lambda i: (i % tps, 0)),
                  pl.BlockSpec((tm, LANES), lambda i: (i % tps, 0))],
        out_specs=[pl.BlockSpec((tm, ATTN_WIDTH), lambda i: (i, 0)),
                   pl.BlockSpec((tm, ATTN_WIDTH), lambda i: (i, 0))],
        out_shape=[out, out],
        compiler_params=_cparams(("parallel",)),
        name="proj_qk_rope",
    )(hb, w, cos, sin)


def _proj_vt_kernel(wt_ref, x_ref, o_ref):
    vt = lax.dot_general(wt_ref[...], x_ref[...], (((1,), (1,)), ((), ())),
                         preferred_element_type=F32).astype(BF16)
    vd = ATTN_VALUE_DIM
    row = lax.broadcasted_iota(jnp.int32, (ATT_VROWS - vd, ATT_BLOCK), 0)
    tail = jnp.where(row == 0, 1.0, 0.0).astype(BF16)
    for c in range(o_ref.shape[0]):
        for h in range(ATTN_HEADS):
            o_ref[c, h, :vd, :] = vt[h * vd:(h + 1) * vd, c * ATT_BLOCK:(c + 1) * ATT_BLOCK]
            o_ref[c, h, vd:, :] = tail


def _proj_vt(hb, wt):
    t, d = hb.shape
    tm = _pick_tile(t, (512, 256))
    per = tm // ATT_BLOCK
    return pl.pallas_call(
        _proj_vt_kernel,
        grid=(t // tm,),
        in_specs=[pl.BlockSpec((ATTN_WIDTH, d), lambda i: (0, 0)),
                  pl.BlockSpec((tm, d), lambda i: (i, 0))],
        out_specs=pl.BlockSpec((per, ATTN_HEADS, ATT_VROWS, ATT_BLOCK), lambda i: (i, 0, 0, 0)),
        out_shape=jax.ShapeDtypeStruct((t // ATT_BLOCK, ATTN_HEADS, ATT_VROWS, ATT_BLOCK), BF16),
        compiler_params=_cparams(("parallel",)),
        name="proj_v_t",
    )(wt, hb)


def _proj_gate_kernel(x_ref, w_ref, b_ref, o_ref):
    acc = jnp.dot(x_ref[...], w_ref[...], preferred_element_type=F32) + b_ref[...]
    o_ref[...] = jax.nn.sigmoid(acc).astype(BF16)


def _proj_gate(hb, w, b):
    t, d = hb.shape
    n = w.shape[1]
    tm = _pick_tile(t, (1408, 512, 256))
    tn = 1024
    return pl.pallas_call(
        _proj_gate_kernel,
        grid=(t // tm, n // tn),
        in_specs=[pl.BlockSpec((tm, d), lambda i, j: (i, 0)),
                  pl.BlockSpec((d, tn), lambda i, j: (0, j)),
                  pl.BlockSpec((1, tn), lambda i, j: (0, j))],
        out_specs=pl.BlockSpec((tm, tn), lambda i, j: (i, j)),
        out_shape=jax.ShapeDtypeStruct((t, n), BF16),
        compiler_params=_cparams(("parallel", "parallel")),
        name="proj_gate",
    )(hb, w, b.reshape(1, n))


SCAN_STRIP = 512
SSM_SUPER = 2


def _ssm_kernel(u_ref, bdb_ref, bdc_ref, pw_ref, dskip_ref, wglu_ref, wout_ref, y_ref, bu_ref, carry_ref):
    tm = u_ref.shape[0]

    @pl.when(pl.program_id(1) == 0)
    def _():
        carry_ref[...] = jnp.zeros_like(carry_ref)

    u = u_ref[...]
    ub = u.astype(BF16)
    sgw = SSM_WIDTH // SSM_SUPER
    sgc = SSM_CH // SSM_SUPER
    for sg in range(SSM_SUPER):
        bu = jnp.dot(ub[:, sg * sgw:(sg + 1) * sgw], bdb_ref[sg], preferred_element_type=F32)
        bu_ref[:, sg * sgc:(sg + 1) * sgc] = bu[:, :sgc]
        bu_ref[:, SSM_CH + sg * sgc:SSM_CH + (sg + 1) * sgc] = bu[:, sgc:]

    for s in range(SSM_CH // SCAN_STRIP):
        re = slice(s * SCAN_STRIP, (s + 1) * SCAN_STRIP)
        im = slice(SSM_CH + s * SCAN_STRIP, SSM_CH + (s + 1) * SCAN_STRIP)
        steps = []
        for k, d in enumerate((1, 2, 4)):
            rows = slice(k * SUBLANES, (k + 1) * SUBLANES)
            steps.append((d, pw_ref[rows, re], pw_ref[rows, im]))
        pr = pw_ref[3 * SUBLANES:4 * SUBLANES, re]
        pi = pw_ref[3 * SUBLANES:4 * SUBLANES, im]

        def group(gi, carry):
            cr, ci = carry
            r0 = pl.multiple_of(gi * SUBLANES, SUBLANES)
            xr = bu_ref[pl.ds(r0, SUBLANES), re]
            xi = bu_ref[pl.ds(r0, SUBLANES), im]
            for d, ar, ai in steps:
                sr = pltpu.roll(xr, d, 0)
                si = pltpu.roll(xi, d, 0)
                xr, xi = xr + ar * sr - ai * si, xi + ar * si + ai * sr
            xr, xi = xr + pr * cr - pi * ci, xi + pr * ci + pi * cr
            bu_ref[pl.ds(r0, SUBLANES), re] = xr
            bu_ref[pl.ds(r0, SUBLANES), im] = xi
            last_r = jnp.broadcast_to(xr[SUBLANES - 1:SUBLANES, :], (SUBLANES, SCAN_STRIP))
            last_i = jnp.broadcast_to(xi[SUBLANES - 1:SUBLANES, :], (SUBLANES, SCAN_STRIP))
            return last_r, last_i

        cr, ci = lax.fori_loop(0, tm // SUBLANES, group, (carry_ref[:, re], carry_ref[:, im]))
        carry_ref[:, re] = cr
        carry_ref[:, im] = ci

    ys = []
    for sg in range(SSM_SUPER):
        xr = bu_ref[:, sg * sgc:(sg + 1) * sgc].astype(BF16)
        xi = bu_ref[:, SSM_CH + sg * sgc:SSM_CH + (sg + 1) * sgc].astype(BF16)
        ys.append(jnp.dot(xr, bdc_ref[sg, 0], preferred_element_type=F32)
                  + jnp.dot(xi, bdc_ref[sg, 1], preferred_element_type=F32))
    y = jnp.concatenate(ys, axis=1) + dskip_ref[...] * u
    y = jax.nn.gelu(y)
    y = y * jax.nn.sigmoid(jnp.dot(y.astype(BF16), wglu_ref[...], preferred_element_type=F32))
    y_ref[...] = jnp.dot(y.astype(BF16), wout_ref[...], preferred_element_type=F32)


def _ssm_branch(u, bdb, bdc, pw, dskip, wglu, wout, *, bsz, lp):
    t = u.shape[0]
    tm = _pick_tile(lp, (256,))
    tps = lp // tm
    const = lambda shape: pl.BlockSpec(shape, lambda b, i: (0,) * len(shape))
    return pl.pallas_call(
        _ssm_kernel,
        grid=(bsz, tps),
        in_specs=[pl.BlockSpec((tm, SSM_WIDTH), lambda b, i: (b * tps + i, 0)),
                  const(bdb.shape), const(bdc.shape), const(pw.shape), const((1, SSM_WIDTH)),
                  const(wglu.shape), const(wout.shape)],
        out_specs=pl.BlockSpec((tm, D_MODEL), lambda b, i: (b * tps + i, 0)),
        out_shape=jax.ShapeDtypeStruct((t, D_MODEL), F32),
        scratch_shapes=[pltpu.VMEM((tm, 2 * SSM_CH), F32), pltpu.VMEM((SUBLANES, 2 * SSM_CH), F32)],
        compiler_params=_cparams(("arbitrary", "arbitrary")),
        name="ssm_branch",
    )(u, bdb, bdc, pw, dskip.reshape(1, SSM_WIDTH), wglu, wout)


def _ssm_tables(lam_re, lam_im, log_dt, b_re, b_im, c_re, c_im):
    lr, li = lam_re.astype(F32), lam_im.astype(F32)
    dt = jnp.exp(log_dt.astype(F32))[:, None]
    mag = jnp.exp(lr * dt)
    ar = mag * jnp.cos(li * dt)
    ai = mag * jnp.sin(li * dt)
    denom = lr * lr + li * li
    nr, ni = ar - 1.0, ai
    coef_r = (nr * lr + ni * li) / denom
    coef_i = (ni * lr - nr * li) / denom
    br, bi = b_re.astype(F32), b_im.astype(F32)
    bbar_r = coef_r[..., None] * br - coef_i[..., None] * bi
    bbar_i = coef_r[..., None] * bi + coef_i[..., None] * br
    gps = SSM_GROUPS // SSM_SUPER
    eye = jnp.eye(gps, dtype=F32)
    sgw, sgc = SSM_WIDTH // SSM_SUPER, SSM_CH // SSM_SUPER
    split = lambda m: m.reshape((SSM_SUPER, gps) + m.shape[1:])
    bd_in = lambda m: jnp.einsum('sgpc,gh->sgchp', split(m), eye).reshape(SSM_SUPER, sgw, sgc)
    bdb = jnp.concatenate([bd_in(bbar_r), bd_in(bbar_i)], axis=2)
    bd_out = lambda m: jnp.einsum('sgcp,gh->sgphc', split(m), eye).reshape(SSM_SUPER, sgc, sgw)
    bdc = jnp.stack([bd_out(c_re.astype(F32)), -bd_out(c_im.astype(F32))], axis=1)
    a1 = (ar.reshape(-1), ai.reshape(-1))
    cmul = lambda x, y: (x[0] * y[0] - x[1] * y[1], x[0] * y[1] + x[1] * y[0])
    pows = [a1]
    for _ in range(SUBLANES - 1):
        pows.append(cmul(pows[-1], a1))
    row = jnp.arange(SUBLANES)[:, None]
    both = lambda p: jnp.concatenate(p)[None, :]
    tables = [jnp.where(row >= d, both(pows[d - 1]), 0.0) for d in (1, 2, 4)]
    tables.append(jnp.concatenate([both(p) for p in pows], axis=0))
    pw = jnp.concatenate(tables, axis=0)
    return bdb.astype(BF16), bdc.astype(BF16), pw


def _conv_kernel(cb_ref, z_ref, zprev_ref, w_ref, wout_ref, y_ref):
    tm = z_ref.shape[0]
    z = z_ref[...]
    zz = jnp.concatenate([zprev_ref[...], z], axis=0)
    w = w_ref[...]
    y = w[2:3, :] * z
    for j in range(CONV_K - 1):
        shift = CONV_K - 1 - j
        y = y + w[j:j + 1, :] * zz[SUBLANES - shift:SUBLANES - shift + tm, :]
    y = cb_ref[...] * y
    y_ref[...] = jnp.dot(y.astype(BF16), wout_ref[...], preferred_element_type=F32)


def _conv_branch(cb, z, w, wout):
    t = z.shape[0]
    tm = _pick_tile(t, (512, 256))
    per = tm // SUBLANES
    return pl.pallas_call(
        _conv_kernel,
        grid=(t // tm,),
        in_specs=[pl.BlockSpec((tm, CONV_WIDTH), lambda i: (i, 0)),
                  pl.BlockSpec((tm, CONV_WIDTH), lambda i: (i, 0)),
                  pl.BlockSpec((SUBLANES, CONV_WIDTH), lambda i: (jnp.maximum(i * per - 1, 0), 0)),
                  pl.BlockSpec((SUBLANES, CONV_WIDTH), lambda i: (0, 0)),
                  pl.BlockSpec((CONV_WIDTH, D_MODEL), lambda i: (0, 0))],
        out_specs=pl.BlockSpec((tm, D_MODEL), lambda i: (i, 0)),
        out_shape=jax.ShapeDtypeStruct((t, D_MODEL), F32),
        compiler_params=_cparams(("parallel",)),
        name="conv_branch",
    )(cb, z, z, jnp.pad(w, ((0, SUBLANES - CONV_K), (0, 0))), wout)


ATT_SUB = 3
ATT_UNROLL = 4
BF16_SUBLANES = 16
ATT_VROWS = ATTN_VALUE_DIM + BF16_SUBLANES


def _attn_kernel(lam_ref, q_ref, k_ref, vt_ref, g_ref, o_ref,
                 qlo_ref, qhi_ref, m_ref, acc_ref, s_ref, *, front, lam_init):
    sb = pl.program_id(2)
    blk = ATT_BLOCK
    real0 = front + N_META
    nt = (((1,), (1,)), ((), ()))

    q = q_ref[...]
    lane = lax.broadcasted_iota(jnp.int32, q.shape, 1)
    qlo_ref[...] = jnp.where(lane < ATTN_HEAD_DIM, q, jnp.zeros_like(q))
    qhi_ref[...] = jnp.where(lane >= ATTN_HEAD_DIM, q, jnp.zeros_like(q))
    m_ref[...] = jnp.full_like(m_ref, NEG_INF)
    acc_ref[...] = jnp.zeros_like(acc_ref)

    def qk(a, k):
        rows = slice(a * blk, (a + 1) * blk)
        return [lax.dot_general(k, qref[rows, :], nt, preferred_element_type=F32) for qref in (qlo_ref, qhi_ref)]

    def consume(a, j, scores, vt, masked):
        if masked:
            pk = j * blk + lax.broadcasted_iota(jnp.int32, (blk, 1), 0)
            pq = (sb * ATT_SUB + a) * blk + lax.broadcasted_iota(jnp.int32, (1, blk), 1)
            cid_k = jnp.where(pk < front, 1 << 30, jnp.where(pk < real0, 0, 1 + (pk - real0) // CHUNK))
            cid_q = jnp.where(pq < real0, 0, 1 + (pq - real0) // CHUNK)
            vis = cid_k <= cid_q
            scores = [jnp.where(vis, s, NEG_INF) for s in scores]
        for idx, s in enumerate(scores):
            m_old = m_ref[a, idx]
            m_new = jnp.maximum(m_old, jnp.max(s, axis=0, keepdims=True))
            alpha = jnp.exp2(m_old - m_new)
            p = jnp.exp2(s - m_new)
            m_ref[a, idx] = m_new
            acc_ref[a, idx] = alpha * acc_ref[a, idx] + jnp.dot(vt, p.astype(BF16), preferred_element_type=F32)

    def k_block(j):
        return k_ref[pl.ds(pl.multiple_of(j * blk, blk), blk), :]

    def kv_block(j):
        return k_block(j), vt_ref[j]

    def qk_all(k, buf):
        for a in range(ATT_SUB):
            for idx, s in enumerate(qk(a, k)):
                s_ref[buf, a, idx] = s

    def consume_all(j, buf, vt, masked):
        for a in range(ATT_SUB):
            consume(a, j, [s_ref[buf, a, 0], s_ref[buf, a, 1]], vt, masked)

    @pl.when(sb > 0)
    def _():
        k, vt = kv_block(0)
        qk_all(k, 0)
        consume_all(0, 0, vt, True)

    n_full = jnp.maximum(sb * ATT_SUB - 1, 0)
    n_trips = n_full // ATT_UNROLL

    @pl.when(n_trips > 0)
    def _():
        qk_all(k_block(1), 0)

    def body(jj, carry):
        j = 1 + ATT_UNROLL * jj
        for u in range(ATT_UNROLL):
            qk_all(k_block(jnp.minimum(j + u + 1, n_full)), (u + 1) % 2)
            consume_all(j + u, u % 2, vt_ref[j + u], False)
        return carry

    lax.fori_loop(0, n_trips, body, 0)

    def leftover(j, carry):
        k, vt = kv_block(j)
        qk_all(k, 0)
        consume_all(j, 0, vt, False)
        return carry

    lax.fori_loop(1 + n_trips * ATT_UNROLL, 1 + n_full, leftover, 0)

    band = [(dj, a) for dj in range(ATT_SUB) for a in range(dj, ATT_SUB)]
    assert len(band) <= 2 * ATT_SUB
    for n_pair, (dj, a) in enumerate(band):
        for idx, s in enumerate(qk(a, k_block(sb * ATT_SUB + dj))):
            s_ref[n_pair // ATT_SUB, n_pair % ATT_SUB, idx] = s
    for n_pair, (dj, a) in enumerate(band):
        j = sb * ATT_SUB + dj
        slot = (n_pair // ATT_SUB, n_pair % ATT_SUB)
        consume(a, j, [s_ref[slot[0], slot[1], 0], s_ref[slot[0], slot[1], 1]], vt_ref[j], True)

    lam = lam_ref[0]
    vd = ATTN_VALUE_DIM
    for a in range(ATT_SUB):
        o = (acc_ref[a, 0, :vd] / acc_ref[a, 0, vd:vd + 1]
             - lam * (acc_ref[a, 1, :vd] / acc_ref[a, 1, vd:vd + 1]))
        o = o * lax.rsqrt(jnp.mean(o * o, axis=0, keepdims=True) + RMS_EPS)
        o = o * g_ref[...] * (1.0 - lam_init)
        o_ref[a * blk:(a + 1) * blk, :] = o.T.astype(BF16)


def _attention(lam, q, k, vt4, g_col, *, bsz, lp, front, lam_init):
    t = q.shape[0]
    blk = ATT_BLOCK
    nb = lp // blk
    assert nb % ATT_SUB == 0 and ATT_UNROLL % 2 == 0
    nsb = nb // ATT_SUB
    tq = ATT_SUB * blk
    return pl.pallas_call(
        functools.partial(_attn_kernel, front=front, lam_init=lam_init),
        grid=(bsz, ATTN_HEADS, nsb),
        in_specs=[pl.BlockSpec(memory_space=pltpu.SMEM),
                  pl.BlockSpec((tq, LANES), lambda b, h, i: (b * nsb + i, h)),
                  pl.BlockSpec((lp, LANES), lambda b, h, i: (b, h)),
                  pl.BlockSpec((nb, None, ATT_VROWS, blk), lambda b, h, i: (b, h, 0, 0)),
                  pl.BlockSpec((LANES, blk), lambda b, h, i: (0, 0))],
        out_specs=pl.BlockSpec((tq, LANES), lambda b, h, i: (b * nsb + i, h)),
        out_shape=jax.ShapeDtypeStruct((t, ATTN_WIDTH), BF16),
        scratch_shapes=[pltpu.VMEM((tq, LANES), BF16), pltpu.VMEM((tq, LANES), BF16),
                        pltpu.VMEM((ATT_SUB, 2, 1, blk), F32),
                        pltpu.VMEM((ATT_SUB, 2, ATT_VROWS, blk), F32),
                        pltpu.VMEM((2, ATT_SUB, 2, blk, blk), F32)],
        compiler_params=_cparams(("parallel", "parallel", "arbitrary")),
        name="diff_attention",
    )(lam, q, k, vt4, g_col)


def _merge_kernel(h_ref, g_ref, ys_ref, yc_ref, o_ref, wao_ref, wo_ref, lg_ref, lb_ref,
                  h1_ref, h1b_ref, *, tiles_per_seq, front):
    tm = h_ref.shape[0]
    ya = jnp.dot(o_ref[...], wao_ref[...], preferred_element_type=F32)
    g = g_ref[...].astype(F32)
    merged = g[:, :D_MODEL] * ys_ref[...] + g[:, D_MODEL:2 * D_MODEL] * yc_ref[...] + g[:, 2 * D_MODEL:] * ya
    r = DEEPNORM_ALPHA * h_ref[...] + jnp.dot(merged.astype(BF16), wo_ref[...], preferred_element_type=F32)
    y = _layer_norm(r, lg_ref[...], lb_ref[...])
    pos = _seq_pos(pl.program_id(0), tiles_per_seq, tm)
    y = jnp.where(pos >= front, y, 0.0)
    h1_ref[...] = y
    h1b_ref[...] = y.astype(BF16)


def _merge(h, g, ys, yc, o, wao, wo, lg, lb, *, lp, front):
    t, d = h.shape
    tm = _pick_tile(lp, (256,))
    row = lambda w: pl.BlockSpec((tm, w), lambda i: (i, 0))
    const = lambda a, b: pl.BlockSpec((a, b), lambda i: (0, 0))
    return pl.pallas_call(
        functools.partial(_merge_kernel, tiles_per_seq=lp // tm, front=front),
        grid=(t // tm,),
        in_specs=[row(d), row(3 * d), row(d), row(d), row(ATTN_WIDTH),
                  const(ATTN_WIDTH, d), const(d, d), const(1, d), const(1, d)],
        out_specs=[row(d), row(d)],
        out_shape=[jax.ShapeDtypeStruct((t, d), F32), jax.ShapeDtypeStruct((t, d), BF16)],
        compiler_params=_cparams(("parallel",)),
        name="merge_ln1",
    )(h, g, ys, yc, o, wao, wo, lg.reshape(1, d), lb.reshape(1, d))


def _router_kernel(h_ref, w_ref, b_ref, ids_ref, wts_ref, rank_ref, cnt_ref, base_ref):
    tm = h_ref.shape[0]

    @pl.when(pl.program_id(0) == 0)
    def _():
        base_ref[...] = jnp.zeros_like(base_ref)

    logits = jnp.dot(h_ref[...], w_ref[...], preferred_element_type=F32,
                     precision=lax.Precision.HIGHEST) + b_ref[...]
    lane = lax.broadcasted_iota(jnp.int32, logits.shape, 1).astype(F32)
    work = logits
    sels, vals, firsts = [], [], []
    for _ in range(TOP_K):
        mx = jnp.max(work, axis=1, keepdims=True)
        first = jnp.min(jnp.where(work == mx, lane, float(N_EXPERTS)), axis=1, keepdims=True)
        sel = lane == first
        sels.append(sel)
        vals.append(mx)
        firsts.append(first)
        work = jnp.where(sel, -jnp.inf, work)
    exps = [jnp.exp(v - vals[0]) for v in vals]
    den = exps[0] + exps[1] + exps[2] + exps[3]

    onehot = jnp.zeros_like(logits)
    for sel in sels:
        onehot = onehot + jnp.where(sel, 1.0, 0.0)
    r = lax.broadcasted_iota(jnp.int32, (tm, tm), 0)
    c = lax.broadcasted_iota(jnp.int32, (tm, tm), 1)
    tri = jnp.where(r > c, 1.0, 0.0).astype(BF16)
    before = jnp.dot(tri, onehot.astype(BF16), preferred_element_type=F32) + base_ref[...]
    base_ref[...] = base_ref[...] + jnp.sum(onehot, axis=0, keepdims=True)
    cnt_ref[...] = base_ref[...]

    out_lane = lax.broadcasted_iota(jnp.int32, (tm, LANES), 1)
    ids = jnp.zeros((tm, LANES), F32)
    wts = jnp.zeros((tm, LANES), F32)
    rank = jnp.zeros((tm, LANES), F32)
    for kk in range(TOP_K):
        rk = jnp.sum(jnp.where(sels[kk], before, 0.0), axis=1, keepdims=True)
        ids = jnp.where(out_lane == kk, firsts[kk], ids)
        wts = jnp.where(out_lane == kk, exps[kk] / den, wts)
        rank = jnp.where(out_lane == kk, rk, rank)
    ids_ref[...] = ids.astype(jnp.int32)
    wts_ref[...] = wts
    rank_ref[...] = rank.astype(jnp.int32)


def _router(h, w, b):
    t, d = h.shape
    tm = _pick_tile(t, (512, 256))
    wide = lambda dt: jax.ShapeDtypeStruct((t, LANES), dt)
    row = pl.BlockSpec((tm, LANES), lambda i: (i, 0))
    return pl.pallas_call(
        _router_kernel,
        grid=(t // tm,),
        in_specs=[pl.BlockSpec((tm, d), lambda i: (i, 0)),
                  pl.BlockSpec((d, N_EXPERTS), lambda i: (0, 0)),
                  pl.BlockSpec((1, N_EXPERTS), lambda i: (0, 0))],
        out_specs=[row, row, row, pl.BlockSpec((1, N_EXPERTS), lambda i: (0, 0))],
        out_shape=[wide(jnp.int32), wide(F32), wide(jnp.int32), jax.ShapeDtypeStruct((1, N_EXPERTS), F32)],
        scratch_shapes=[pltpu.VMEM((1, N_EXPERTS), F32)],
        compiler_params=_cparams(("arbitrary",)),
        name="moe_router",
    )(h, w, b.reshape(1, N_EXPERTS))


MOE_TILE = 256
ROUTE_TOK = 256


def _dispatch_kernel(last_ref, nu_ref, dest_ref, h_ref, xs_ref, zero_ref, stage_ref, zsem, sems):
    i = pl.program_id(0)
    n = pl.num_programs(0)
    n_tiles = xs_ref.shape[0] // MOE_TILE

    def fill(tile):
        return pltpu.make_async_copy(zero_ref, xs_ref.at[pl.ds(tile * MOE_TILE, MOE_TILE)], zsem)

    @pl.when(i == 0)
    def _():
        zero_ref[...] = jnp.zeros_like(zero_ref)
        for e in range(N_EXPERTS):
            @pl.when(last_ref[e] >= 0)
            def _():
                fill(last_ref[e]).start()

        def start_unused(tile, carry):
            fill(tile).start()
            return carry

        def wait_unused(tile, carry):
            fill(tile).wait()
            return carry

        lax.fori_loop(nu_ref[0], n_tiles, start_unused, 0)
        for e in range(N_EXPERTS):
            @pl.when(last_ref[e] >= 0)
            def _():
                fill(last_ref[e]).wait()
        lax.fori_loop(nu_ref[0], n_tiles, wait_unused, 0)

    slot = lax.rem(i, 2)
    stage_ref[slot] = h_ref[...]

    def issue(r8, carry):
        base = pl.multiple_of(r8 * SUBLANES, SUBLANES)
        for rr in range(SUBLANES):
            for kk in range(TOP_K):
                d = dest_ref[(base + rr) * TOP_K + kk]
                pltpu.make_async_copy(stage_ref.at[slot, pl.ds(base + rr, 1)], xs_ref.at[pl.ds(d, 1)],
                                      sems.at[slot]).start(priority=kk % 2)
        return carry

    lax.fori_loop(0, ROUTE_TOK // SUBLANES, issue, 0)

    def drain(s):
        for _ in range(TOP_K):
            pltpu.make_async_copy(stage_ref.at[s], xs_ref.at[pl.ds(0, ROUTE_TOK)], sems.at[s]).wait()

    @pl.when(i > 0)
    def _():
        drain(1 - slot)

    @pl.when(i == n - 1)
    def _():
        drain(slot)


def _dispatch(h, dest_flat, last_tile, n_used, n_rows):
    t, d = h.shape
    return pl.pallas_call(
        _dispatch_kernel,
        grid_spec=pltpu.PrefetchScalarGridSpec(
            num_scalar_prefetch=2,
            grid=(t // ROUTE_TOK,),
            in_specs=[pl.BlockSpec((ROUTE_TOK * TOP_K,), lambda i, last, nu: (i,), memory_space=pltpu.SMEM),
                      pl.BlockSpec((ROUTE_TOK, d), lambda i, last, nu: (i, 0))],
            out_specs=pl.BlockSpec(memory_space=pl.ANY),
            scratch_shapes=[pltpu.VMEM((MOE_TILE, d), F32), pltpu.VMEM((2, ROUTE_TOK, d), F32),
                            pltpu.SemaphoreType.DMA(()), pltpu.SemaphoreType.DMA((2,))]),
        out_shape=jax.ShapeDtypeStruct((n_rows, d), F32),
        compiler_params=_cparams(("arbitrary",)),
        name="moe_dispatch",
    )(last_tile, n_used, dest_flat, h)


def _experts_kernel(te_ref, nu_ref, xs_ref, wgu_ref, bgu_ref, wd_ref, bd_ref, y_ref, wgu_b_ref, wd_b_ref):
    i = pl.program_id(0)

    @pl.when(jnp.logical_or(i == 0, te_ref[i] != te_ref[jnp.maximum(i - 1, 0)]))
    def _():
        wgu_b_ref[...] = wgu_ref[0].astype(BF16)
        wd_b_ref[...] = wd_ref[0].astype(BF16)

    @pl.when(i >= nu_ref[0])
    def _():
        y_ref[...] = jnp.zeros_like(y_ref)

    @pl.when(i < nu_ref[0])
    def _():
        gu = jnp.dot(xs_ref[...].astype(BF16), wgu_b_ref[...], preferred_element_type=F32) + bgu_ref[0]
        gate = jnp.minimum(gu[:, :D_FF], SWIGLU_LIMIT)
        up = jnp.clip(gu[:, D_FF:], -SWIGLU_LIMIT, SWIGLU_LIMIT)
        hid = (up + 1.0) * gate * jax.nn.sigmoid(SWIGLU_ALPHA * gate)
        y_ref[...] = jnp.dot(hid.astype(BF16), wd_b_ref[...], preferred_element_type=F32) + bd_ref[0]


def _experts(tile_expert, n_used, xs, wgu, bgu, wd, bd, *, layer):
    n_rows, d = xs.shape
    n_exp = wgu.shape[1]
    rows = lambda i, te, nu: (jnp.minimum(i, nu[0] - 1), 0)
    per_e = lambda i, te, nu: (te[i], 0, 0)
    per_le = lambda i, te, nu: (layer, te[i], 0, 0)
    return pl.pallas_call(
        _experts_kernel,
        grid_spec=pltpu.PrefetchScalarGridSpec(
            num_scalar_prefetch=2,
            grid=(n_rows // MOE_TILE,),
            in_specs=[pl.BlockSpec((MOE_TILE, d), rows),
                      pl.BlockSpec((None, 1, d, 2 * D_FF), per_le),
                      pl.BlockSpec((1, 1, 2 * D_FF), per_e),
                      pl.BlockSpec((None, 1, D_FF, d), per_le),
                      pl.BlockSpec((1, 1, d), per_e)],
            out_specs=pl.BlockSpec((MOE_TILE, d), lambda i, te, nu: (i, 0)),
            scratch_shapes=[pltpu.VMEM((d, 2 * D_FF), BF16), pltpu.VMEM((D_FF, d), BF16)]),
        out_shape=jax.ShapeDtypeStruct((n_rows, d), F32),
        compiler_params=_cparams(("arbitrary",)),
        name="moe_experts",
    )(tile_expert, n_used, xs, wgu, bgu.reshape(n_exp, 1, 2 * D_FF), wd, bd.reshape(n_exp, 1, d))


COMBINE_TOK = 128


def _combine_kernel(dcur_ref, dnext_ref, w_ref, h_ref, y_ref, lg_ref, lb_ref, h2_ref, h2b_ref,
                    ybuf_ref, sems, *, tiles_per_seq, front):
    i = pl.program_id(0)
    n = pl.num_programs(0)
    slot = lax.rem(i, 2)

    def issue(d_ref, s):
        def body(r8, carry):
            base = pl.multiple_of(r8 * SUBLANES, SUBLANES)
            for rr in range(SUBLANES):
                for kk in range(TOP_K):
                    d = d_ref[(base + rr) * TOP_K + kk]
                    pltpu.make_async_copy(y_ref.at[pl.ds(d, 1)], ybuf_ref.at[s, kk, pl.ds(base + rr, 1)],
                                          sems.at[s]).start(priority=kk % 2)
            return carry
        lax.fori_loop(0, COMBINE_TOK // SUBLANES, body, 0)

    @pl.when(i == 0)
    def _():
        issue(dcur_ref, 0)

    @pl.when(i + 1 < n)
    def _():
        issue(dnext_ref, 1 - slot)

    for kk in range(TOP_K):
        pltpu.make_async_copy(y_ref.at[pl.ds(0, COMBINE_TOK)], ybuf_ref.at[slot, kk], sems.at[slot]).wait()

    w = w_ref[...]
    acc = w[:, 0:1] * ybuf_ref[slot, 0]
    for kk in range(1, TOP_K):
        acc = acc + w[:, kk:kk + 1] * ybuf_ref[slot, kk]
    y = _layer_norm(DEEPNORM_ALPHA * h_ref[...] + acc, lg_ref[...], lb_ref[...])
    pos = _seq_pos(i, tiles_per_seq, COMBINE_TOK)
    y = jnp.where(pos >= front, y, 0.0)
    h2_ref[...] = y
    h2b_ref[...] = y.astype(BF16)


def _combine(dest_flat, wts, h, y, lg, lb, *, lp, front, drop_prefix=False):
    t, d = h.shape
    n = t // COMBINE_TOK
    blk = COMBINE_TOK * TOP_K
    row = lambda w: pl.BlockSpec((COMBINE_TOK, w), lambda i: (i, 0))
    const = pl.BlockSpec((1, d), lambda i: (0, 0))
    out_rows, out_f32 = t, row(d)
    if drop_prefix:
        tps = lp // COMBINE_TOK
        skip = (front + N_META) // COMBINE_TOK
        out_rows = t - (t // lp) * (front + N_META)
        out_f32 = pl.BlockSpec((COMBINE_TOK, d),
                               lambda i: ((i // tps) * (tps - skip) + jnp.maximum(i % tps - skip, 0), 0))
    return pl.pallas_call(
        functools.partial(_combine_kernel, tiles_per_seq=lp // COMBINE_TOK, front=front),
        grid=(n,),
        in_specs=[pl.BlockSpec((blk,), lambda i: (i,), memory_space=pltpu.SMEM),
                  pl.BlockSpec((blk,), lambda i: (jnp.minimum(i + 1, n - 1),), memory_space=pltpu.SMEM),
                  row(LANES), row(d), pl.BlockSpec(memory_space=pl.ANY), const, const],
        out_specs=[out_f32, row(d)],
        out_shape=[jax.ShapeDtypeStruct((out_rows, d), F32), jax.ShapeDtypeStruct((t, d), BF16)],
        scratch_shapes=[pltpu.VMEM((2, TOP_K, COMBINE_TOK, d), F32), pltpu.SemaphoreType.DMA((2,))],
        compiler_params=_cparams(("arbitrary",)),
        name="moe_combine_ln2",
    )(dest_flat, dest_flat, wts, h, y, lg.reshape(1, d), lb.reshape(1, d))


def _routing_plan(ids, rank, counts, n_tiles):
    n_exp = counts.shape[0]
    padded = (counts + MOE_TILE - 1) // MOE_TILE * MOE_TILE
    ends = jnp.cumsum(padded)
    starts = ends - padded
    dest = (jnp.take(starts, ids) + rank).reshape(-1)
    n_used = (ends[-1] // MOE_TILE).reshape(1)
    tile_start = jnp.arange(n_tiles, dtype=jnp.int32) * MOE_TILE
    tile_expert = jnp.sum((tile_start[:, None] >= ends[None, :]).astype(jnp.int32), axis=1)
    last_used_expert = jnp.max(jnp.where(padded > 0, jnp.arange(n_exp, dtype=jnp.int32), 0))
    tile_expert = jnp.minimum(tile_expert, last_used_expert)
    last_tile = jnp.where(padded > 0, ends // MOE_TILE - 1, -1)
    return dest.astype(jnp.int32), tile_expert.astype(jnp.int32), n_used.astype(jnp.int32), last_tile.astype(jnp.int32)


def _rope_tables(lp, front):
    pos = (jnp.arange(lp, dtype=jnp.int32) - front).astype(F32)
    inv = ROPE_THETA ** (-jnp.arange(0, ATTN_HEAD_DIM, 2, dtype=F32) / ATTN_HEAD_DIM)
    ang = pos[:, None] * inv[None, :]
    ang = jnp.concatenate([ang, ang, ang, ang], axis=-1)
    return jnp.cos(ang), jnp.sin(ang)


def kernel(x, meta_tokens, ln_in_g, ln_in_b, w_in, ssm_lambda_re, ssm_lambda_im, ssm_log_dt, ssm_b_re, ssm_b_im, ssm_c_re, ssm_c_im, ssm_d, ssm_w_glu, ssm_w_out, conv_w, conv_w_out, attn_lambda_q1, attn_lambda_k1, attn_lambda_q2, attn_lambda_k2, attn_subln_g, attn_w_out, gate_w, gate_b, w_o, ln1_g, ln1_b, router_w, router_b, expert_w_gu, expert_b_gu, expert_w_down, expert_b_down, ln2_g, ln2_b):
    bsz, seq, d = x.shape
    assert d == D_MODEL and seq % ATT_BLOCK == 0
    lp = seq + ATT_BLOCK
    front = ATT_BLOCK - N_META
    t = bsz * lp

    meta = jnp.broadcast_to(meta_tokens.astype(x.dtype)[None], (bsz, N_META, d))
    hcat = jnp.concatenate([jnp.zeros((bsz, front, d), x.dtype), meta, x], axis=1).reshape(t, d)
    h, hb = _ln_in(hcat, ln_in_g, ln_in_b, lp=lp, front=front)
    cos, sin = _rope_tables(lp, front)

    s3 = SSM_WIDTH + 3 * CONV_WIDTH
    s5 = s3 + 2 * ATTN_WIDTH
    for l in range(DEPTH):
        w_in_b = w_in[l].astype(BF16)
        u, cb, z = _proj_sc(hb, w_in_b[:, :s3])
        q, k = _proj_qk(hb, w_in_b[:, s3:s5], cos, sin, lp=lp)
        vt = _proj_vt(hb, w_in_b[:, s5:].T)
        g = _proj_gate(hb, gate_w[l].astype(BF16), gate_b[l])

        bdb, bdc, pw = _ssm_tables(ssm_lambda_re[l], ssm_lambda_im[l], ssm_log_dt[l], ssm_b_re[l], ssm_b_im[l],
                                   ssm_c_re[l], ssm_c_im[l])
        ys = _ssm_branch(u, bdb, bdc, pw, ssm_d[l].astype(F32), ssm_w_glu[l].astype(BF16),
                         ssm_w_out[l].astype(BF16), bsz=bsz, lp=lp)
        yc = _conv_branch(cb, z, conv_w[l].astype(F32), conv_w_out[l].astype(BF16))

        lam_init = 0.8 - 0.6 * math.exp(-0.3 * l)
        lam = (jnp.exp(jnp.sum(attn_lambda_q1[l].astype(F32) * attn_lambda_k1[l].astype(F32)))
               - jnp.exp(jnp.sum(attn_lambda_q2[l].astype(F32) * attn_lambda_k2[l].astype(F32)))
               + lam_init).reshape(1)
        g_col = jnp.broadcast_to(attn_subln_g[l].astype(F32)[:, None], (ATTN_VALUE_DIM, ATT_BLOCK))
        o = _attention(lam, q, k, vt, g_col, bsz=bsz, lp=lp, front=front, lam_init=lam_init)

        h, hb = _merge(h, g, ys, yc, o, attn_w_out[l].astype(BF16), w_o[l].astype(BF16), ln1_g[l], ln1_b[l],
                       lp=lp, front=front)

        ids, wts, rank, counts = _router(h, router_w[l].astype(F32), router_b[l].astype(F32))
        n_tiles = t * TOP_K // MOE_TILE + N_EXPERTS
        dest, tile_expert, n_used, last_tile = _routing_plan(ids[:, :TOP_K], rank[:, :TOP_K],
                                                             counts[0].astype(jnp.int32), n_tiles)
        xs = _dispatch(h, dest, last_tile, n_used, n_tiles * MOE_TILE)
        y = _experts(tile_expert, n_used, xs, expert_w_gu.astype(F32), expert_b_gu[l].astype(F32),
                     expert_w_down.astype(F32), expert_b_down[l].astype(F32), layer=l)
        h, hb = _combine(dest, wts, h, y, ln2_g[l], ln2_b[l], lp=lp, front=front, drop_prefix=(l == DEPTH - 1))

    return h.reshape(bsz, seq, d)
```

```python
import functools
import math

import jax
import jax.numpy as jnp
from jax import lax
from jax.experimental import pallas as pl
from jax.experimental.pallas import tpu as pltpu

F32 = jnp.float32
BF16 = jnp.bfloat16

D_MODEL = 1024
DEPTH = 2
CHUNK = 64
N_META = 16
SSM_WIDTH = 512
SSM_GROUP = 16
SSM_GROUPS = 32
SSM_STATE = 64
SSM_CH = SSM_GROUPS * SSM_STATE
CONV_WIDTH = 512
CONV_K = 3
ATTN_HEADS = 8
ATTN_HEAD_DIM = 64
ATTN_VALUE_DIM = 128
ATTN_WIDTH = 1024
ROPE_THETA = 10000.0
N_EXPERTS = 32
TOP_K = 4
D_FF = 1024
SWIGLU_LIMIT = 7.0
SWIGLU_ALPHA = 1.702
DEEPNORM_ALPHA = (2.0 * DEPTH) ** 0.25
LN_EPS = 1e-5
RMS_EPS = 1e-5
NEG_INF = -1e30

ATT_BLOCK = 256
LANES = 128
SUBLANES = 8
VMEM_LIMIT = 48 * 1024 * 1024


def _cparams(sem):
    return pltpu.CompilerParams(dimension_semantics=sem, vmem_limit_bytes=VMEM_LIMIT)


def _pick_tile(n, candidates):
    for c in candidates:
        if n % c == 0:
            return c
    raise ValueError(f"no tile for {n}")


def _layer_norm(x, g, b):
    mu = jnp.mean(x, axis=-1, keepdims=True)
    xc = x - mu
    var = jnp.mean(xc * xc, axis=-1, keepdims=True)
    return xc * lax.rsqrt(var + LN_EPS) * g + b


def _seq_pos(tile_idx, tiles_per_seq, tm):
    base = lax.rem(tile_idx, tiles_per_seq) * tm
    return base + lax.broadcasted_iota(jnp.int32, (tm, 1), 0)


def _ln_in_kernel(x_ref, g_ref, b_ref, h_ref, hb_ref, *, tiles_per_seq, front):
    tm = x_ref.shape[0]
    y = _layer_norm(x_ref[...], g_ref[...], b_ref[...])
    pos = _seq_pos(pl.program_id(0), tiles_per_seq, tm)
    y = jnp.where(pos >= front, y, 0.0)
    h_ref[...] = y
    hb_ref[...] = y.astype(BF16)


def _ln_in(hcat, g, b, *, lp, front):
    t, d = hcat.shape
    tm = _pick_tile(lp, (512, 256))
    return pl.pallas_call(
        functools.partial(_ln_in_kernel, tiles_per_seq=lp // tm, front=front),
        grid=(t // tm,),
        in_specs=[pl.BlockSpec((tm, d), lambda i: (i, 0)),
                  pl.BlockSpec((1, d), lambda i: (0, 0)),
                  pl.BlockSpec((1, d), lambda i: (0, 0))],
        out_specs=[pl.BlockSpec((tm, d), lambda i: (i, 0)),
                   pl.BlockSpec((tm, d), lambda i: (i, 0))],
        out_shape=[jax.ShapeDtypeStruct((t, d), F32), jax.ShapeDtypeStruct((t, d), BF16)],
        compiler_params=_cparams(("parallel",)),
        name="ln_in",
    )(hcat, g.reshape(1, d), b.reshape(1, d))


def _proj_sc_kernel(x_ref, w_ref, u_ref, cb_ref, z_ref):
    acc = jnp.dot(x_ref[...], w_ref[...], preferred_element_type=F32)
    s0, s1, s2, s3 = SSM_WIDTH, SSM_WIDTH + CONV_WIDTH, SSM_WIDTH + 2 * CONV_WIDTH, SSM_WIDTH + 3 * CONV_WIDTH
    u_ref[...] = acc[:, :s0]
    cb_ref[...] = acc[:, s0:s1]
    z_ref[...] = acc[:, s1:s2] * acc[:, s2:s3]


def _proj_sc(hb, w):
    t, d = hb.shape
    n = w.shape[1]
    tm = _pick_tile(t, (704, 512, 256))
    out = jax.ShapeDtypeStruct((t, SSM_WIDTH), F32)
    ospec = pl.BlockSpec((tm, SSM_WIDTH), lambda i: (i, 0))
    return pl.pallas_call(
        _proj_sc_kernel,
        grid=(t // tm,),
        in_specs=[pl.BlockSpec((tm, d), lambda i: (i, 0)),
                  pl.BlockSpec((d, n), lambda i: (0, 0))],
        out_specs=[ospec, ospec, ospec],
        out_shape=[out, out, out],
        compiler_params=_cparams(("parallel",)),
        name="proj_ssm_conv",
    )(hb, w)


def _proj_qk_kernel(x_ref, w_ref, cos_ref, sin_ref, q_ref, k_ref):
    acc = jnp.dot(x_ref[...], w_ref[...], preferred_element_type=F32)
    cos = cos_ref[...]
    sin = sin_ref[...]
    lane = lax.broadcasted_iota(jnp.int32, (1, LANES), 1)
    low_half = lax.rem(lane, ATTN_HEAD_DIM) < ATTN_HEAD_DIM // 2
    nblk = ATTN_WIDTH // LANES
    for blk in range(2 * nblk):
        a = acc[:, blk * LANES:(blk + 1) * LANES]
        rot = jnp.where(low_half, -pltpu.roll(a, LANES - ATTN_HEAD_DIM // 2, 1),
                        pltpu.roll(a, ATTN_HEAD_DIM // 2, 1))
        r = a * cos + rot * sin
        if blk < nblk:
            q_ref[:, blk * LANES:(blk + 1) * LANES] = (r * (ATTN_HEAD_DIM ** -0.5 * math.log2(math.e))).astype(BF16)
        else:
            k_ref[:, (blk - nblk) * LANES:(blk - nblk + 1) * LANES] = r.astype(BF16)


def _proj_qk(hb, w, cos, sin, *, lp):
    t, d = hb.shape
    tm = _pick_tile(lp, (704, 512, 256))
    tps = lp // tm
    out = jax.ShapeDtypeStruct((t, ATTN_WIDTH), BF16)
    return pl.pallas_call(
        _proj_qk_kernel,
        grid=(t // tm,),
        in_specs=[pl.BlockSpec((tm, d), lambda i: (i, 0)),
                  pl.BlockSpec((d, 2 * ATTN_WIDTH), lambda i: (0, 0)),
                  pl.BlockSpec((tm, LANES), lambda i: (i % tps, 0)),
                  pl.BlockSpec((tm, LANES), lambda i: (i % tps, 0))],
        out_specs=[pl.BlockSpec((tm, ATTN_WIDTH), lambda i: (i, 0)),
                   pl.BlockSpec((tm, ATTN_WIDTH), lambda i: (i, 0))],
        out_shape=[out, out],
        compiler_params=_cparams(("parallel",)),
        name="proj_qk_rope",
    )(hb, w, cos, sin)


def _proj_vt_kernel(wt_ref, x_ref, o_ref):
    vt = lax.dot_general(wt_ref[...], x_ref[...], (((1,), (1,)), ((), ())),
                         preferred_element_type=F32).astype(BF16)
    vd = ATTN_VALUE_DIM
    row = lax.broadcasted_iota(jnp.int32, (ATT_VROWS - vd, ATT_BLOCK), 0)
    tail = jnp.where(row == 0, 1.0, 0.0).astype(BF16)
    for c in range(o_ref.shape[0]):
        for h in range(ATTN_HEADS):
            o_ref[c, h, :vd, :] = vt[h * vd:(h + 1) * vd, c * ATT_BLOCK:(c + 1) * ATT_BLOCK]
            o_ref[c, h, vd:, :] = tail


def _proj_vt(hb, wt):
    t, d = hb.shape
    tm = _pick_tile(t, (512, 256))
    per = tm // ATT_BLOCK
    return pl.pallas_call(
        _proj_vt_kernel,
        grid=(t // tm,),
        in_specs=[pl.BlockSpec((ATTN_WIDTH, d), lambda i: (0, 0)),
                  pl.BlockSpec((tm, d), lambda i: (i, 0))],
        out_specs=pl.BlockSpec((per, ATTN_HEADS, ATT_VROWS, ATT_BLOCK), lambda i: (i, 0, 0, 0)),
        out_shape=jax.ShapeDtypeStruct((t // ATT_BLOCK, ATTN_HEADS, ATT_VROWS, ATT_BLOCK), BF16),
        compiler_params=_cparams(("parallel",)),
        name="proj_v_t",
    )(wt, hb)


def _proj_gate_kernel(x_ref, w_ref, b_ref, o_ref):
    acc = jnp.dot(x_ref[...], w_ref[...], preferred_element_type=F32) + b_ref[...]
    o_ref[...] = jax.nn.sigmoid(acc).astype(BF16)


def _proj_gate(hb, w, b):
    t, d = hb.shape
    n = w.shape[1]
    tm = _pick_tile(t, (1408, 512, 256))
    tn = 1024
    return pl.pallas_call(
        _proj_gate_kernel,
        grid=(t // tm, n // tn),
        in_specs=[pl.BlockSpec((tm, d), lambda i, j: (i, 0)),
                  pl.BlockSpec((d, tn), lambda i, j: (0, j)),
                  pl.BlockSpec((1, tn), lambda i, j: (0, j))],
        out_specs=pl.BlockSpec((tm, tn), lambda i, j: (i, j)),
        out_shape=jax.ShapeDtypeStruct((t, n), BF16),
        compiler_params=_cparams(("parallel", "parallel")),
        name="proj_gate",
    )(hb, w, b.reshape(1, n))


SCAN_STRIP = 512
SSM_SUPER = 2


def _ssm_kernel(u_ref, bdb_ref, bdc_ref, pw_ref, dskip_ref, wglu_ref, wout_ref, y_ref, bu_ref, carry_ref):
    tm = u_ref.shape[0]

    @pl.when(pl.program_id(1) == 0)
    def _():
        carry_ref[...] = jnp.zeros_like(carry_ref)

    u = u_ref[...]
    ub = u.astype(BF16)
    sgw = SSM_WIDTH // SSM_SUPER
    sgc = SSM_CH // SSM_SUPER
    for sg in range(SSM_SUPER):
        bu = jnp.dot(ub[:, sg * sgw:(sg + 1) * sgw], bdb_ref[sg], preferred_element_type=F32)
        bu_ref[:, sg * sgc:(sg + 1) * sgc] = bu[:, :sgc]
        bu_ref[:, SSM_CH + sg * sgc:SSM_CH + (sg + 1) * sgc] = bu[:, sgc:]

    for s in range(SSM_CH // SCAN_STRIP):
        re = slice(s * SCAN_STRIP, (s + 1) * SCAN_STRIP)
        im = slice(SSM_CH + s * SCAN_STRIP, SSM_CH + (s + 1) * SCAN_STRIP)
        steps = []
        for k, d in enumerate((1, 2, 4)):
            rows = slice(k * SUBLANES, (k + 1) * SUBLANES)
            steps.append((d, pw_ref[rows, re], pw_ref[rows, im]))
        pr = pw_ref[3 * SUBLANES:4 * SUBLANES, re]
        pi = pw_ref[3 * SUBLANES:4 * SUBLANES, im]

        def group(gi, carry):
            cr, ci = carry
            r0 = pl.multiple_of(gi * SUBLANES, SUBLANES)
            xr = bu_ref[pl.ds(r0, SUBLANES), re]
            xi = bu_ref[pl.ds(r0, SUBLANES), im]
            for d, ar, ai in steps:
                sr = pltpu.roll(xr, d, 0)
                si = pltpu.roll(xi, d, 0)
                xr, xi = xr + ar * sr - ai * si, xi + ar * si + ai * sr
            xr, xi = xr + pr * cr - pi * ci, xi + pr * ci + pi * cr
            bu_ref[pl.ds(r0, SUBLANES), re] = xr
            bu_ref[pl.ds(r0, SUBLANES), im] = xi
            last_r = jnp.broadcast_to(xr[SUBLANES - 1:SUBLANES, :], (SUBLANES, SCAN_STRIP))
            last_i = jnp.broadcast_to(xi[SUBLANES - 1:SUBLANES, :], (SUBLANES, SCAN_STRIP))
            return last_r, last_i

        cr, ci = lax.fori_loop(0, tm // SUBLANES, group, (carry_ref[:, re], carry_ref[:, im]))
        carry_ref[:, re] = cr
        carry_ref[:, im] = ci

    ys = []
    for sg in range(SSM_SUPER):
        xr = bu_ref[:, sg * sgc:(sg + 1) * sgc].astype(BF16)
        xi = bu_ref[:, SSM_CH + sg * sgc:SSM_CH + (sg + 1) * sgc].astype(BF16)
        ys.append(jnp.dot(xr, bdc_ref[sg, 0], preferred_element_type=F32)
                  + jnp.dot(xi, bdc_ref[sg, 1], preferred_element_type=F32))
    y = jnp.concatenate(ys, axis=1) + dskip_ref[...] * u
    y = jax.nn.gelu(y)
    y = y * jax.nn.sigmoid(jnp.dot(y.astype(BF16), wglu_ref[...], preferred_element_type=F32))
    y_ref[...] = jnp.dot(y.astype(BF16), wout_ref[...], preferred_element_type=F32).astype(BF16)


def _ssm_branch(u, bdb, bdc, pw, dskip, wglu, wout, *, bsz, lp):
    t = u.shape[0]
    tm = _pick_tile(lp, (256,))
    tps = lp // tm
    const = lambda shape: pl.BlockSpec(shape, lambda b, i: (0,) * len(shape))
    return pl.pallas_call(
        _ssm_kernel,
        grid=(bsz, tps),
        in_specs=[pl.BlockSpec((tm, SSM_WIDTH), lambda b, i: (b * tps + i, 0)),
                  const(bdb.shape), const(bdc.shape), const(pw.shape), const((1, SSM_WIDTH)),
                  const(wglu.shape), const(wout.shape)],
        out_specs=pl.BlockSpec((tm, D_MODEL), lambda b, i: (b * tps + i, 0)),
        out_shape=jax.ShapeDtypeStruct((t, D_MODEL), BF16),
        scratch_shapes=[pltpu.VMEM((tm, 2 * SSM_CH), F32), pltpu.VMEM((SUBLANES, 2 * SSM_CH), F32)],
        compiler_params=_cparams(("arbitrary", "arbitrary")),
        name="ssm_branch",
    )(u, bdb, bdc, pw, dskip.reshape(1, SSM_WIDTH), wglu, wout)


def _ssm_tables(lam_re, lam_im, log_dt, b_re, b_im, c_re, c_im):
    lr, li = lam_re.astype(F32), lam_im.astype(F32)
    dt = jnp.exp(log_dt.astype(F32))[:, None]
    mag = jnp.exp(lr * dt)
    ar = mag * jnp.cos(li * dt)
    ai = mag * jnp.sin(li * dt)
    denom = lr * lr + li * li
    nr, ni = ar - 1.0, ai
    coef_r = (nr * lr + ni * li) / denom
    coef_i = (ni * lr - nr * li) / denom
    br, bi = b_re.astype(F32), b_im.astype(F32)
    bbar_r = coef_r[..., None] * br - coef_i[..., None] * bi
    bbar_i = coef_r[..., None] * bi + coef_i[..., None] * br
    gps = SSM_GROUPS // SSM_SUPER
    eye = jnp.eye(gps, dtype=F32)
    sgw, sgc = SSM_WIDTH // SSM_SUPER, SSM_CH // SSM_SUPER
    split = lambda m: m.reshape((SSM_SUPER, gps) + m.shape[1:])
    bd_in = lambda m: jnp.einsum('sgpc,gh->sgchp', split(m), eye).reshape(SSM_SUPER, sgw, sgc)
    bdb = jnp.concatenate([bd_in(bbar_r), bd_in(bbar_i)], axis=2)
    bd_out = lambda m: jnp.einsum('sgcp,gh->sgphc', split(m), eye).reshape(SSM_SUPER, sgc, sgw)
    bdc = jnp.stack([bd_out(c_re.astype(F32)), -bd_out(c_im.astype(F32))], axis=1)
    a1 = (ar.reshape(-1), ai.reshape(-1))
    cmul = lambda x, y: (x[0] * y[0] - x[1] * y[1], x[0] * y[1] + x[1] * y[0])
    pows = [a1]
    for _ in range(SUBLANES - 1):
        pows.append(cmul(pows[-1], a1))
    row = jnp.arange(SUBLANES)[:, None]
    both = lambda p: jnp.concatenate(p)[None, :]
    tables = [jnp.where(row >= d, both(pows[d - 1]), 0.0) for d in (1, 2, 4)]
    tables.append(jnp.concatenate([both(p) for p in pows], axis=0))
    pw = jnp.concatenate(tables, axis=0)
    return bdb.astype(BF16), bdc.astype(BF16), pw


def _conv_kernel(cb_ref, z_ref, zprev_ref, w_ref, wout_ref, y_ref):
    tm = z_ref.shape[0]
    z = z_ref[...]
    zz = jnp.concatenate([zprev_ref[...], z], axis=0)
    w = w_ref[...]
    y = w[2:3, :] * z
    for j in range(CONV_K - 1):
        shift = CONV_K - 1 - j
        y = y + w[j:j + 1, :] * zz[SUBLANES - shift:SUBLANES - shift + tm, :]
    y = cb_ref[...] * y
    y_ref[...] = jnp.dot(y.astype(BF16), wout_ref[...], preferred_element_type=F32).astype(BF16)


def _conv_branch(cb, z, w, wout):
    t = z.shape[0]
    tm = _pick_tile(t, (512, 256))
    per = tm // SUBLANES
    return pl.pallas_call(
        _conv_kernel,
        grid=(t // tm,),
        in_specs=[pl.BlockSpec((tm, CONV_WIDTH), lambda i: (i, 0)),
                  pl.BlockSpec((tm, CONV_WIDTH), lambda i: (i, 0)),
                  pl.BlockSpec((SUBLANES, CONV_WIDTH), lambda i: (jnp.maximum(i * per - 1, 0), 0)),
                  pl.BlockSpec((SUBLANES, CONV_WIDTH), lambda i: (0, 0)),
                  pl.BlockSpec((CONV_WIDTH, D_MODEL), lambda i: (0, 0))],
        out_specs=pl.BlockSpec((tm, D_MODEL), lambda i: (i, 0)),
        out_shape=jax.ShapeDtypeStruct((t, D_MODEL), BF16),
        compiler_params=_cparams(("parallel",)),
        name="conv_branch",
    )(cb, z, z, jnp.pad(w, ((0, SUBLANES - CONV_K), (0, 0))), wout)


ATT_SUB = 3
ATT_UNROLL = 4
BF16_SUBLANES = 16
ATT_VROWS = ATTN_VALUE_DIM + BF16_SUBLANES


def _attn_kernel(lam_ref, q_ref, k_ref, vt_ref, g_ref, o_ref,
                 qlo_ref, qhi_ref, m_ref, acc_ref, s_ref, *, front, lam_init):
    sb = pl.program_id(2)
    blk = ATT_BLOCK
    real0 = front + N_META
    nt = (((1,), (1,)), ((), ()))

    q = q_ref[...]
    lane = lax.broadcasted_iota(jnp.int32, q.shape, 1)
    qlo_ref[...] = jnp.where(lane < ATTN_HEAD_DIM, q, jnp.zeros_like(q))
    qhi_ref[...] = jnp.where(lane >= ATTN_HEAD_DIM, q, jnp.zeros_like(q))
    m_ref[...] = jnp.full_like(m_ref, NEG_INF)
    acc_ref[...] = jnp.zeros_like(acc_ref)

    def qk(a, k):
        rows = slice(a * blk, (a + 1) * blk)
        return [lax.dot_general(k, qref[rows, :], nt, preferred_element_type=F32) for qref in (qlo_ref, qhi_ref)]

    def consume(a, j, scores, vt, masked):
        if masked:
            pk = j * blk + lax.broadcasted_iota(jnp.int32, (blk, 1), 0)
            pq = (sb * ATT_SUB + a) * blk + lax.broadcasted_iota(jnp.int32, (1, blk), 1)
            cid_k = jnp.where(pk < front, 1 << 30, jnp.where(pk < real0, 0, 1 + (pk - real0) // CHUNK))
            cid_q = jnp.where(pq < real0, 0, 1 + (pq - real0) // CHUNK)
            vis = cid_k <= cid_q
            scores = [jnp.where(vis, s, NEG_INF) for s in scores]
        for idx, s in enumerate(scores):
            m_old = m_ref[a, idx]
            m_new = jnp.maximum(m_old, jnp.max(s, axis=0, keepdims=True))
            alpha = jnp.exp2(m_old - m_new)
            p = jnp.exp2(s - m_new)
            m_ref[a, idx] = m_new
            acc_ref[a, idx] = alpha * acc_ref[a, idx] + jnp.dot(vt, p.astype(BF16), preferred_element_type=F32)

    def k_block(j):
        return k_ref[pl.ds(pl.multiple_of(j * blk, blk), blk), :]

    def kv_block(j):
        return k_block(j), vt_ref[j]

    def qk_all(k, buf):
        for a in range(ATT_SUB):
            for idx, s in enumerate(qk(a, k)):
                s_ref[buf, a, idx] = s

    def consume_all(j, buf, vt, masked):
        for a in range(ATT_SUB):
            consume(a, j, [s_ref[buf, a, 0], s_ref[buf, a, 1]], vt, masked)

    @pl.when(sb > 0)
    def _():
        k, vt = kv_block(0)
        qk_all(k, 0)
        consume_all(0, 0, vt, True)

    n_full = jnp.maximum(sb * ATT_SUB - 1, 0)
    n_trips = n_full // ATT_UNROLL

    @pl.when(n_trips > 0)
    def _():
        qk_all(k_block(1), 0)

    def body(jj, carry):
        j = 1 + ATT_UNROLL * jj
        for u in range(ATT_UNROLL):
            qk_all(k_block(jnp.minimum(j + u + 1, n_full)), (u + 1) % 2)
            consume_all(j + u, u % 2, vt_ref[j + u], False)
        return carry

    lax.fori_loop(0, n_trips, body, 0)

    def leftover(j, carry):
        k, vt = kv_block(j)
        qk_all(k, 0)
        consume_all(j, 0, vt, False)
        return carry

    lax.fori_loop(1 + n_trips * ATT_UNROLL, 1 + n_full, leftover, 0)

    band = [(dj, a) for dj in range(ATT_SUB) for a in range(dj, ATT_SUB)]
    assert len(band) <= 2 * ATT_SUB
    for n_pair, (dj, a) in enumerate(band):
        for idx, s in enumerate(qk(a, k_block(sb * ATT_SUB + dj))):
            s_ref[n_pair // ATT_SUB, n_pair % ATT_SUB, idx] = s
    for n_pair, (dj, a) in enumerate(band):
        j = sb * ATT_SUB + dj
        slot = (n_pair // ATT_SUB, n_pair % ATT_SUB)
        consume(a, j, [s_ref[slot[0], slot[1], 0], s_ref[slot[0], slot[1], 1]], vt_ref[j], True)

    lam = lam_ref[0]
    vd = ATTN_VALUE_DIM
    for a in range(ATT_SUB):
        o = (acc_ref[a, 0, :vd] / acc_ref[a, 0, vd:vd + 1]
             - lam * (acc_ref[a, 1, :vd] / acc_ref[a, 1, vd:vd + 1]))
        o = o * lax.rsqrt(jnp.mean(o * o, axis=0, keepdims=True) + RMS_EPS)
        o = o * g_ref[...] * (1.0 - lam_init)
        o_ref[a * blk:(a + 1) * blk, :] = o.T.astype(BF16)


def _attention(lam, q, k, vt4, g_col, *, bsz, lp, front, lam_init):
    t = q.shape[0]
    blk = ATT_BLOCK
    nb = lp // blk
    assert nb % ATT_SUB == 0 and ATT_UNROLL % 2 == 0
    nsb = nb // ATT_SUB
    tq = ATT_SUB * blk
    return pl.pallas_call(
        functools.partial(_attn_kernel, front=front, lam_init=lam_init),
        grid=(bsz, ATTN_HEADS, nsb),
        in_specs=[pl.BlockSpec(memory_space=pltpu.SMEM),
                  pl.BlockSpec((tq, LANES), lambda b, h, i: (b * nsb + i, h)),
                  pl.BlockSpec((lp, LANES), lambda b, h, i: (b, h)),
                  pl.BlockSpec((nb, None, ATT_VROWS, blk), lambda b, h, i: (b, h, 0, 0)),
                  pl.BlockSpec((LANES, blk), lambda b, h, i: (0, 0))],
        out_specs=pl.BlockSpec((tq, LANES), lambda b, h, i: (b * nsb + i, h)),
        out_shape=jax.ShapeDtypeStruct((t, ATTN_WIDTH), BF16),
        scratch_shapes=[pltpu.VMEM((tq, LANES), BF16), pltpu.VMEM((tq, LANES), BF16),
                        pltpu.VMEM((ATT_SUB, 2, 1, blk), F32),
                        pltpu.VMEM((ATT_SUB, 2, ATT_VROWS, blk), F32),
                        pltpu.VMEM((2, ATT_SUB, 2, blk, blk), F32)],
        compiler_params=_cparams(("parallel", "parallel", "arbitrary")),
        name="diff_attention",
    )(lam, q, k, vt4, g_col)


def _merge_kernel(h_ref, g_ref, ys_ref, yc_ref, o_ref, wao_ref, wo_ref, lg_ref, lb_ref,
                  h1_ref, h1b_ref, *, tiles_per_seq, front):
    tm = h_ref.shape[0]
    ya = jnp.dot(o_ref[...], wao_ref[...], preferred_element_type=F32)
    g = g_ref[...].astype(F32)
    merged = g[:, :D_MODEL] * ys_ref[...] + g[:, D_MODEL:2 * D_MODEL] * yc_ref[...] + g[:, 2 * D_MODEL:] * ya
    r = DEEPNORM_ALPHA * h_ref[...] + jnp.dot(merged.astype(BF16), wo_ref[...], preferred_element_type=F32)
    y = _layer_norm(r, lg_ref[...], lb_ref[...])
    pos = _seq_pos(pl.program_id(0), tiles_per_seq, tm)
    y = jnp.where(pos >= front, y, 0.0)
    h1_ref[...] = y
    h1b_ref[...] = y.astype(BF16)


def _merge(h, g, ys, yc, o, wao, wo, lg, lb, *, lp, front):
    t, d = h.shape
    tm = _pick_tile(lp, (256,))
    row = lambda w: pl.BlockSpec((tm, w), lambda i: (i, 0))
    const = lambda a, b: pl.BlockSpec((a, b), lambda i: (0, 0))
    return pl.pallas_call(
        functools.partial(_merge_kernel, tiles_per_seq=lp // tm, front=front),
        grid=(t // tm,),
        in_specs=[row(d), row(3 * d), row(d), row(d), row(ATTN_WIDTH),
                  const(ATTN_WIDTH, d), const(d, d), const(1, d), const(1, d)],
        out_specs=[row(d), row(d)],
        out_shape=[jax.ShapeDtypeStruct((t, d), F32), jax.ShapeDtypeStruct((t, d), BF16)],
        compiler_params=_cparams(("parallel",)),
        name="merge_ln1",
    )(h, g, ys, yc, o, wao, wo, lg.reshape(1, d), lb.reshape(1, d))


def _router_kernel(h_ref, w_ref, b_ref, ids_ref, wts_ref, rank_ref, cnt_ref, base_ref):
    tm = h_ref.shape[0]

    @pl.when(pl.program_id(0) == 0)
    def _():
        base_ref[...] = jnp.zeros_like(base_ref)

    logits = jnp.dot(h_ref[...], w_ref[...], preferred_element_type=F32,
                     precision=lax.Precision.HIGHEST) + b_ref[...]
    lane = lax.broadcasted_iota(jnp.int32, logits.shape, 1).astype(F32)
    work = logits
    sels, vals, firsts = [], [], []
    for _ in range(TOP_K):
        mx = jnp.max(work, axis=1, keepdims=True)
        first = jnp.min(jnp.where(work == mx, lane, float(N_EXPERTS)), axis=1, keepdims=True)
        sel = lane == first
        sels.append(sel)
        vals.append(mx)
        firsts.append(first)
        work = jnp.where(sel, -jnp.inf, work)
    exps = [jnp.exp(v - vals[0]) for v in vals]
    den = exps[0] + exps[1] + exps[2] + exps[3]

    onehot = jnp.zeros_like(logits)
    for sel in sels:
        onehot = onehot + jnp.where(sel, 1.0, 0.0)
    r = lax.broadcasted_iota(jnp.int32, (tm, tm), 0)
    c = lax.broadcasted_iota(jnp.int32, (tm, tm), 1)
    tri = jnp.where(r > c, 1.0, 0.0).astype(BF16)
    before = jnp.dot(tri, onehot.astype(BF16), preferred_element_type=F32) + base_ref[...]
    base_ref[...] = base_ref[...] + jnp.sum(onehot, axis=0, keepdims=True)
    cnt_ref[...] = base_ref[...]

    out_lane = lax.broadcasted_iota(jnp.int32, (tm, LANES), 1)
    ids = jnp.zeros((tm, LANES), F32)
    wts = jnp.zeros((tm, LANES), F32)
    rank = jnp.zeros((tm, LANES), F32)
    for kk in range(TOP_K):
        rk = jnp.sum(jnp.where(sels[kk], before, 0.0), axis=1, keepdims=True)
        ids = jnp.where(out_lane == kk, firsts[kk], ids)
        wts = jnp.where(out_lane == kk, exps[kk] / den, wts)
        rank = jnp.where(out_lane == kk, rk, rank)
    ids_ref[...] = ids.astype(jnp.int32)
    wts_ref[...] = wts
    rank_ref[...] = rank.astype(jnp.int32)


def _router(h, w, b):
    t, d = h.shape
    tm = _pick_tile(t, (512, 256))
    wide = lambda dt: jax.ShapeDtypeStruct((t, LANES), dt)
    row = pl.BlockSpec((tm, LANES), lambda i: (i, 0))
    return pl.pallas_call(
        _router_kernel,
        grid=(t // tm,),
        in_specs=[pl.BlockSpec((tm, d), lambda i: (i, 0)),
                  pl.BlockSpec((d, N_EXPERTS), lambda i: (0, 0)),
                  pl.BlockSpec((1, N_EXPERTS), lambda i: (0, 0))],
        out_specs=[row, row, row, pl.BlockSpec((1, N_EXPERTS), lambda i: (0, 0))],
        out_shape=[wide(jnp.int32), wide(F32), wide(jnp.int32), jax.ShapeDtypeStruct((1, N_EXPERTS), F32)],
        scratch_shapes=[pltpu.VMEM((1, N_EXPERTS), F32)],
        compiler_params=_cparams(("arbitrary",)),
        name="moe_router",
    )(h, w, b.reshape(1, N_EXPERTS))


MOE_TILE = 512
ROUTE_TOK = 256
ROW_TILE = (SUBLANES, LANES)
assert D_MODEL == SUBLANES * LANES
VMEM_LIMIT_EXPERTS = 58 * 1024 * 1024


def _rows_to_tiles(x):
    return x.reshape((x.shape[0],) + ROW_TILE)


def _tiles_to_rows(x):
    return x.reshape(x.shape[0], D_MODEL)


def _dispatch_kernel(last_ref, nu_ref, dest_ref, h_ref, xs_ref, zero_ref, stage_ref, zsem, sems):
    i = pl.program_id(0)
    n = pl.num_programs(0)
    n_tiles = xs_ref.shape[0] // MOE_TILE

    def fill(tile):
        return pltpu.make_async_copy(zero_ref, xs_ref.at[pl.ds(tile * MOE_TILE, MOE_TILE)], zsem)

    @pl.when(i == 0)
    def _():
        zero_ref[...] = jnp.zeros_like(zero_ref)
        for e in range(N_EXPERTS):
            @pl.when(last_ref[e] >= 0)
            def _():
                fill(last_ref[e]).start()

        def start_unused(tile, carry):
            fill(tile).start()
            return carry

        def wait_unused(tile, carry):
            fill(tile).wait()
            return carry

        lax.fori_loop(nu_ref[0], n_tiles, start_unused, 0)
        for e in range(N_EXPERTS):
            @pl.when(last_ref[e] >= 0)
            def _():
                fill(last_ref[e]).wait()
        lax.fori_loop(nu_ref[0], n_tiles, wait_unused, 0)

    slot = lax.rem(i, 2)
    stage_ref[slot] = _rows_to_tiles(h_ref[...])

    def issue(r8, carry):
        base = pl.multiple_of(r8 * SUBLANES, SUBLANES)
        for rr in range(SUBLANES):
            for kk in range(TOP_K):
                d = dest_ref[(base + rr) * TOP_K + kk]
                pltpu.make_async_copy(stage_ref.at[slot, pl.ds(base + rr, 1)], xs_ref.at[pl.ds(d, 1)],
                                      sems.at[slot]).start(priority=kk % 2)
        return carry

    lax.fori_loop(0, ROUTE_TOK // SUBLANES, issue, 0)

    def drain(s):
        for _ in range(TOP_K):
            pltpu.make_async_copy(stage_ref.at[s], xs_ref.at[pl.ds(0, ROUTE_TOK)], sems.at[s]).wait()

    @pl.when(i > 0)
    def _():
        drain(1 - slot)

    @pl.when(i == n - 1)
    def _():
        drain(slot)


def _dispatch(h, dest_flat, last_tile, n_used, n_rows):
    t, d = h.shape
    return pl.pallas_call(
        _dispatch_kernel,
        grid_spec=pltpu.PrefetchScalarGridSpec(
            num_scalar_prefetch=2,
            grid=(t // ROUTE_TOK,),
            in_specs=[pl.BlockSpec((ROUTE_TOK * TOP_K,), lambda i, last, nu: (i,), memory_space=pltpu.SMEM),
                      pl.BlockSpec((ROUTE_TOK, d), lambda i, last, nu: (i, 0))],
            out_specs=pl.BlockSpec(memory_space=pl.ANY),
            scratch_shapes=[pltpu.VMEM((MOE_TILE,) + ROW_TILE, F32), pltpu.VMEM((2, ROUTE_TOK) + ROW_TILE, F32),
                            pltpu.SemaphoreType.DMA(()), pltpu.SemaphoreType.DMA((2,))]),
        out_shape=jax.ShapeDtypeStruct((n_rows,) + ROW_TILE, F32),
        compiler_params=_cparams(("arbitrary",)),
        name="moe_dispatch",
    )(last_tile, n_used, dest_flat, h)


def _experts_kernel(te_ref, nu_ref, xs_ref, wgu_ref, bgu_ref, wd_ref, bd_ref, y_ref, wgu_b_ref, wd_b_ref):
    i = pl.program_id(0)

    @pl.when(jnp.logical_or(i == 0, te_ref[i] != te_ref[jnp.maximum(i - 1, 0)]))
    def _():
        wgu_b_ref[...] = wgu_ref[0].astype(BF16)
        wd_b_ref[...] = wd_ref[0].astype(BF16)

    @pl.when(i >= nu_ref[0])
    def _():
        y_ref[...] = jnp.zeros_like(y_ref)

    @pl.when(i < nu_ref[0])
    def _():
        x = _tiles_to_rows(xs_ref[...]).astype(BF16)
        gu = jnp.dot(x, wgu_b_ref[...], preferred_element_type=F32) + bgu_ref[0]
        gate = jnp.minimum(gu[:, :D_FF], SWIGLU_LIMIT)
        up = jnp.clip(gu[:, D_FF:], -SWIGLU_LIMIT, SWIGLU_LIMIT)
        hid = (up + 1.0) * gate * jax.nn.sigmoid(SWIGLU_ALPHA * gate)
        y_ref[...] = _rows_to_tiles(jnp.dot(hid.astype(BF16), wd_b_ref[...], preferred_element_type=F32)
                                    + bd_ref[0])


def _experts(tile_expert, n_used, xs, wgu, bgu, wd, bd, *, layer):
    n_rows = xs.shape[0]
    d = D_MODEL
    n_exp = wgu.shape[1]
    rows = lambda i, te, nu: (jnp.minimum(i, nu[0] - 1), 0, 0)
    per_e = lambda i, te, nu: (te[i], 0, 0)
    per_le = lambda i, te, nu: (layer, te[i], 0, 0)
    return pl.pallas_call(
        _experts_kernel,
        grid_spec=pltpu.PrefetchScalarGridSpec(
            num_scalar_prefetch=2,
            grid=(n_rows // MOE_TILE,),
            in_specs=[pl.BlockSpec((MOE_TILE,) + ROW_TILE, rows),
                      pl.BlockSpec((None, 1, d, 2 * D_FF), per_le),
                      pl.BlockSpec((1, 1, 2 * D_FF), per_e),
                      pl.BlockSpec((None, 1, D_FF, d), per_le),
                      pl.BlockSpec((1, 1, d), per_e)],
            out_specs=pl.BlockSpec((MOE_TILE,) + ROW_TILE, lambda i, te, nu: (i, 0, 0)),
            scratch_shapes=[pltpu.VMEM((d, 2 * D_FF), BF16), pltpu.VMEM((D_FF, d), BF16)]),
        out_shape=jax.ShapeDtypeStruct((n_rows,) + ROW_TILE, F32),
        compiler_params=pltpu.CompilerParams(dimension_semantics=("arbitrary",),
                                             vmem_limit_bytes=VMEM_LIMIT_EXPERTS),
        name="moe_experts",
    )(tile_expert, n_used, xs, wgu, bgu.reshape(n_exp, 1, 2 * D_FF), wd, bd.reshape(n_exp, 1, d))


COMBINE_TOK = 128


def _combine_kernel(dcur_ref, dnext_ref, w_ref, h_ref, y_ref, lg_ref, lb_ref, h2_ref, h2b_ref,
                    ybuf_ref, sems, *, tiles_per_seq, front):
    i = pl.program_id(0)
    n = pl.num_programs(0)
    slot = lax.rem(i, 2)

    def issue(d_ref, s):
        def body(r8, carry):
            base = pl.multiple_of(r8 * SUBLANES, SUBLANES)
            for rr in range(SUBLANES):
                for kk in range(TOP_K):
                    d = d_ref[(base + rr) * TOP_K + kk]
                    pltpu.make_async_copy(y_ref.at[pl.ds(d, 1)], ybuf_ref.at[s, kk, pl.ds(base + rr, 1)],
                                          sems.at[s]).start(priority=kk % 2)
            return carry
        lax.fori_loop(0, COMBINE_TOK // SUBLANES, body, 0)

    @pl.when(i == 0)
    def _():
        issue(dcur_ref, 0)

    @pl.when(i + 1 < n)
    def _():
        issue(dnext_ref, 1 - slot)

    for kk in range(TOP_K):
        pltpu.make_async_copy(y_ref.at[pl.ds(0, COMBINE_TOK)], ybuf_ref.at[slot, kk], sems.at[slot]).wait()

    w = w_ref[...]
    acc = w[:, 0:1] * _tiles_to_rows(ybuf_ref[slot, 0])
    for kk in range(1, TOP_K):
        acc = acc + w[:, kk:kk + 1] * _tiles_to_rows(ybuf_ref[slot, kk])
    y = _layer_norm(DEEPNORM_ALPHA * h_ref[...] + acc, lg_ref[...], lb_ref[...])
    pos = _seq_pos(i, tiles_per_seq, COMBINE_TOK)
    y = jnp.where(pos >= front, y, 0.0)
    h2_ref[...] = y
    h2b_ref[...] = y.astype(BF16)


def _combine(dest_flat, wts, h, y, lg, lb, *, lp, front, drop_prefix=False):
    t, d = h.shape
    n = t // COMBINE_TOK
    blk = COMBINE_TOK * TOP_K
    row = lambda w: pl.BlockSpec((COMBINE_TOK, w), lambda i: (i, 0))
    const = pl.BlockSpec((1, d), lambda i: (0, 0))
    out_rows, out_f32 = t, row(d)
    if drop_prefix:
        tps = lp // COMBINE_TOK
        skip = (front + N_META) // COMBINE_TOK
        out_rows = t - (t // lp) * (front + N_META)
        out_f32 = pl.BlockSpec((COMBINE_TOK, d),
                               lambda i: ((i // tps) * (tps - skip) + jnp.maximum(i % tps - skip, 0), 0))
    return pl.pallas_call(
        functools.partial(_combine_kernel, tiles_per_seq=lp // COMBINE_TOK, front=front),
        grid=(n,),
        in_specs=[pl.BlockSpec((blk,), lambda i: (i,), memory_space=pltpu.SMEM),
                  pl.BlockSpec((blk,), lambda i: (jnp.minimum(i + 1, n - 1),), memory_space=pltpu.SMEM),
                  row(LANES), row(d), pl.BlockSpec(memory_space=pl.ANY), const, const],
        out_specs=[out_f32, row(d)],
        out_shape=[jax.ShapeDtypeStruct((out_rows, d), F32), jax.ShapeDtypeStruct((t, d), BF16)],
        scratch_shapes=[pltpu.VMEM((2, TOP_K, COMBINE_TOK) + ROW_TILE, F32), pltpu.SemaphoreType.DMA((2,))],
        compiler_params=_cparams(("arbitrary",)),
        name="moe_combine_ln2",
    )(dest_flat, dest_flat, wts, h, y, lg.reshape(1, d), lb.reshape(1, d))


def _routing_plan(ids, rank, counts, n_tiles):
    n_exp = counts.shape[0]
    padded = (counts + MOE_TILE - 1) // MOE_TILE * MOE_TILE
    ends = jnp.cumsum(padded)
    starts = ends - padded
    dest = (jnp.take(starts, ids) + rank).reshape(-1)
    n_used = (ends[-1] // MOE_TILE).reshape(1)
    tile_start = jnp.arange(n_tiles, dtype=jnp.int32) * MOE_TILE
    tile_expert = jnp.sum((tile_start[:, None] >= ends[None, :]).astype(jnp.int32), axis=1)
    last_used_expert = jnp.max(jnp.where(padded > 0, jnp.arange(n_exp, dtype=jnp.int32), 0))
    tile_expert = jnp.minimum(tile_expert, last_used_expert)
    last_tile = jnp.where(padded > 0, ends // MOE_TILE - 1, -1)
    return dest.astype(jnp.int32), tile_expert.astype(jnp.int32), n_used.astype(jnp.int32), last_tile.astype(jnp.int32)


def _rope_tables(lp, front):
    pos = (jnp.arange(lp, dtype=jnp.int32) - front).astype(F32)
    inv = ROPE_THETA ** (-jnp.arange(0, ATTN_HEAD_DIM, 2, dtype=F32) / ATTN_HEAD_DIM)
    ang = pos[:, None] * inv[None, :]
    ang = jnp.concatenate([ang, ang, ang, ang], axis=-1)
    return jnp.cos(ang), jnp.sin(ang)


def kernel(x, meta_tokens, ln_in_g, ln_in_b, w_in, ssm_lambda_re, ssm_lambda_im, ssm_log_dt, ssm_b_re, ssm_b_im, ssm_c_re, ssm_c_im, ssm_d, ssm_w_glu, ssm_w_out, conv_w, conv_w_out, attn_lambda_q1, attn_lambda_k1, attn_lambda_q2, attn_lambda_k2, attn_subln_g, attn_w_out, gate_w, gate_b, w_o, ln1_g, ln1_b, router_w, router_b, expert_w_gu, expert_b_gu, expert_w_down, expert_b_down, ln2_g, ln2_b):
    bsz, seq, d = x.shape
    assert d == D_MODEL and seq % ATT_BLOCK == 0
    lp = seq + ATT_BLOCK
    front = ATT_BLOCK - N_META
    t = bsz * lp

    meta = jnp.broadcast_to(meta_tokens.astype(x.dtype)[None], (bsz, N_META, d))
    hcat = jnp.concatenate([jnp.zeros((bsz, front, d), x.dtype), meta, x], axis=1).reshape(t, d)
    h, hb = _ln_in(hcat, ln_in_g, ln_in_b, lp=lp, front=front)
    cos, sin = _rope_tables(lp, front)

    s3 = SSM_WIDTH + 3 * CONV_WIDTH
    s5 = s3 + 2 * ATTN_WIDTH
    for l in range(DEPTH):
        w_in_b = w_in[l].astype(BF16)
        u, cb, z = _proj_sc(hb, w_in_b[:, :s3])
        q, k = _proj_qk(hb, w_in_b[:, s3:s5], cos, sin, lp=lp)
        vt = _proj_vt(hb, w_in_b[:, s5:].T)
        g = _proj_gate(hb, gate_w[l].astype(BF16), gate_b[l])

        bdb, bdc, pw = _ssm_tables(ssm_lambda_re[l], ssm_lambda_im[l], ssm_log_dt[l], ssm_b_re[l], ssm_b_im[l],
                                   ssm_c_re[l], ssm_c_im[l])
        ys = _ssm_branch(u, bdb, bdc, pw, ssm_d[l].astype(F32), ssm_w_glu[l].astype(BF16),
                         ssm_w_out[l].astype(BF16), bsz=bsz, lp=lp)
        yc = _conv_branch(cb, z, conv_w[l].astype(F32), conv_w_out[l].astype(BF16))

        lam_init = 0.8 - 0.6 * math.exp(-0.3 * l)
        lam = (jnp.exp(jnp.sum(attn_lambda_q1[l].astype(F32) * attn_lambda_k1[l].astype(F32)))
               - jnp.exp(jnp.sum(attn_lambda_q2[l].astype(F32) * attn_lambda_k2[l].astype(F32)))
               + lam_init).reshape(1)
        g_col = jnp.broadcast_to(attn_subln_g[l].astype(F32)[:, None], (ATTN_VALUE_DIM, ATT_BLOCK))
        o = _attention(lam, q, k, vt, g_col, bsz=bsz, lp=lp, front=front, lam_init=lam_init)

        h, hb = _merge(h, g, ys, yc, o, attn_w_out[l].astype(BF16), w_o[l].astype(BF16), ln1_g[l], ln1_b[l],
                       lp=lp, front=front)

        ids, wts, rank, counts = _router(h, router_w[l].astype(F32), router_b[l].astype(F32))
        n_tiles = t * TOP_K // MOE_TILE + N_EXPERTS
        dest, tile_expert, n_used, last_tile = _routing_plan(ids[:, :TOP_K], rank[:, :TOP_K],
                                                             counts[0].astype(jnp.int32), n_tiles)
        xs = _dispatch(h, dest, last_tile, n_used, n_tiles * MOE_TILE)
        y = _experts(tile_expert, n_used, xs, expert_w_gu.astype(F32), expert_b_gu[l].astype(F32),
                     expert_w_down.astype(F32), expert_b_down[l].astype(F32), layer=l)
        h, hb = _combine(dest, wts, h, y, ln2_g[l], ln2_b[l], lp=lp, front=front, drop_prefix=(l == DEPTH - 1))

    return h.reshape(bsz, seq, d)
```

```python
import functools
import math

import jax
import jax.numpy as jnp
from jax import lax
from jax.experimental import pallas as pl
from jax.experimental.pallas import tpu as pltpu

F32 = jnp.float32
BF16 = jnp.bfloat16

D_MODEL = 1024
DEPTH = 2
CHUNK = 64
N_META = 16
SSM_WIDTH = 512
SSM_GROUP = 16
SSM_GROUPS = 32
SSM_STATE = 64
SSM_CH = SSM_GROUPS * SSM_STATE
CONV_WIDTH = 512
CONV_K = 3
ATTN_HEADS = 8
ATTN_HEAD_DIM = 64
ATTN_VALUE_DIM = 128
ATTN_WIDTH = 1024
ROPE_THETA = 10000.0
N_EXPERTS = 32
TOP_K = 4
D_FF = 1024
SWIGLU_LIMIT = 7.0
SWIGLU_ALPHA = 1.702
DEEPNORM_ALPHA = (2.0 * DEPTH) ** 0.25
LN_EPS = 1e-5
RMS_EPS = 1e-5
NEG_INF = -1e30

ATT_BLOCK = 256
LANES = 128
SUBLANES = 8
VMEM_LIMIT = 48 * 1024 * 1024


def _cparams(sem):
    return pltpu.CompilerParams(dimension_semantics=sem, vmem_limit_bytes=VMEM_LIMIT)


def _pick_tile(n, candidates):
    for c in candidates:
        if n % c == 0:
            return c
    raise ValueError(f"no tile for {n}")


def _layer_norm(x, g, b):
    mu = jnp.mean(x, axis=-1, keepdims=True)
    xc = x - mu
    var = jnp.mean(xc * xc, axis=-1, keepdims=True)
    return xc * lax.rsqrt(var + LN_EPS) * g + b


def _seq_pos(tile_idx, tiles_per_seq, tm):
    base = lax.rem(tile_idx, tiles_per_seq) * tm
    return base + lax.broadcasted_iota(jnp.int32, (tm, 1), 0)


def _ln_in_kernel(x_ref, g_ref, b_ref, h_ref, hb_ref, *, tiles_per_seq, front):
    tm = x_ref.shape[0]
    y = _layer_norm(x_ref[...], g_ref[...], b_ref[...])
    pos = _seq_pos(pl.program_id(0), tiles_per_seq, tm)
    y = jnp.where(pos >= front, y, 0.0)
    h_ref[...] = y
    hb_ref[...] = y.astype(BF16)


def _ln_in(hcat, g, b, *, lp, front):
    t, d = hcat.shape
    tm = _pick_tile(lp, (512, 256))
    return pl.pallas_call(
        functools.partial(_ln_in_kernel, tiles_per_seq=lp // tm, front=front),
        grid=(t // tm,),
        in_specs=[pl.BlockSpec((tm, d), lambda i: (i, 0)),
                  pl.BlockSpec((1, d), lambda i: (0, 0)),
                  pl.BlockSpec((1, d), lambda i: (0, 0))],
        out_specs=[pl.BlockSpec((tm, d), lambda i: (i, 0)),
                   pl.BlockSpec((tm, d), lambda i: (i, 0))],
        out_shape=[jax.ShapeDtypeStruct((t, d), F32), jax.ShapeDtypeStruct((t, d), BF16)],
        compiler_params=_cparams(("parallel",)),
        name="ln_in",
    )(hcat, g.reshape(1, d), b.reshape(1, d))


def _proj_sc_kernel(x_ref, w_ref, u_ref, cb_ref, z_ref):
    acc = jnp.dot(x_ref[...], w_ref[...], preferred_element_type=F32)
    s0, s1, s2, s3 = SSM_WIDTH, SSM_WIDTH + CONV_WIDTH, SSM_WIDTH + 2 * CONV_WIDTH, SSM_WIDTH + 3 * CONV_WIDTH
    u_ref[...] = acc[:, :s0]
    cb_ref[...] = acc[:, s0:s1]
    z_ref[...] = acc[:, s1:s2] * acc[:, s2:s3]


def _proj_sc(hb, w):
    t, d = hb.shape
    n = w.shape[1]
    tm = _pick_tile(t, (704, 512, 256))
    out = jax.ShapeDtypeStruct((t, SSM_WIDTH), F32)
    ospec = pl.BlockSpec((tm, SSM_WIDTH), lambda i: (i, 0))
    return pl.pallas_call(
        _proj_sc_kernel,
        grid=(t // tm,),
        in_specs=[pl.BlockSpec((tm, d), lambda i: (i, 0)),
                  pl.BlockSpec((d, n), lambda i: (0, 0))],
        out_specs=[ospec, ospec, ospec],
        out_shape=[out, out, out],
        compiler_params=_cparams(("parallel",)),
        name="proj_ssm_conv",
    )(hb, w)


def _proj_qk_kernel(x_ref, w_ref, cos_ref, sin_ref, q_ref, k_ref):
    acc = jnp.dot(x_ref[...], w_ref[...], preferred_element_type=F32)
    cos = cos_ref[...]
    sin = sin_ref[...]
    lane = lax.broadcasted_iota(jnp.int32, (1, LANES), 1)
    low_half = lax.rem(lane, ATTN_HEAD_DIM) < ATTN_HEAD_DIM // 2
    nblk = ATTN_WIDTH // LANES
    for blk in range(2 * nblk):
        a = acc[:, blk * LANES:(blk + 1) * LANES]
        rot = jnp.where(low_half, -pltpu.roll(a, LANES - ATTN_HEAD_DIM // 2, 1),
                        pltpu.roll(a, ATTN_HEAD_DIM // 2, 1))
        r = a * cos + rot * sin
        if blk < nblk:
            q_ref[:, blk * LANES:(blk + 1) * LANES] = (r * (ATTN_HEAD_DIM ** -0.5 * math.log2(math.e))).astype(BF16)
        else:
            k_ref[:, (blk - nblk) * LANES:(blk - nblk + 1) * LANES] = r.astype(BF16)


def _proj_qk(hb, w, cos, sin, *, lp):
    t, d = hb.shape
    tm = _pick_tile(lp, (704, 512, 256))
    tps = lp // tm
    out = jax.ShapeDtypeStruct((t, ATTN_WIDTH), BF16)
    return pl.pallas_call(
        _proj_qk_kernel,
        grid=(t // tm,),
        in_specs=[pl.BlockSpec((tm, d), lambda i: (i, 0)),
                  pl.BlockSpec((d, 2 * ATTN_WIDTH), lambda i: (0, 0)),
                  pl.BlockSpec((tm, LANES), lambda i: (i % tps, 0)),
                  pl.BlockSpec((tm, LANES), lambda i: (i % tps, 0))],
        out_specs=[pl.BlockSpec((tm, ATTN_WIDTH), lambda i: (i, 0)),
                   pl.BlockSpec((tm, ATTN_WIDTH), lambda i: (i, 0))],
        out_shape=[out, out],
        compiler_params=_cparams(("parallel",)),
        name="proj_qk_rope",
    )(hb, w, cos, sin)


def _proj_vt_kernel(wt_ref, x_ref, o_ref):
    vt = lax.dot_general(wt_ref[...], x_ref[...], (((1,), (1,)), ((), ())),
                         preferred_element_type=F32).astype(BF16)
    vd = ATTN_VALUE_DIM
    row = lax.broadcasted_iota(jnp.int32, (ATT_VROWS - vd, ATT_BLOCK), 0)
    tail = jnp.where(row == 0, 1.0, 0.0).astype(BF16)
    for c in range(o_ref.shape[0]):
        for h in range(ATTN_HEADS):
            o_ref[c, h, :vd, :] = vt[h * vd:(h + 1) * vd, c * ATT_BLOCK:(c + 1) * ATT_BLOCK]
            o_ref[c, h, vd:, :] = tail


def _proj_vt(hb, wt):
    t, d = hb.shape
    tm = _pick_tile(t, (512, 256))
    per = tm // ATT_BLOCK
    return pl.pallas_call(
        _proj_vt_kernel,
        grid=(t // tm,),
        in_specs=[pl.BlockSpec((ATTN_WIDTH, d), lambda i: (0, 0)),
                  pl.BlockSpec((tm, d), lambda i: (i, 0))],
        out_specs=pl.BlockSpec((per, ATTN_HEADS, ATT_VROWS, ATT_BLOCK), lambda i: (i, 0, 0, 0)),
        out_shape=jax.ShapeDtypeStruct((t // ATT_BLOCK, ATTN_HEADS, ATT_VROWS, ATT_BLOCK), BF16),
        compiler_params=_cparams(("parallel",)),
        name="proj_v_t",
    )(wt, hb)


SCAN_STRIP = 512
SSM_SUPER = 2


def _ssm_kernel(u_ref, bdb_ref, bdc_ref, pw_ref, dskip_ref, wglu_ref, wout_ref, y_ref, bu_ref, carry_ref):
    tm = u_ref.shape[0]

    @pl.when(pl.program_id(1) == 0)
    def _():
        carry_ref[...] = jnp.zeros_like(carry_ref)

    u = u_ref[...]
    ub = u.astype(BF16)
    sgw = SSM_WIDTH // SSM_SUPER
    sgc = SSM_CH // SSM_SUPER
    for sg in range(SSM_SUPER):
        bu = jnp.dot(ub[:, sg * sgw:(sg + 1) * sgw], bdb_ref[sg], preferred_element_type=F32)
        bu_ref[:, sg * sgc:(sg + 1) * sgc] = bu[:, :sgc]
        bu_ref[:, SSM_CH + sg * sgc:SSM_CH + (sg + 1) * sgc] = bu[:, sgc:]

    for s in range(SSM_CH // SCAN_STRIP):
        re = slice(s * SCAN_STRIP, (s + 1) * SCAN_STRIP)
        im = slice(SSM_CH + s * SCAN_STRIP, SSM_CH + (s + 1) * SCAN_STRIP)
        steps = []
        for k, d in enumerate((1, 2, 4)):
            rows = slice(k * SUBLANES, (k + 1) * SUBLANES)
            steps.append((d, pw_ref[rows, re], pw_ref[rows, im]))
        pr = pw_ref[3 * SUBLANES:4 * SUBLANES, re]
        pi = pw_ref[3 * SUBLANES:4 * SUBLANES, im]

        def group(gi, carry):
            cr, ci = carry
            r0 = pl.multiple_of(gi * SUBLANES, SUBLANES)
            xr = bu_ref[pl.ds(r0, SUBLANES), re]
            xi = bu_ref[pl.ds(r0, SUBLANES), im]
            for d, ar, ai in steps:
                sr = pltpu.roll(xr, d, 0)
                si = pltpu.roll(xi, d, 0)
                xr, xi = xr + ar * sr - ai * si, xi + ar * si + ai * sr
            xr, xi = xr + pr * cr - pi * ci, xi + pr * ci + pi * cr
            bu_ref[pl.ds(r0, SUBLANES), re] = xr
            bu_ref[pl.ds(r0, SUBLANES), im] = xi
            last_r = jnp.broadcast_to(xr[SUBLANES - 1:SUBLANES, :], (SUBLANES, SCAN_STRIP))
            last_i = jnp.broadcast_to(xi[SUBLANES - 1:SUBLANES, :], (SUBLANES, SCAN_STRIP))
            return last_r, last_i

        cr, ci = lax.fori_loop(0, tm // SUBLANES, group, (carry_ref[:, re], carry_ref[:, im]))
        carry_ref[:, re] = cr
        carry_ref[:, im] = ci

    ys = []
    for sg in range(SSM_SUPER):
        xr = bu_ref[:, sg * sgc:(sg + 1) * sgc].astype(BF16)
        xi = bu_ref[:, SSM_CH + sg * sgc:SSM_CH + (sg + 1) * sgc].astype(BF16)
        ys.append(jnp.dot(xr, bdc_ref[sg, 0], preferred_element_type=F32)
                  + jnp.dot(xi, bdc_ref[sg, 1], preferred_element_type=F32))
    y = jnp.concatenate(ys, axis=1) + dskip_ref[...] * u
    y = jax.nn.gelu(y)
    y = y * jax.nn.sigmoid(jnp.dot(y.astype(BF16), wglu_ref[...], preferred_element_type=F32))
    y_ref[...] = jnp.dot(y.astype(BF16), wout_ref[...], preferred_element_type=F32).astype(BF16)


def _ssm_branch(u, bdb, bdc, pw, dskip, wglu, wout, *, bsz, lp):
    t = u.shape[0]
    tm = _pick_tile(lp, (256,))
    tps = lp // tm
    const = lambda shape: pl.BlockSpec(shape, lambda b, i: (0,) * len(shape))
    return pl.pallas_call(
        _ssm_kernel,
        grid=(bsz, tps),
        in_specs=[pl.BlockSpec((tm, SSM_WIDTH), lambda b, i: (b * tps + i, 0)),
                  const(bdb.shape), const(bdc.shape), const(pw.shape), const((1, SSM_WIDTH)),
                  const(wglu.shape), const(wout.shape)],
        out_specs=pl.BlockSpec((tm, D_MODEL), lambda b, i: (b * tps + i, 0)),
        out_shape=jax.ShapeDtypeStruct((t, D_MODEL), BF16),
        scratch_shapes=[pltpu.VMEM((tm, 2 * SSM_CH), F32), pltpu.VMEM((SUBLANES, 2 * SSM_CH), F32)],
        compiler_params=_cparams(("arbitrary", "arbitrary")),
        name="ssm_branch",
    )(u, bdb, bdc, pw, dskip.reshape(1, SSM_WIDTH), wglu, wout)


def _ssm_tables(lam_re, lam_im, log_dt, b_re, b_im, c_re, c_im):
    lr, li = lam_re.astype(F32), lam_im.astype(F32)
    dt = jnp.exp(log_dt.astype(F32))[:, None]
    mag = jnp.exp(lr * dt)
    ar = mag * jnp.cos(li * dt)
    ai = mag * jnp.sin(li * dt)
    denom = lr * lr + li * li
    nr, ni = ar - 1.0, ai
    coef_r = (nr * lr + ni * li) / denom
    coef_i = (ni * lr - nr * li) / denom
    br, bi = b_re.astype(F32), b_im.astype(F32)
    bbar_r = coef_r[..., None] * br - coef_i[..., None] * bi
    bbar_i = coef_r[..., None] * bi + coef_i[..., None] * br
    gps = SSM_GROUPS // SSM_SUPER
    eye = jnp.eye(gps, dtype=F32)
    sgw, sgc = SSM_WIDTH // SSM_SUPER, SSM_CH // SSM_SUPER
    split = lambda m: m.reshape((SSM_SUPER, gps) + m.shape[1:])
    bd_in = lambda m: jnp.einsum('sgpc,gh->sgchp', split(m), eye).reshape(SSM_SUPER, sgw, sgc)
    bdb = jnp.concatenate([bd_in(bbar_r), bd_in(bbar_i)], axis=2)
    bd_out = lambda m: jnp.einsum('sgcp,gh->sgphc', split(m), eye).reshape(SSM_SUPER, sgc, sgw)
    bdc = jnp.stack([bd_out(c_re.astype(F32)), -bd_out(c_im.astype(F32))], axis=1)
    a1 = (ar.reshape(-1), ai.reshape(-1))
    cmul = lambda x, y: (x[0] * y[0] - x[1] * y[1], x[0] * y[1] + x[1] * y[0])
    pows = [a1]
    for _ in range(SUBLANES - 1):
        pows.append(cmul(pows[-1], a1))
    row = jnp.arange(SUBLANES)[:, None]
    both = lambda p: jnp.concatenate(p)[None, :]
    tables = [jnp.where(row >= d, both(pows[d - 1]), 0.0) for d in (1, 2, 4)]
    tables.append(jnp.concatenate([both(p) for p in pows], axis=0))
    pw = jnp.concatenate(tables, axis=0)
    return bdb.astype(BF16), bdc.astype(BF16), pw


def _conv_kernel(cb_ref, z_ref, zprev_ref, w_ref, wout_ref, y_ref):
    tm = z_ref.shape[0]
    z = z_ref[...]
    zz = jnp.concatenate([zprev_ref[...], z], axis=0)
    w = w_ref[...]
    y = w[2:3, :] * z
    for j in range(CONV_K - 1):
        shift = CONV_K - 1 - j
        y = y + w[j:j + 1, :] * zz[SUBLANES - shift:SUBLANES - shift + tm, :]
    y = cb_ref[...] * y
    y_ref[...] = jnp.dot(y.astype(BF16), wout_ref[...], preferred_element_type=F32).astype(BF16)


def _conv_branch(cb, z, w, wout):
    t = z.shape[0]
    tm = _pick_tile(t, (512, 256))
    per = tm // SUBLANES
    return pl.pallas_call(
        _conv_kernel,
        grid=(t // tm,),
        in_specs=[pl.BlockSpec((tm, CONV_WIDTH), lambda i: (i, 0)),
                  pl.BlockSpec((tm, CONV_WIDTH), lambda i: (i, 0)),
                  pl.BlockSpec((SUBLANES, CONV_WIDTH), lambda i: (jnp.maximum(i * per - 1, 0), 0)),
                  pl.BlockSpec((SUBLANES, CONV_WIDTH), lambda i: (0, 0)),
                  pl.BlockSpec((CONV_WIDTH, D_MODEL), lambda i: (0, 0))],
        out_specs=pl.BlockSpec((tm, D_MODEL), lambda i: (i, 0)),
        out_shape=jax.ShapeDtypeStruct((t, D_MODEL), BF16),
        compiler_params=_cparams(("parallel",)),
        name="conv_branch",
    )(cb, z, z, jnp.pad(w, ((0, SUBLANES - CONV_K), (0, 0))), wout)


ATT_SUB = 3
ATT_UNROLL = 4
BF16_SUBLANES = 16
ATT_VROWS = ATTN_VALUE_DIM + BF16_SUBLANES


def _attn_kernel(lam_ref, q_ref, k_ref, vt_ref, g_ref, o_ref,
                 qlo_ref, qhi_ref, m_ref, acc_ref, s_ref, *, front, lam_init):
    sb = pl.program_id(2)
    blk = ATT_BLOCK
    real0 = front + N_META
    nt = (((1,), (1,)), ((), ()))

    q = q_ref[...]
    lane = lax.broadcasted_iota(jnp.int32, q.shape, 1)
    qlo_ref[...] = jnp.where(lane < ATTN_HEAD_DIM, q, jnp.zeros_like(q))
    qhi_ref[...] = jnp.where(lane >= ATTN_HEAD_DIM, q, jnp.zeros_like(q))
    m_ref[...] = jnp.full_like(m_ref, NEG_INF)
    acc_ref[...] = jnp.zeros_like(acc_ref)

    def qk(a, k):
        rows = slice(a * blk, (a + 1) * blk)
        return [lax.dot_general(k, qref[rows, :], nt, preferred_element_type=F32) for qref in (qlo_ref, qhi_ref)]

    def consume(a, j, scores, vt, masked, key_off=0):
        if masked:
            pk = j * blk + key_off + lax.broadcasted_iota(jnp.int32, (blk - key_off, 1), 0)
            pq = (sb * ATT_SUB + a) * blk + lax.broadcasted_iota(jnp.int32, (1, blk), 1)
            cid_k = jnp.where(pk < front, 1 << 30, jnp.where(pk < real0, 0, 1 + (pk - real0) // CHUNK))
            cid_q = jnp.where(pq < real0, 0, 1 + (pq - real0) // CHUNK)
            vis = cid_k <= cid_q
            scores = [jnp.where(vis, s, NEG_INF) for s in scores]
        for idx, s in enumerate(scores):
            m_old = m_ref[a, idx]
            m_new = jnp.maximum(m_old, jnp.max(s, axis=0, keepdims=True))
            alpha = jnp.exp2(m_old - m_new)
            p = jnp.exp2(s - m_new)
            m_ref[a, idx] = m_new
            acc_ref[a, idx] = alpha * acc_ref[a, idx] + jnp.dot(vt, p.astype(BF16), preferred_element_type=F32)

    def k_block(j):
        return k_ref[pl.ds(pl.multiple_of(j * blk, blk), blk), :]

    def kv_block(j):
        return k_block(j), vt_ref[j]

    def qk_all(k, buf):
        for a in range(ATT_SUB):
            for idx, s in enumerate(qk(a, k)):
                s_ref[buf, a, idx] = s

    def consume_all(j, buf, vt, masked):
        for a in range(ATT_SUB):
            consume(a, j, [s_ref[buf, a, 0], s_ref[buf, a, 1]], vt, masked)

    half = blk // 2
    assert front >= half

    @pl.when(sb > 0)
    def _():
        k = k_ref[half:blk, :]
        vt = vt_ref[0][:, half:]
        for a in range(ATT_SUB):
            for idx, s in enumerate(qk(a, k)):
                s_ref[0, a, idx, :half, :] = s
        for a in range(ATT_SUB):
            consume(a, 0, [s_ref[0, a, 0, :half, :], s_ref[0, a, 1, :half, :]], vt, True, key_off=half)

    n_full = jnp.maximum(sb * ATT_SUB - 1, 0)
    n_trips = n_full // ATT_UNROLL

    @pl.when(n_trips > 0)
    def _():
        qk_all(k_block(1), 0)

    def body(jj, carry):
        j = 1 + ATT_UNROLL * jj
        for u in range(ATT_UNROLL):
            qk_all(k_block(jnp.minimum(j + u + 1, n_full)), (u + 1) % 2)
            consume_all(j + u, u % 2, vt_ref[j + u], False)
        return carry

    lax.fori_loop(0, n_trips, body, 0)

    def leftover(j, carry):
        k, vt = kv_block(j)
        qk_all(k, 0)
        consume_all(j, 0, vt, False)
        return carry

    lax.fori_loop(1 + n_trips * ATT_UNROLL, 1 + n_full, leftover, 0)

    band = [(dj, a) for dj in range(ATT_SUB) for a in range(dj, ATT_SUB)]
    assert len(band) <= 2 * ATT_SUB
    for n_pair, (dj, a) in enumerate(band):
        for idx, s in enumerate(qk(a, k_block(sb * ATT_SUB + dj))):
            s_ref[n_pair // ATT_SUB, n_pair % ATT_SUB, idx] = s
    for n_pair, (dj, a) in enumerate(band):
        j = sb * ATT_SUB + dj
        slot = (n_pair // ATT_SUB, n_pair % ATT_SUB)
        consume(a, j, [s_ref[slot[0], slot[1], 0], s_ref[slot[0], slot[1], 1]], vt_ref[j], True)

    lam = lam_ref[0]
    vd = ATTN_VALUE_DIM
    for a in range(ATT_SUB):
        o = (acc_ref[a, 0, :vd] / acc_ref[a, 0, vd:vd + 1]
             - lam * (acc_ref[a, 1, :vd] / acc_ref[a, 1, vd:vd + 1]))
        o = o * lax.rsqrt(jnp.mean(o * o, axis=0, keepdims=True) + RMS_EPS)
        o = o * g_ref[...] * (1.0 - lam_init)
        o_ref[a * blk:(a + 1) * blk, :] = o.T.astype(BF16)


def _attention(lam, q, k, vt4, g_col, *, bsz, lp, front, lam_init):
    t = q.shape[0]
    blk = ATT_BLOCK
    nb = lp // blk
    assert nb % ATT_SUB == 0 and ATT_UNROLL % 2 == 0
    nsb = nb // ATT_SUB
    tq = ATT_SUB * blk
    return pl.pallas_call(
        functools.partial(_attn_kernel, front=front, lam_init=lam_init),
        grid=(bsz, ATTN_HEADS, nsb),
        in_specs=[pl.BlockSpec(memory_space=pltpu.SMEM),
                  pl.BlockSpec((tq, LANES), lambda b, h, i: (b * nsb + i, h)),
                  pl.BlockSpec((lp, LANES), lambda b, h, i: (b, h)),
                  pl.BlockSpec((nb, None, ATT_VROWS, blk), lambda b, h, i: (b, h, 0, 0)),
                  pl.BlockSpec((LANES, blk), lambda b, h, i: (0, 0))],
        out_specs=pl.BlockSpec((tq, LANES), lambda b, h, i: (b * nsb + i, h)),
        out_shape=jax.ShapeDtypeStruct((t, ATTN_WIDTH), BF16),
        scratch_shapes=[pltpu.VMEM((tq, LANES), BF16), pltpu.VMEM((tq, LANES), BF16),
                        pltpu.VMEM((ATT_SUB, 2, 1, blk), F32),
                        pltpu.VMEM((ATT_SUB, 2, ATT_VROWS, blk), F32),
                        pltpu.VMEM((2, ATT_SUB, 2, blk, blk), F32)],
        compiler_params=_cparams(("parallel", "parallel", "arbitrary")),
        name="diff_attention",
    )(lam, q, k, vt4, g_col)


def _merge_kernel(h_ref, hb_ref, ys_ref, yc_ref, o_ref, wg_ref, bg_ref, wao_ref, wo_ref, lg_ref, lb_ref,
                  h1_ref, h1b_ref, *, tiles_per_seq, front):
    tm = h_ref.shape[0]
    ya = jnp.dot(o_ref[...], wao_ref[...], preferred_element_type=F32)
    g = jax.nn.sigmoid(jnp.dot(hb_ref[...], wg_ref[...], preferred_element_type=F32) + bg_ref[...])
    merged = g[:, :D_MODEL] * ys_ref[...] + g[:, D_MODEL:2 * D_MODEL] * yc_ref[...] + g[:, 2 * D_MODEL:] * ya
    r = DEEPNORM_ALPHA * h_ref[...] + jnp.dot(merged.astype(BF16), wo_ref[...], preferred_element_type=F32)
    y = _layer_norm(r, lg_ref[...], lb_ref[...])
    pos = _seq_pos(pl.program_id(0), tiles_per_seq, tm)
    y = jnp.where(pos >= front, y, 0.0)
    h1_ref[...] = y
    h1b_ref[...] = y.astype(BF16)


def _merge(h, hb, ys, yc, o, wg, bg, wao, wo, lg, lb, *, lp, front):
    t, d = h.shape
    tm = _pick_tile(lp, (256,))
    row = lambda w: pl.BlockSpec((tm, w), lambda i: (i, 0))
    const = lambda a, b: pl.BlockSpec((a, b), lambda i: (0, 0))
    return pl.pallas_call(
        functools.partial(_merge_kernel, tiles_per_seq=lp // tm, front=front),
        grid=(t // tm,),
        in_specs=[row(d), row(d), row(d), row(d), row(ATTN_WIDTH),
                  const(d, 3 * d), const(1, 3 * d), const(ATTN_WIDTH, d), const(d, d), const(1, d), const(1, d)],
        out_specs=[row(d), row(d)],
        out_shape=[jax.ShapeDtypeStruct((t, d), F32), jax.ShapeDtypeStruct((t, d), BF16)],
        compiler_params=_cparams(("parallel",)),
        name="merge_ln1",
    )(h, hb, ys, yc, o, wg, bg.reshape(1, 3 * d), wao, wo, lg.reshape(1, d), lb.reshape(1, d))


def _router_kernel(h_ref, w_ref, b_ref, ids_ref, wts_ref, rank_ref, cnt_ref, base_ref):
    tm = h_ref.shape[0]

    @pl.when(pl.program_id(0) == 0)
    def _():
        base_ref[...] = jnp.zeros_like(base_ref)

    h = h_ref[...]
    w = w_ref[...]
    h_hi = h.astype(BF16)
    h_lo = (h - h_hi.astype(F32)).astype(BF16)
    w_hi = w.astype(BF16)
    w_lo = (w - w_hi.astype(F32)).astype(BF16)
    logits = (jnp.dot(h_hi, w_hi, preferred_element_type=F32) + jnp.dot(h_hi, w_lo, preferred_element_type=F32)
              + jnp.dot(h_lo, w_hi, preferred_element_type=F32)) + b_ref[...]
    lane = lax.broadcasted_iota(jnp.int32, logits.shape, 1).astype(F32)
    work = logits
    sels, vals, firsts = [], [], []
    for _ in range(TOP_K):
        mx = jnp.max(work, axis=1, keepdims=True)
        first = jnp.min(jnp.where(work == mx, lane, float(N_EXPERTS)), axis=1, keepdims=True)
        sel = lane == first
        sels.append(sel)
        vals.append(mx)
        firsts.append(first)
        work = jnp.where(sel, -jnp.inf, work)
    exps = [jnp.exp(v - vals[0]) for v in vals]
    den = exps[0] + exps[1] + exps[2] + exps[3]

    onehot = jnp.zeros_like(logits)
    for sel in sels:
        onehot = onehot + jnp.where(sel, 1.0, 0.0)
    r = lax.broadcasted_iota(jnp.int32, (tm, tm), 0)
    c = lax.broadcasted_iota(jnp.int32, (tm, tm), 1)
    tri = jnp.where(r > c, 1.0, 0.0).astype(BF16)
    before = jnp.dot(tri, onehot.astype(BF16), preferred_element_type=F32) + base_ref[...]
    base_ref[...] = base_ref[...] + jnp.sum(onehot, axis=0, keepdims=True)
    cnt_ref[...] = base_ref[...]

    out_lane = lax.broadcasted_iota(jnp.int32, (tm, LANES), 1)
    ids = jnp.zeros((tm, LANES), F32)
    wts = jnp.zeros((tm, LANES), F32)
    rank = jnp.zeros((tm, LANES), F32)
    for kk in range(TOP_K):
        rk = jnp.sum(jnp.where(sels[kk], before, 0.0), axis=1, keepdims=True)
        ids = jnp.where(out_lane == kk, firsts[kk], ids)
        wts = jnp.where(out_lane == kk, exps[kk] / den, wts)
        rank = jnp.where(out_lane == kk, rk, rank)
    ids_ref[...] = ids.astype(jnp.int32)
    wts_ref[...] = wts
    rank_ref[...] = rank.astype(jnp.int32)


def _router(h, w, b):
    t, d = h.shape
    tm = _pick_tile(t, (512, 256))
    wide = lambda dt: jax.ShapeDtypeStruct((t, LANES), dt)
    row = pl.BlockSpec((tm, LANES), lambda i: (i, 0))
    return pl.pallas_call(
        _router_kernel,
        grid=(t // tm,),
        in_specs=[pl.BlockSpec((tm, d), lambda i: (i, 0)),
                  pl.BlockSpec((d, N_EXPERTS), lambda i: (0, 0)),
                  pl.BlockSpec((1, N_EXPERTS), lambda i: (0, 0))],
        out_specs=[row, row, row, pl.BlockSpec((1, N_EXPERTS), lambda i: (0, 0))],
        out_shape=[wide(jnp.int32), wide(F32), wide(jnp.int32), jax.ShapeDtypeStruct((1, N_EXPERTS), F32)],
        scratch_shapes=[pltpu.VMEM((1, N_EXPERTS), F32)],
        compiler_params=_cparams(("arbitrary",)),
        name="moe_router",
    )(h, w, b.reshape(1, N_EXPERTS))


MOE_TILE = 512
ROUTE_TOK = 256
ROW_TILE = (SUBLANES, LANES)
assert D_MODEL == SUBLANES * LANES
VMEM_LIMIT_EXPERTS = 58 * 1024 * 1024


def _rows_to_tiles(x):
    return x.reshape((x.shape[0],) + ROW_TILE)


def _tiles_to_rows(x):
    return x.reshape(x.shape[0], D_MODEL)


def _dispatch_kernel(last_ref, nu_ref, dest_ref, h_ref, xs_ref, zero_ref, stage_ref, zsem, sems):
    i = pl.program_id(0)
    n = pl.num_programs(0)
    n_tiles = xs_ref.shape[0] // MOE_TILE

    def fill(tile):
        return pltpu.make_async_copy(zero_ref, xs_ref.at[pl.ds(tile * MOE_TILE, MOE_TILE)], zsem)

    @pl.when(i == 0)
    def _():
        zero_ref[...] = jnp.zeros_like(zero_ref)
        for e in range(N_EXPERTS):
            @pl.when(last_ref[e] >= 0)
            def _():
                fill(last_ref[e]).start()

        def start_unused(tile, carry):
            fill(tile).start()
            return carry

        def wait_unused(tile, carry):
            fill(tile).wait()
            return carry

        lax.fori_loop(nu_ref[0], n_tiles, start_unused, 0)
        for e in range(N_EXPERTS):
            @pl.when(last_ref[e] >= 0)
            def _():
                fill(last_ref[e]).wait()
        lax.fori_loop(nu_ref[0], n_tiles, wait_unused, 0)

    slot = lax.rem(i, 2)
    stage_ref[slot] = _rows_to_tiles(h_ref[...])

    def issue(r8, carry):
        base = pl.multiple_of(r8 * SUBLANES, SUBLANES)
        for rr in range(SUBLANES):
            for kk in range(TOP_K):
                d = dest_ref[(base + rr) * TOP_K + kk]
                pltpu.make_async_copy(stage_ref.at[slot, pl.ds(base + rr, 1)], xs_ref.at[pl.ds(d, 1)],
                                      sems.at[slot]).start(priority=kk % 2)
        return carry

    lax.fori_loop(0, ROUTE_TOK // SUBLANES, issue, 0)

    def drain(s):
        for _ in range(TOP_K):
            pltpu.make_async_copy(stage_ref.at[s], xs_ref.at[pl.ds(0, ROUTE_TOK)], sems.at[s]).wait()

    @pl.when(i > 0)
    def _():
        drain(1 - slot)

    @pl.when(i == n - 1)
    def _():
        drain(slot)


def _dispatch(h, dest_flat, last_tile, n_used, n_rows):
    t, d = h.shape
    return pl.pallas_call(
        _dispatch_kernel,
        grid_spec=pltpu.PrefetchScalarGridSpec(
            num_scalar_prefetch=2,
            grid=(t // ROUTE_TOK,),
            in_specs=[pl.BlockSpec((ROUTE_TOK * TOP_K,), lambda i, last, nu: (i,), memory_space=pltpu.SMEM),
                      pl.BlockSpec((ROUTE_TOK, d), lambda i, last, nu: (i, 0))],
            out_specs=pl.BlockSpec(memory_space=pl.ANY),
            scratch_shapes=[pltpu.VMEM((MOE_TILE,) + ROW_TILE, F32), pltpu.VMEM((2, ROUTE_TOK) + ROW_TILE, F32),
                            pltpu.SemaphoreType.DMA(()), pltpu.SemaphoreType.DMA((2,))]),
        out_shape=jax.ShapeDtypeStruct((n_rows,) + ROW_TILE, F32),
        compiler_params=_cparams(("arbitrary",)),
        name="moe_dispatch",
    )(last_tile, n_used, dest_flat, h)


def _experts_kernel(te_ref, nu_ref, xs_ref, wgu_ref, bgu_ref, wd_ref, bd_ref, y_ref, wgu_b_ref, wd_b_ref):
    i = pl.program_id(0)

    @pl.when(jnp.logical_or(i == 0, te_ref[i] != te_ref[jnp.maximum(i - 1, 0)]))
    def _():
        wgu_b_ref[...] = wgu_ref[0].astype(BF16)
        wd_b_ref[...] = wd_ref[0].astype(BF16)

    @pl.when(i >= nu_ref[0])
    def _():
        y_ref[...] = jnp.zeros_like(y_ref)

    @pl.when(i < nu_ref[0])
    def _():
        x = _tiles_to_rows(xs_ref[...]).astype(BF16)
        gu = jnp.dot(x, wgu_b_ref[...], preferred_element_type=F32) + bgu_ref[0]
        gate = jnp.minimum(gu[:, :D_FF], SWIGLU_LIMIT)
        up = jnp.clip(gu[:, D_FF:], -SWIGLU_LIMIT, SWIGLU_LIMIT)
        hid = (up + 1.0) * gate * jax.nn.sigmoid(SWIGLU_ALPHA * gate)
        y_ref[...] = _rows_to_tiles(jnp.dot(hid.astype(BF16), wd_b_ref[...], preferred_element_type=F32)
                                    + bd_ref[0])


def _experts(tile_expert, n_used, xs, wgu, bgu, wd, bd, *, layer):
    n_rows = xs.shape[0]
    d = D_MODEL
    n_exp = wgu.shape[1]
    rows = lambda i, te, nu: (jnp.minimum(i, nu[0] - 1), 0, 0)
    per_e = lambda i, te, nu: (te[i], 0, 0)
    per_le = lambda i, te, nu: (layer, te[i], 0, 0)
    return pl.pallas_call(
        _experts_kernel,
        grid_spec=pltpu.PrefetchScalarGridSpec(
            num_scalar_prefetch=2,
            grid=(n_rows // MOE_TILE,),
            in_specs=[pl.BlockSpec((MOE_TILE,) + ROW_TILE, rows),
                      pl.BlockSpec((None, 1, d, 2 * D_FF), per_le),
                      pl.BlockSpec((1, 1, 2 * D_FF), per_e),
                      pl.BlockSpec((None, 1, D_FF, d), per_le),
                      pl.BlockSpec((1, 1, d), per_e)],
            out_specs=pl.BlockSpec((MOE_TILE,) + ROW_TILE, lambda i, te, nu: (i, 0, 0)),
            scratch_shapes=[pltpu.VMEM((d, 2 * D_FF), BF16), pltpu.VMEM((D_FF, d), BF16)]),
        out_shape=jax.ShapeDtypeStruct((n_rows,) + ROW_TILE, F32),
        compiler_params=pltpu.CompilerParams(dimension_semantics=("arbitrary",),
                                             vmem_limit_bytes=VMEM_LIMIT_EXPERTS),
        name="moe_experts",
    )(tile_expert, n_used, xs, wgu, bgu.reshape(n_exp, 1, 2 * D_FF), wd, bd.reshape(n_exp, 1, d))


COMBINE_TOK = 128


def _combine_kernel(dcur_ref, dnext_ref, w_ref, h_ref, y_ref, lg_ref, lb_ref, h2_ref, h2b_ref,
                    ybuf_ref, sems, *, tiles_per_seq, front):
    i = pl.program_id(0)
    n = pl.num_programs(0)
    slot = lax.rem(i, 2)

    def issue(d_ref, s):
        def body(r8, carry):
            base = pl.multiple_of(r8 * SUBLANES, SUBLANES)
            for rr in range(SUBLANES):
                for kk in range(TOP_K):
                    d = d_ref[(base + rr) * TOP_K + kk]
                    pltpu.make_async_copy(y_ref.at[pl.ds(d, 1)], ybuf_ref.at[s, kk, pl.ds(base + rr, 1)],
                                          sems.at[s]).start(priority=kk % 2)
            return carry
        lax.fori_loop(0, COMBINE_TOK // SUBLANES, body, 0)

    @pl.when(i == 0)
    def _():
        issue(dcur_ref, 0)

    @pl.when(i + 1 < n)
    def _():
        issue(dnext_ref, 1 - slot)

    for kk in range(TOP_K):
        pltpu.make_async_copy(y_ref.at[pl.ds(0, COMBINE_TOK)], ybuf_ref.at[slot, kk], sems.at[slot]).wait()

    w = w_ref[...]
    acc = w[:, 0:1] * _tiles_to_rows(ybuf_ref[slot, 0])
    for kk in range(1, TOP_K):
        acc = acc + w[:, kk:kk + 1] * _tiles_to_rows(ybuf_ref[slot, kk])
    y = _layer_norm(DEEPNORM_ALPHA * h_ref[...] + acc, lg_ref[...], lb_ref[...])
    pos = _seq_pos(i, tiles_per_seq, COMBINE_TOK)
    y = jnp.where(pos >= front, y, 0.0)
    h2_ref[...] = y
    h2b_ref[...] = y.astype(BF16)


def _combine(dest_flat, wts, h, y, lg, lb, *, lp, front, drop_prefix=False):
    t, d = h.shape
    n = t // COMBINE_TOK
    blk = COMBINE_TOK * TOP_K
    row = lambda w: pl.BlockSpec((COMBINE_TOK, w), lambda i: (i, 0))
    const = pl.BlockSpec((1, d), lambda i: (0, 0))
    out_rows, out_f32 = t, row(d)
    if drop_prefix:
        tps = lp // COMBINE_TOK
        skip = (front + N_META) // COMBINE_TOK
        out_rows = t - (t // lp) * (front + N_META)
        out_f32 = pl.BlockSpec((COMBINE_TOK, d),
                               lambda i: ((i // tps) * (tps - skip) + jnp.maximum(i % tps - skip, 0), 0))
    return pl.pallas_call(
        functools.partial(_combine_kernel, tiles_per_seq=lp // COMBINE_TOK, front=front),
        grid=(n,),
        in_specs=[pl.BlockSpec((blk,), lambda i: (i,), memory_space=pltpu.SMEM),
                  pl.BlockSpec((blk,), lambda i: (jnp.minimum(i + 1, n - 1),), memory_space=pltpu.SMEM),
                  row(LANES), row(d), pl.BlockSpec(memory_space=pl.ANY), const, const],
        out_specs=[out_f32, row(d)],
        out_shape=[jax.ShapeDtypeStruct((out_rows, d), F32), jax.ShapeDtypeStruct((t, d), BF16)],
        scratch_shapes=[pltpu.VMEM((2, TOP_K, COMBINE_TOK) + ROW_TILE, F32), pltpu.SemaphoreType.DMA((2,))],
        compiler_params=_cparams(("arbitrary",)),
        name="moe_combine_ln2",
    )(dest_flat, dest_flat, wts, h, y, lg.reshape(1, d), lb.reshape(1, d))


def _routing_plan(ids, rank, counts, n_tiles):
    n_exp = counts.shape[0]
    padded = (counts + MOE_TILE - 1) // MOE_TILE * MOE_TILE
    ends = jnp.cumsum(padded)
    starts = ends - padded
    dest = (jnp.take(starts, ids) + rank).reshape(-1)
    n_used = (ends[-1] // MOE_TILE).reshape(1)
    tile_start = jnp.arange(n_tiles, dtype=jnp.int32) * MOE_TILE
    tile_expert = jnp.sum((tile_start[:, None] >= ends[None, :]).astype(jnp.int32), axis=1)
    last_used_expert = jnp.max(jnp.where(padded > 0, jnp.arange(n_exp, dtype=jnp.int32), 0))
    tile_expert = jnp.minimum(tile_expert, last_used_expert)
    last_tile = jnp.where(padded > 0, ends // MOE_TILE - 1, -1)
    return dest.astype(jnp.int32), tile_expert.astype(jnp.int32), n_used.astype(jnp.int32), last_tile.astype(jnp.int32)


def _rope_tables(lp, front):
    pos = (jnp.arange(lp, dtype=jnp.int32) - front).astype(F32)
    inv = ROPE_THETA ** (-jnp.arange(0, ATTN_HEAD_DIM, 2, dtype=F32) / ATTN_HEAD_DIM)
    ang = pos[:, None] * inv[None, :]
    ang = jnp.concatenate([ang, ang, ang, ang], axis=-1)
    return jnp.cos(ang), jnp.sin(ang)


def kernel(x, meta_tokens, ln_in_g, ln_in_b, w_in, ssm_lambda_re, ssm_lambda_im, ssm_log_dt, ssm_b_re, ssm_b_im, ssm_c_re, ssm_c_im, ssm_d, ssm_w_glu, ssm_w_out, conv_w, conv_w_out, attn_lambda_q1, attn_lambda_k1, attn_lambda_q2, attn_lambda_k2, attn_subln_g, attn_w_out, gate_w, gate_b, w_o, ln1_g, ln1_b, router_w, router_b, expert_w_gu, expert_b_gu, expert_w_down, expert_b_down, ln2_g, ln2_b):
    bsz, seq, d = x.shape
    assert d == D_MODEL and seq % ATT_BLOCK == 0
    lp = seq + ATT_BLOCK
    front = ATT_BLOCK - N_META
    t = bsz * lp

    meta = jnp.broadcast_to(meta_tokens.astype(x.dtype)[None], (bsz, N_META, d))
    hcat = jnp.concatenate([jnp.zeros((bsz, front, d), x.dtype), meta, x], axis=1).reshape(t, d)
    h, hb = _ln_in(hcat, ln_in_g, ln_in_b, lp=lp, front=front)
    cos, sin = _rope_tables(lp, front)

    s3 = SSM_WIDTH + 3 * CONV_WIDTH
    s5 = s3 + 2 * ATTN_WIDTH
    for l in range(DEPTH):
        w_in_b = w_in[l].astype(BF16)
        u, cb, z = _proj_sc(hb, w_in_b[:, :s3])
        q, k = _proj_qk(hb, w_in_b[:, s3:s5], cos, sin, lp=lp)
        vt = _proj_vt(hb, w_in_b[:, s5:].T)
        bdb, bdc, pw = _ssm_tables(ssm_lambda_re[l], ssm_lambda_im[l], ssm_log_dt[l], ssm_b_re[l], ssm_b_im[l],
                                   ssm_c_re[l], ssm_c_im[l])
        ys = _ssm_branch(u, bdb, bdc, pw, ssm_d[l].astype(F32), ssm_w_glu[l].astype(BF16),
                         ssm_w_out[l].astype(BF16), bsz=bsz, lp=lp)
        yc = _conv_branch(cb, z, conv_w[l].astype(F32), conv_w_out[l].astype(BF16))

        lam_init = 0.8 - 0.6 * math.exp(-0.3 * l)
        lam = (jnp.exp(jnp.sum(attn_lambda_q1[l].astype(F32) * attn_lambda_k1[l].astype(F32)))
               - jnp.exp(jnp.sum(attn_lambda_q2[l].astype(F32) * attn_lambda_k2[l].astype(F32)))
               + lam_init).reshape(1)
        g_col = jnp.broadcast_to(attn_subln_g[l].astype(F32)[:, None], (ATTN_VALUE_DIM, ATT_BLOCK))
        o = _attention(lam, q, k, vt, g_col, bsz=bsz, lp=lp, front=front, lam_init=lam_init)

        h, hb = _merge(h, hb, ys, yc, o, gate_w[l].astype(BF16), gate_b[l].astype(F32),
                       attn_w_out[l].astype(BF16), w_o[l].astype(BF16), ln1_g[l], ln1_b[l], lp=lp, front=front)

        ids, wts, rank, counts = _router(h, router_w[l].astype(F32), router_b[l].astype(F32))
        n_tiles = t * TOP_K // MOE_TILE + N_EXPERTS
        dest, tile_expert, n_used, last_tile = _routing_plan(ids[:, :TOP_K], rank[:, :TOP_K],
                                                             counts[0].astype(jnp.int32), n_tiles)
        xs = _dispatch(h, dest, last_tile, n_used, n_tiles * MOE_TILE)
        y = _experts(tile_expert, n_used, xs, expert_w_gu.astype(F32), expert_b_gu[l].astype(F32),
                     expert_w_down.astype(F32), expert_b_down[l].astype(F32), layer=l)
        h, hb = _combine(dest, wts, h, y, ln2_g[l], ln2_b[l], lp=lp, front=front, drop_prefix=(l == DEPTH - 1))

    return h.reshape(bsz, seq, d)
```

```python
import functools
import math

import jax
import jax.numpy as jnp
from jax import lax
from jax.experimental import pallas as pl
from jax.experimental.pallas import tpu as pltpu

F32 = jnp.float32
BF16 = jnp.bfloat16

D_MODEL = 1024
DEPTH = 2
CHUNK = 64
N_META = 16
SSM_WIDTH = 512
SSM_GROUP = 16
SSM_GROUPS = 32
SSM_STATE = 64
SSM_CH = SSM_GROUPS * SSM_STATE
CONV_WIDTH = 512
CONV_K = 3
ATTN_HEADS = 8
ATTN_HEAD_DIM = 64
ATTN_VALUE_DIM = 128
ATTN_WIDTH = 1024
ROPE_THETA = 10000.0
N_EXPERTS = 32
TOP_K = 4
D_FF = 1024
SWIGLU_LIMIT = 7.0
SWIGLU_ALPHA = 1.702
DEEPNORM_ALPHA = (2.0 * DEPTH) ** 0.25
LN_EPS = 1e-5
RMS_EPS = 1e-5
NEG_INF = -1e30

ATT_BLOCK = 256
LANES = 128
SUBLANES = 8
VMEM_LIMIT = 48 * 1024 * 1024


def _cparams(sem):
    return pltpu.CompilerParams(dimension_semantics=sem, vmem_limit_bytes=VMEM_LIMIT)


def _pick_tile(n, candidates):
    for c in candidates:
        if n % c == 0:
            return c
    raise ValueError(f"no tile for {n}")


def _layer_norm(x, g, b):
    mu = jnp.mean(x, axis=-1, keepdims=True)
    xc = x - mu
    var = jnp.mean(xc * xc, axis=-1, keepdims=True)
    return xc * lax.rsqrt(var + LN_EPS) * g + b


def _seq_pos(tile_idx, tiles_per_seq, tm):
    base = lax.rem(tile_idx, tiles_per_seq) * tm
    return base + lax.broadcasted_iota(jnp.int32, (tm, 1), 0)


def _ln_in_kernel(x_ref, meta_ref, g_ref, b_ref, h_ref, hb_ref, *, tiles_per_seq, front):
    d = x_ref.shape[1]
    first = lax.rem(pl.program_id(0), tiles_per_seq) == 0

    @pl.when(first)
    def _():
        y = jnp.concatenate([jnp.zeros((front, d), F32), _layer_norm(meta_ref[...], g_ref[...], b_ref[...])], axis=0)
        h_ref[...] = y
        hb_ref[...] = y.astype(BF16)

    @pl.when(jnp.logical_not(first))
    def _():
        y = _layer_norm(x_ref[...], g_ref[...], b_ref[...])
        h_ref[...] = y
        hb_ref[...] = y.astype(BF16)


def _ln_in(x2, meta, g, b, *, lp, front):
    d = x2.shape[1]
    tm = front + N_META
    tps = lp // tm
    bsz = x2.shape[0] // (lp - tm)
    t = bsz * lp
    frames = lambda i: ((i // tps) * (tps - 1) + jnp.maximum(i % tps - 1, 0), 0)
    return pl.pallas_call(
        functools.partial(_ln_in_kernel, tiles_per_seq=tps, front=front),
        grid=(t // tm,),
        in_specs=[pl.BlockSpec((tm, d), frames),
                  pl.BlockSpec((N_META, d), lambda i: (0, 0)),
                  pl.BlockSpec((1, d), lambda i: (0, 0)),
                  pl.BlockSpec((1, d), lambda i: (0, 0))],
        out_specs=[pl.BlockSpec((tm, d), lambda i: (i, 0)),
                   pl.BlockSpec((tm, d), lambda i: (i, 0))],
        out_shape=[jax.ShapeDtypeStruct((t, d), F32), jax.ShapeDtypeStruct((t, d), BF16)],
        compiler_params=_cparams(("parallel",)),
        name="ln_in",
    )(x2, meta, g.reshape(1, d), b.reshape(1, d))


def _proj_sc_kernel(x_ref, w_ref, u_ref, cb_ref, z_ref):
    acc = jnp.dot(x_ref[...], w_ref[...], preferred_element_type=F32)
    s0, s1, s2, s3 = SSM_WIDTH, SSM_WIDTH + CONV_WIDTH, SSM_WIDTH + 2 * CONV_WIDTH, SSM_WIDTH + 3 * CONV_WIDTH
    u_ref[...] = acc[:, :s0]
    cb_ref[...] = acc[:, s0:s1]
    z_ref[...] = acc[:, s1:s2] * acc[:, s2:s3]


def _proj_sc(hb, w):
    t, d = hb.shape
    n = w.shape[1]
    tm = _pick_tile(t, (704, 512, 256))
    out = jax.ShapeDtypeStruct((t, SSM_WIDTH), F32)
    ospec = pl.BlockSpec((tm, SSM_WIDTH), lambda i: (i, 0))
    return pl.pallas_call(
        _proj_sc_kernel,
        grid=(t // tm,),
        in_specs=[pl.BlockSpec((tm, d), lambda i: (i, 0)),
                  pl.BlockSpec((d, n), lambda i: (0, 0))],
        out_specs=[ospec, ospec, ospec],
        out_shape=[out, out, out],
        compiler_params=_cparams(("parallel",)),
        name="proj_ssm_conv",
    )(hb, w)


def _proj_qk_kernel(x_ref, w_ref, cos_ref, sin_ref, q_ref, k_ref):
    acc = jnp.dot(x_ref[...], w_ref[...], preferred_element_type=F32)
    cos = cos_ref[...]
    sin = sin_ref[...]
    lane = lax.broadcasted_iota(jnp.int32, (1, LANES), 1)
    low_half = lax.rem(lane, ATTN_HEAD_DIM) < ATTN_HEAD_DIM // 2
    nblk = ATTN_WIDTH // LANES
    for blk in range(2 * nblk):
        a = acc[:, blk * LANES:(blk + 1) * LANES]
        rot = jnp.where(low_half, -pltpu.roll(a, LANES - ATTN_HEAD_DIM // 2, 1),
                        pltpu.roll(a, ATTN_HEAD_DIM // 2, 1))
        r = a * cos + rot * sin
        if blk < nblk:
            q_ref[:, blk * LANES:(blk + 1) * LANES] = (r * (ATTN_HEAD_DIM ** -0.5 * math.log2(math.e))).astype(BF16)
        else:
            k_ref[:, (blk - nblk) * LANES:(blk - nblk + 1) * LANES] = r.astype(BF16)


def _proj_qk(hb, w, cos, sin, *, lp):
    t, d = hb.shape
    tm = _pick_tile(lp, (704, 512, 256))
    tps = lp // tm
    out = jax.ShapeDtypeStruct((t, ATTN_WIDTH), BF16)
    return pl.pallas_call(
        _proj_qk_kernel,
        grid=(t // tm,),
        in_specs=[pl.BlockSpec((tm, d), lambda i: (i, 0)),
                  pl.BlockSpec((d, 2 * ATTN_WIDTH), lambda i: (0, 0)),
                  pl.BlockSpec((tm, LANES), lambda i: (i % tps, 0)),
                  pl.BlockSpec((tm, LANES), lambda i: (i % tps, 0))],
        out_specs=[pl.BlockSpec((tm, ATTN_WIDTH), lambda i: (i, 0)),
                   pl.BlockSpec((tm, ATTN_WIDTH), lambda i: (i, 0))],
        out_shape=[out, out],
        compiler_params=_cparams(("parallel",)),
        name="proj_qk_rope",
    )(hb, w, cos, sin)


def _proj_vt_kernel(wt_ref, x_ref, o_ref):
    vt = lax.dot_general(wt_ref[...], x_ref[...], (((1,), (1,)), ((), ())),
                         preferred_element_type=F32).astype(BF16)
    vd = ATTN_VALUE_DIM
    row = lax.broadcasted_iota(jnp.int32, (ATT_VROWS - vd, ATT_BLOCK), 0)
    tail = jnp.where(row == 0, 1.0, 0.0).astype(BF16)
    for c in range(o_ref.shape[0]):
        for h in range(ATTN_HEADS):
            o_ref[c, h, :vd, :] = vt[h * vd:(h + 1) * vd, c * ATT_BLOCK:(c + 1) * ATT_BLOCK]
            o_ref[c, h, vd:, :] = tail


def _proj_vt(hb, wt):
    t, d = hb.shape
    tm = _pick_tile(t, (512, 256))
    per = tm // ATT_BLOCK
    return pl.pallas_call(
        _proj_vt_kernel,
        grid=(t // tm,),
        in_specs=[pl.BlockSpec((ATTN_WIDTH, d), lambda i: (0, 0)),
                  pl.BlockSpec((tm, d), lambda i: (i, 0))],
        out_specs=pl.BlockSpec((per, ATTN_HEADS, ATT_VROWS, ATT_BLOCK), lambda i: (i, 0, 0, 0)),
        out_shape=jax.ShapeDtypeStruct((t // ATT_BLOCK, ATTN_HEADS, ATT_VROWS, ATT_BLOCK), BF16),
        compiler_params=_cparams(("parallel",)),
        name="proj_v_t",
    )(wt, hb)


SCAN_STRIP = 512
SSM_SUPER = 2


def _ssm_kernel(u_ref, bdb_ref, bdc_ref, pw_ref, dskip_ref, wglu_ref, wout_ref, y_ref, bu_ref, carry_ref):
    tm = u_ref.shape[0]

    @pl.when(pl.program_id(1) == 0)
    def _():
        carry_ref[...] = jnp.zeros_like(carry_ref)

    u = u_ref[...]
    ub = u.astype(BF16)
    sgw = SSM_WIDTH // SSM_SUPER
    sgc = SSM_CH // SSM_SUPER
    for sg in range(SSM_SUPER):
        bu = jnp.dot(ub[:, sg * sgw:(sg + 1) * sgw], bdb_ref[sg], preferred_element_type=F32)
        bu_ref[:, sg * sgc:(sg + 1) * sgc] = bu[:, :sgc]
        bu_ref[:, SSM_CH + sg * sgc:SSM_CH + (sg + 1) * sgc] = bu[:, sgc:]

    for s in range(SSM_CH // SCAN_STRIP):
        re = slice(s * SCAN_STRIP, (s + 1) * SCAN_STRIP)
        im = slice(SSM_CH + s * SCAN_STRIP, SSM_CH + (s + 1) * SCAN_STRIP)
        steps = []
        for k, d in enumerate((1, 2, 4)):
            rows = slice(k * SUBLANES, (k + 1) * SUBLANES)
            steps.append((d, pw_ref[rows, re], pw_ref[rows, im]))
        pr = pw_ref[3 * SUBLANES:4 * SUBLANES, re]
        pi = pw_ref[3 * SUBLANES:4 * SUBLANES, im]

        def group(gi, carry):
            cr, ci = carry
            r0 = pl.multiple_of(gi * SUBLANES, SUBLANES)
            xr = bu_ref[pl.ds(r0, SUBLANES), re]
            xi = bu_ref[pl.ds(r0, SUBLANES), im]
            for d, ar, ai in steps:
                sr = pltpu.roll(xr, d, 0)
                si = pltpu.roll(xi, d, 0)
                xr, xi = xr + ar * sr - ai * si, xi + ar * si + ai * sr
            xr, xi = xr + pr * cr - pi * ci, xi + pr * ci + pi * cr
            bu_ref[pl.ds(r0, SUBLANES), re] = xr
            bu_ref[pl.ds(r0, SUBLANES), im] = xi
            last_r = jnp.broadcast_to(xr[SUBLANES - 1:SUBLANES, :], (SUBLANES, SCAN_STRIP))
            last_i = jnp.broadcast_to(xi[SUBLANES - 1:SUBLANES, :], (SUBLANES, SCAN_STRIP))
            return last_r, last_i

        cr, ci = lax.fori_loop(0, tm // SUBLANES, group, (carry_ref[:, re], carry_ref[:, im]))
        carry_ref[:, re] = cr
        carry_ref[:, im] = ci

    ys = []
    for sg in range(SSM_SUPER):
        xr = bu_ref[:, sg * sgc:(sg + 1) * sgc].astype(BF16)
        xi = bu_ref[:, SSM_CH + sg * sgc:SSM_CH + (sg + 1) * sgc].astype(BF16)
        ys.append(jnp.dot(xr, bdc_ref[sg, 0], preferred_element_type=F32)
                  + jnp.dot(xi, bdc_ref[sg, 1], preferred_element_type=F32))
    y = jnp.concatenate(ys, axis=1) + dskip_ref[...] * u
    y = jax.nn.gelu(y)
    y = y * jax.nn.sigmoid(jnp.dot(y.astype(BF16), wglu_ref[...], preferred_element_type=F32))
    y_ref[...] = jnp.dot(y.astype(BF16), wout_ref[...], preferred_element_type=F32).astype(BF16)


def _ssm_branch(u, bdb, bdc, pw, dskip, wglu, wout, *, bsz, lp):
    t = u.shape[0]
    tm = _pick_tile(lp, (256,))
    tps = lp // tm
    const = lambda shape: pl.BlockSpec(shape, lambda b, i: (0,) * len(shape))
    return pl.pallas_call(
        _ssm_kernel,
        grid=(bsz, tps),
        in_specs=[pl.BlockSpec((tm, SSM_WIDTH), lambda b, i: (b * tps + i, 0)),
                  const(bdb.shape), const(bdc.shape), const(pw.shape), const((1, SSM_WIDTH)),
                  const(wglu.shape), const(wout.shape)],
        out_specs=pl.BlockSpec((tm, D_MODEL), lambda b, i: (b * tps + i, 0)),
        out_shape=jax.ShapeDtypeStruct((t, D_MODEL), BF16),
        scratch_shapes=[pltpu.VMEM((tm, 2 * SSM_CH), F32), pltpu.VMEM((SUBLANES, 2 * SSM_CH), F32)],
        compiler_params=_cparams(("arbitrary", "arbitrary")),
        name="ssm_branch",
    )(u, bdb, bdc, pw, dskip.reshape(1, SSM_WIDTH), wglu, wout)


def _ssm_tables(lam_re, lam_im, log_dt, b_re, b_im, c_re, c_im):
    lr, li = lam_re.astype(F32), lam_im.astype(F32)
    dt = jnp.exp(log_dt.astype(F32))[:, None]
    mag = jnp.exp(lr * dt)
    ar = mag * jnp.cos(li * dt)
    ai = mag * jnp.sin(li * dt)
    denom = lr * lr + li * li
    nr, ni = ar - 1.0, ai
    coef_r = (nr * lr + ni * li) / denom
    coef_i = (ni * lr - nr * li) / denom
    br, bi = b_re.astype(F32), b_im.astype(F32)
    bbar_r = coef_r[..., None] * br - coef_i[..., None] * bi
    bbar_i = coef_r[..., None] * bi + coef_i[..., None] * br
    gps = SSM_GROUPS // SSM_SUPER
    eye = jnp.eye(gps, dtype=F32)
    sgw, sgc = SSM_WIDTH // SSM_SUPER, SSM_CH // SSM_SUPER
    split = lambda m: m.reshape((SSM_SUPER, gps) + m.shape[1:])
    bd_in = lambda m: jnp.einsum('sgpc,gh->sgchp', split(m), eye).reshape(SSM_SUPER, sgw, sgc)
    bdb = jnp.concatenate([bd_in(bbar_r), bd_in(bbar_i)], axis=2)
    bd_out = lambda m: jnp.einsum('sgcp,gh->sgphc', split(m), eye).reshape(SSM_SUPER, sgc, sgw)
    bdc = jnp.stack([bd_out(c_re.astype(F32)), -bd_out(c_im.astype(F32))], axis=1)
    a1 = (ar.reshape(-1), ai.reshape(-1))
    cmul = lambda x, y: (x[0] * y[0] - x[1] * y[1], x[0] * y[1] + x[1] * y[0])
    pows = [a1]
    for _ in range(SUBLANES - 1):
        pows.append(cmul(pows[-1], a1))
    row = jnp.arange(SUBLANES)[:, None]
    both = lambda p: jnp.concatenate(p)[None, :]
    tables = [jnp.where(row >= d, both(pows[d - 1]), 0.0) for d in (1, 2, 4)]
    tables.append(jnp.concatenate([both(p) for p in pows], axis=0))
    pw = jnp.concatenate(tables, axis=0)
    return bdb.astype(BF16), bdc.astype(BF16), pw


def _conv_kernel(cb_ref, z_ref, zprev_ref, w_ref, wout_ref, y_ref):
    tm = z_ref.shape[0]
    z = z_ref[...]
    zz = jnp.concatenate([zprev_ref[...], z], axis=0)
    w = w_ref[...]
    y = w[2:3, :] * z
    for j in range(CONV_K - 1):
        shift = CONV_K - 1 - j
        y = y + w[j:j + 1, :] * zz[SUBLANES - shift:SUBLANES - shift + tm, :]
    y = cb_ref[...] * y
    y_ref[...] = jnp.dot(y.astype(BF16), wout_ref[...], preferred_element_type=F32).astype(BF16)


def _conv_branch(cb, z, w, wout):
    t = z.shape[0]
    tm = _pick_tile(t, (512, 256))
    per = tm // SUBLANES
    return pl.pallas_call(
        _conv_kernel,
        grid=(t // tm,),
        in_specs=[pl.BlockSpec((tm, CONV_WIDTH), lambda i: (i, 0)),
                  pl.BlockSpec((tm, CONV_WIDTH), lambda i: (i, 0)),
                  pl.BlockSpec((SUBLANES, CONV_WIDTH), lambda i: (jnp.maximum(i * per - 1, 0), 0)),
                  pl.BlockSpec((SUBLANES, CONV_WIDTH), lambda i: (0, 0)),
                  pl.BlockSpec((CONV_WIDTH, D_MODEL), lambda i: (0, 0))],
        out_specs=pl.BlockSpec((tm, D_MODEL), lambda i: (i, 0)),
        out_shape=jax.ShapeDtypeStruct((t, D_MODEL), BF16),
        compiler_params=_cparams(("parallel",)),
        name="conv_branch",
    )(cb, z, z, jnp.pad(w, ((0, SUBLANES - CONV_K), (0, 0))), wout)


ATT_SUB = 3
ATT_UNROLL = 4
BF16_SUBLANES = 16
ATT_VROWS = ATTN_VALUE_DIM + BF16_SUBLANES


def _attn_kernel(lam_ref, q_ref, k_ref, vt_ref, g_ref, o_ref,
                 qlo_ref, qhi_ref, m_ref, acc_ref, s_ref, *, front, lam_init):
    sb = pl.program_id(2)
    blk = ATT_BLOCK
    real0 = front + N_META
    nt = (((1,), (1,)), ((), ()))

    q = q_ref[...]
    lane = lax.broadcasted_iota(jnp.int32, q.shape, 1)
    qlo_ref[...] = jnp.where(lane < ATTN_HEAD_DIM, q, jnp.zeros_like(q))
    qhi_ref[...] = jnp.where(lane >= ATTN_HEAD_DIM, q, jnp.zeros_like(q))
    m_ref[...] = jnp.full_like(m_ref, NEG_INF)
    acc_ref[...] = jnp.zeros_like(acc_ref)

    def qk(a, k):
        rows = slice(a * blk, (a + 1) * blk)
        return [lax.dot_general(k, qref[rows, :], nt, preferred_element_type=F32) for qref in (qlo_ref, qhi_ref)]

    def consume(a, j, scores, vt, masked, key_off=0):
        if masked:
            pk = j * blk + key_off + lax.broadcasted_iota(jnp.int32, (blk - key_off, 1), 0)
            pq = (sb * ATT_SUB + a) * blk + lax.broadcasted_iota(jnp.int32, (1, blk), 1)
            cid_k = jnp.where(pk < front, 1 << 30, jnp.where(pk < real0, 0, 1 + (pk - real0) // CHUNK))
            cid_q = jnp.where(pq < real0, 0, 1 + (pq - real0) // CHUNK)
            vis = cid_k <= cid_q
            scores = [jnp.where(vis, s, NEG_INF) for s in scores]
        for idx, s in enumerate(scores):
            m_old = m_ref[a, idx]
            m_new = jnp.maximum(m_old, jnp.max(s, axis=0, keepdims=True))
            alpha = jnp.exp2(m_old - m_new)
            p = jnp.exp2(s - m_new)
            m_ref[a, idx] = m_new
            acc_ref[a, idx] = alpha * acc_ref[a, idx] + jnp.dot(vt, p.astype(BF16), preferred_element_type=F32)

    def k_block(j):
        return k_ref[pl.ds(pl.multiple_of(j * blk, blk), blk), :]

    def kv_block(j):
        return k_block(j), vt_ref[j]

    def qk_all(k, buf):
        for a in range(ATT_SUB):
            for idx, s in enumerate(qk(a, k)):
                s_ref[buf, a, idx] = s

    def consume_all(j, buf, vt, masked):
        for a in range(ATT_SUB):
            consume(a, j, [s_ref[buf, a, 0], s_ref[buf, a, 1]], vt, masked)

    half = blk // 2
    assert front >= half

    @pl.when(sb > 0)
    def _():
        k = k_ref[half:blk, :]
        vt = vt_ref[0][:, half:]
        for a in range(ATT_SUB):
            for idx, s in enumerate(qk(a, k)):
                s_ref[0, a, idx, :half, :] = s
        for a in range(ATT_SUB):
            consume(a, 0, [s_ref[0, a, 0, :half, :], s_ref[0, a, 1, :half, :]], vt, True, key_off=half)

    n_full = jnp.maximum(sb * ATT_SUB - 1, 0)
    n_trips = n_full // ATT_UNROLL

    @pl.when(n_trips > 0)
    def _():
        qk_all(k_block(1), 0)

    def body(jj, carry):
        j = 1 + ATT_UNROLL * jj
        for u in range(ATT_UNROLL):
            qk_all(k_block(jnp.minimum(j + u + 1, n_full)), (u + 1) % 2)
            consume_all(j + u, u % 2, vt_ref[j + u], False)
        return carry

    lax.fori_loop(0, n_trips, body, 0)

    left0 = 1 + n_trips * ATT_UNROLL
    n_pairs = (n_full + 1 - left0) // 2

    @pl.when(n_pairs > 0)
    def _():
        qk_all(k_block(left0), 0)

    def leftover_pair(jj, carry):
        j = left0 + 2 * jj
        qk_all(k_block(j + 1), 1)
        consume_all(j, 0, vt_ref[j], False)
        qk_all(k_block(jnp.minimum(j + 2, n_full)), 0)
        consume_all(j + 1, 1, vt_ref[j + 1], False)
        return carry

    lax.fori_loop(0, n_pairs, leftover_pair, 0)

    def leftover(j, carry):
        k, vt = kv_block(j)
        qk_all(k, 0)
        consume_all(j, 0, vt, False)
        return carry

    lax.fori_loop(left0 + 2 * n_pairs, 1 + n_full, leftover, 0)

    band = [(dj, a) for dj in range(ATT_SUB) for a in range(dj, ATT_SUB)]
    assert len(band) <= 2 * ATT_SUB
    for n_pair, (dj, a) in enumerate(band):
        for idx, s in enumerate(qk(a, k_block(sb * ATT_SUB + dj))):
            s_ref[n_pair // ATT_SUB, n_pair % ATT_SUB, idx] = s
    for n_pair, (dj, a) in enumerate(band):
        j = sb * ATT_SUB + dj
        slot = (n_pair // ATT_SUB, n_pair % ATT_SUB)
        consume(a, j, [s_ref[slot[0], slot[1], 0], s_ref[slot[0], slot[1], 1]], vt_ref[j], True)

    lam = lam_ref[0]
    vd = ATTN_VALUE_DIM
    for a in range(ATT_SUB):
        o = (acc_ref[a, 0, :vd] / acc_ref[a, 0, vd:vd + 1]
             - lam * (acc_ref[a, 1, :vd] / acc_ref[a, 1, vd:vd + 1]))
        o = o * lax.rsqrt(jnp.mean(o * o, axis=0, keepdims=True) + RMS_EPS)
        o = o * g_ref[...] * (1.0 - lam_init)
        o_ref[a * blk:(a + 1) * blk, :] = o.T.astype(BF16)


def _attention(lam, q, k, vt4, g_col, *, bsz, lp, front, lam_init):
    t = q.shape[0]
    blk = ATT_BLOCK
    nb = lp // blk
    assert nb % ATT_SUB == 0 and ATT_UNROLL % 2 == 0
    nsb = nb // ATT_SUB
    tq = ATT_SUB * blk
    return pl.pallas_call(
        functools.partial(_attn_kernel, front=front, lam_init=lam_init),
        grid=(bsz, ATTN_HEADS, nsb),
        in_specs=[pl.BlockSpec(memory_space=pltpu.SMEM),
                  pl.BlockSpec((tq, LANES), lambda b, h, i: (b * nsb + i, h)),
                  pl.BlockSpec((lp, LANES), lambda b, h, i: (b, h)),
                  pl.BlockSpec((nb, None, ATT_VROWS, blk), lambda b, h, i: (b, h, 0, 0)),
                  pl.BlockSpec((LANES, blk), lambda b, h, i: (0, 0))],
        out_specs=pl.BlockSpec((tq, LANES), lambda b, h, i: (b * nsb + i, h)),
        out_shape=jax.ShapeDtypeStruct((t, ATTN_WIDTH), BF16),
        scratch_shapes=[pltpu.VMEM((tq, LANES), BF16), pltpu.VMEM((tq, LANES), BF16),
                        pltpu.VMEM((ATT_SUB, 2, 1, blk), F32),
                        pltpu.VMEM((ATT_SUB, 2, ATT_VROWS, blk), F32),
                        pltpu.VMEM((2, ATT_SUB, 2, blk, blk), F32)],
        compiler_params=_cparams(("parallel", "parallel", "arbitrary")),
        name="diff_attention",
    )(lam, q, k, vt4, g_col)


def _merge_kernel(h_ref, hb_ref, ys_ref, yc_ref, o_ref, wg_ref, bg_ref, wao_ref, wo_ref, lg_ref, lb_ref,
                  h1_ref, h1b_ref, *, tiles_per_seq, front):
    tm = h_ref.shape[0]
    ya = jnp.dot(o_ref[...], wao_ref[...], preferred_element_type=F32)
    g = jax.nn.sigmoid(jnp.dot(hb_ref[...], wg_ref[...], preferred_element_type=F32) + bg_ref[...])
    merged = g[:, :D_MODEL] * ys_ref[...] + g[:, D_MODEL:2 * D_MODEL] * yc_ref[...] + g[:, 2 * D_MODEL:] * ya
    r = DEEPNORM_ALPHA * h_ref[...] + jnp.dot(merged.astype(BF16), wo_ref[...], preferred_element_type=F32)
    y = _layer_norm(r, lg_ref[...], lb_ref[...])
    pos = _seq_pos(pl.program_id(0), tiles_per_seq, tm)
    y = jnp.where(pos >= front, y, 0.0)
    h1_ref[...] = y
    h1b_ref[...] = y.astype(BF16)


def _merge(h, hb, ys, yc, o, wg, bg, wao, wo, lg, lb, *, lp, front):
    t, d = h.shape
    tm = _pick_tile(lp, (256,))
    row = lambda w: pl.BlockSpec((tm, w), lambda i: (i, 0))
    const = lambda a, b: pl.BlockSpec((a, b), lambda i: (0, 0))
    return pl.pallas_call(
        functools.partial(_merge_kernel, tiles_per_seq=lp // tm, front=front),
        grid=(t // tm,),
        in_specs=[row(d), row(d), row(d), row(d), row(ATTN_WIDTH),
                  const(d, 3 * d), const(1, 3 * d), const(ATTN_WIDTH, d), const(d, d), const(1, d), const(1, d)],
        out_specs=[row(d), row(d)],
        out_shape=[jax.ShapeDtypeStruct((t, d), F32), jax.ShapeDtypeStruct((t, d), BF16)],
        compiler_params=_cparams(("parallel",)),
        name="merge_ln1",
    )(h, hb, ys, yc, o, wg, bg.reshape(1, 3 * d), wao, wo, lg.reshape(1, d), lb.reshape(1, d))


def _router_kernel(h_ref, w_ref, b_ref, ids_ref, wts_ref, rank_ref, cnt_ref, base_ref):
    tm = h_ref.shape[0]

    @pl.when(pl.program_id(0) == 0)
    def _():
        base_ref[...] = jnp.zeros_like(base_ref)

    h = h_ref[...]
    w = w_ref[...]
    h_hi = h.astype(BF16)
    h_lo = (h - h_hi.astype(F32)).astype(BF16)
    w_hi = w.astype(BF16)
    w_lo = (w - w_hi.astype(F32)).astype(BF16)
    logits = (jnp.dot(h_hi, w_hi, preferred_element_type=F32) + jnp.dot(h_hi, w_lo, preferred_element_type=F32)
              + jnp.dot(h_lo, w_hi, preferred_element_type=F32)) + b_ref[...]
    lane = lax.broadcasted_iota(jnp.int32, logits.shape, 1).astype(F32)
    work = logits
    sels, vals, firsts = [], [], []
    for _ in range(TOP_K):
        mx = jnp.max(work, axis=1, keepdims=True)
        first = jnp.min(jnp.where(work == mx, lane, float(N_EXPERTS)), axis=1, keepdims=True)
        sel = lane == first
        sels.append(sel)
        vals.append(mx)
        firsts.append(first)
        work = jnp.where(sel, -jnp.inf, work)
    exps = [jnp.exp(v - vals[0]) for v in vals]
    den = exps[0] + exps[1] + exps[2] + exps[3]

    onehot = jnp.zeros_like(logits)
    for sel in sels:
        onehot = onehot + jnp.where(sel, 1.0, 0.0)
    r = lax.broadcasted_iota(jnp.int32, (tm, tm), 0)
    c = lax.broadcasted_iota(jnp.int32, (tm, tm), 1)
    tri = jnp.where(r > c, 1.0, 0.0).astype(BF16)
    before = jnp.dot(tri, onehot.astype(BF16), preferred_element_type=F32) + base_ref[...]
    base_ref[...] = base_ref[...] + jnp.sum(onehot, axis=0, keepdims=True)
    cnt_ref[...] = base_ref[...]

    out_lane = lax.broadcasted_iota(jnp.int32, (tm, LANES), 1)
    ids = jnp.zeros((tm, LANES), F32)
    wts = jnp.zeros((tm, LANES), F32)
    rank = jnp.zeros((tm, LANES), F32)
    for kk in range(TOP_K):
        rk = jnp.sum(jnp.where(sels[kk], before, 0.0), axis=1, keepdims=True)
        ids = jnp.where(out_lane == kk, firsts[kk], ids)
        wts = jnp.where(out_lane == kk, exps[kk] / den, wts)
        rank = jnp.where(out_lane == kk, rk, rank)
    ids_ref[...] = ids.astype(jnp.int32)
    wts_ref[...] = wts
    rank_ref[...] = rank.astype(jnp.int32)


def _router(h, w, b):
    t, d = h.shape
    tm = _pick_tile(t, (512, 256))
    wide = lambda dt: jax.ShapeDtypeStruct((t, LANES), dt)
    row = pl.BlockSpec((tm, LANES), lambda i: (i, 0))
    return pl.pallas_call(
        _router_kernel,
        grid=(t // tm,),
        in_specs=[pl.BlockSpec((tm, d), lambda i: (i, 0)),
                  pl.BlockSpec((d, N_EXPERTS), lambda i: (0, 0)),
                  pl.BlockSpec((1, N_EXPERTS), lambda i: (0, 0))],
        out_specs=[row, row, row, pl.BlockSpec((1, N_EXPERTS), lambda i: (0, 0))],
        out_shape=[wide(jnp.int32), wide(F32), wide(jnp.int32), jax.ShapeDtypeStruct((1, N_EXPERTS), F32)],
        scratch_shapes=[pltpu.VMEM((1, N_EXPERTS), F32)],
        compiler_params=_cparams(("arbitrary",)),
        name="moe_router",
    )(h, w, b.reshape(1, N_EXPERTS))


MOE_TILE = 512
ROUTE_TOK = 256
ROW_TILE = (SUBLANES, LANES)
assert D_MODEL == SUBLANES * LANES
VMEM_LIMIT_EXPERTS = 58 * 1024 * 1024


def _rows_to_tiles(x):
    return x.reshape((x.shape[0],) + ROW_TILE)


def _tiles_to_rows(x):
    return x.reshape(x.shape[0], D_MODEL)


def _dispatch_kernel(last_ref, nu_ref, dest_ref, h_ref, xs_ref, zero_ref, stage_ref, zsem, sems):
    i = pl.program_id(0)
    n = pl.num_programs(0)
    n_tiles = xs_ref.shape[0] // MOE_TILE

    def fill(tile):
        return pltpu.make_async_copy(zero_ref, xs_ref.at[pl.ds(tile * MOE_TILE, MOE_TILE)], zsem)

    @pl.when(i == 0)
    def _():
        zero_ref[...] = jnp.zeros_like(zero_ref)
        for e in range(N_EXPERTS):
            @pl.when(last_ref[e] >= 0)
            def _():
                fill(last_ref[e]).start()

        def start_unused(tile, carry):
            fill(tile).start()
            return carry

        def wait_unused(tile, carry):
            fill(tile).wait()
            return carry

        lax.fori_loop(nu_ref[0], n_tiles, start_unused, 0)
        for e in range(N_EXPERTS):
            @pl.when(last_ref[e] >= 0)
            def _():
                fill(last_ref[e]).wait()
        lax.fori_loop(nu_ref[0], n_tiles, wait_unused, 0)

    slot = lax.rem(i, 2)
    stage_ref[slot] = _rows_to_tiles(h_ref[...])

    def issue(r8, carry):
        base = pl.multiple_of(r8 * SUBLANES, SUBLANES)
        for rr in range(SUBLANES):
            for kk in range(TOP_K):
                d = dest_ref[(base + rr) * TOP_K + kk]
                pltpu.make_async_copy(stage_ref.at[slot, pl.ds(base + rr, 1)], xs_ref.at[pl.ds(d, 1)],
                                      sems.at[slot]).start(priority=kk % 2)
        return carry

    lax.fori_loop(0, ROUTE_TOK // SUBLANES, issue, 0)

    def drain(s):
        for _ in range(TOP_K):
            pltpu.make_async_copy(stage_ref.at[s], xs_ref.at[pl.ds(0, ROUTE_TOK)], sems.at[s]).wait()

    @pl.when(i > 0)
    def _():
        drain(1 - slot)

    @pl.when(i == n - 1)
    def _():
        drain(slot)


def _dispatch(h, dest_flat, last_tile, n_used, n_rows):
    t, d = h.shape
    return pl.pallas_call(
        _dispatch_kernel,
        grid_spec=pltpu.PrefetchScalarGridSpec(
            num_scalar_prefetch=2,
            grid=(t // ROUTE_TOK,),
            in_specs=[pl.BlockSpec((ROUTE_TOK * TOP_K,), lambda i, last, nu: (i,), memory_space=pltpu.SMEM),
                      pl.BlockSpec((ROUTE_TOK, d), lambda i, last, nu: (i, 0))],
            out_specs=pl.BlockSpec(memory_space=pl.ANY),
            scratch_shapes=[pltpu.VMEM((MOE_TILE,) + ROW_TILE, F32), pltpu.VMEM((2, ROUTE_TOK) + ROW_TILE, F32),
                            pltpu.SemaphoreType.DMA(()), pltpu.SemaphoreType.DMA((2,))]),
        out_shape=jax.ShapeDtypeStruct((n_rows,) + ROW_TILE, F32),
        compiler_params=_cparams(("arbitrary",)),
        name="moe_dispatch",
    )(last_tile, n_used, dest_flat, h)


def _experts_kernel(te_ref, nu_ref, xs_ref, wgu_ref, bgu_ref, wd_ref, bd_ref, y_ref, wgu_b_ref, wd_b_ref):
    i = pl.program_id(0)

    @pl.when(jnp.logical_or(i == 0, te_ref[i] != te_ref[jnp.maximum(i - 1, 0)]))
    def _():
        wgu_b_ref[...] = wgu_ref[0].astype(BF16)
        wd_b_ref[...] = wd_ref[0].astype(BF16)

    @pl.when(i >= nu_ref[0])
    def _():
        y_ref[...] = jnp.zeros_like(y_ref)

    @pl.when(i < nu_ref[0])
    def _():
        x = _tiles_to_rows(xs_ref[...]).astype(BF16)
        gu = jnp.dot(x, wgu_b_ref[...], preferred_element_type=F32) + bgu_ref[0]
        gate = jnp.minimum(gu[:, :D_FF], SWIGLU_LIMIT)
        up = jnp.clip(gu[:, D_FF:], -SWIGLU_LIMIT, SWIGLU_LIMIT)
        hid = (up + 1.0) * gate * jax.nn.sigmoid(SWIGLU_ALPHA * gate)
        y_ref[...] = _rows_to_tiles(jnp.dot(hid.astype(BF16), wd_b_ref[...], preferred_element_type=F32)
                                    + bd_ref[0])


def _experts(tile_expert, n_used, xs, wgu, bgu, wd, bd, *, layer):
    n_rows = xs.shape[0]
    d = D_MODEL
    n_exp = wgu.shape[1]
    rows = lambda i, te, nu: (jnp.minimum(i, nu[0] - 1), 0, 0)
    per_e = lambda i, te, nu: (te[i], 0, 0)
    per_le = lambda i, te, nu: (layer, te[i], 0, 0)
    return pl.pallas_call(
        _experts_kernel,
        grid_spec=pltpu.PrefetchScalarGridSpec(
            num_scalar_prefetch=2,
            grid=(n_rows // MOE_TILE,),
            in_specs=[pl.BlockSpec((MOE_TILE,) + ROW_TILE, rows),
                      pl.BlockSpec((None, 1, d, 2 * D_FF), per_le),
                      pl.BlockSpec((1, 1, 2 * D_FF), per_e),
                      pl.BlockSpec((None, 1, D_FF, d), per_le),
                      pl.BlockSpec((1, 1, d), per_e)],
            out_specs=pl.BlockSpec((MOE_TILE,) + ROW_TILE, lambda i, te, nu: (i, 0, 0)),
            scratch_shapes=[pltpu.VMEM((d, 2 * D_FF), BF16), pltpu.VMEM((D_FF, d), BF16)]),
        out_shape=jax.ShapeDtypeStruct((n_rows,) + ROW_TILE, F32),
        compiler_params=pltpu.CompilerParams(dimension_semantics=("arbitrary",),
                                             vmem_limit_bytes=VMEM_LIMIT_EXPERTS),
        name="moe_experts",
    )(tile_expert, n_used, xs, wgu, bgu.reshape(n_exp, 1, 2 * D_FF), wd, bd.reshape(n_exp, 1, d))


COMBINE_TOK = 128


def _combine_kernel(dcur_ref, dnext_ref, w_ref, h_ref, y_ref, lg_ref, lb_ref, h2_ref, h2b_ref,
                    ybuf_ref, sems, *, tiles_per_seq, front):
    i = pl.program_id(0)
    n = pl.num_programs(0)
    slot = lax.rem(i, 2)

    def issue(d_ref, s):
        def body(r8, carry):
            base = pl.multiple_of(r8 * SUBLANES, SUBLANES)
            for rr in range(SUBLANES):
                for kk in range(TOP_K):
                    d = d_ref[(base + rr) * TOP_K + kk]
                    pltpu.make_async_copy(y_ref.at[pl.ds(d, 1)], ybuf_ref.at[s, kk, pl.ds(base + rr, 1)],
                                          sems.at[s]).start(priority=kk % 2)
            return carry
        lax.fori_loop(0, COMBINE_TOK // SUBLANES, body, 0)

    @pl.when(i == 0)
    def _():
        issue(dcur_ref, 0)

    @pl.when(i + 1 < n)
    def _():
        issue(dnext_ref, 1 - slot)

    for kk in range(TOP_K):
        pltpu.make_async_copy(y_ref.at[pl.ds(0, COMBINE_TOK)], ybuf_ref.at[slot, kk], sems.at[slot]).wait()

    w = w_ref[...]
    acc = w[:, 0:1] * _tiles_to_rows(ybuf_ref[slot, 0])
    for kk in range(1, TOP_K):
        acc = acc + w[:, kk:kk + 1] * _tiles_to_rows(ybuf_ref[slot, kk])
    y = _layer_norm(DEEPNORM_ALPHA * h_ref[...] + acc, lg_ref[...], lb_ref[...])
    pos = _seq_pos(i, tiles_per_seq, COMBINE_TOK)
    y = jnp.where(pos >= front, y, 0.0)
    h2_ref[...] = y
    h2b_ref[...] = y.astype(BF16)


def _combine(dest_flat, wts, h, y, lg, lb, *, lp, front, drop_prefix=False):
    t, d = h.shape
    n = t // COMBINE_TOK
    blk = COMBINE_TOK * TOP_K
    row = lambda w: pl.BlockSpec((COMBINE_TOK, w), lambda i: (i, 0))
    const = pl.BlockSpec((1, d), lambda i: (0, 0))
    out_rows, out_f32 = t, row(d)
    if drop_prefix:
        tps = lp // COMBINE_TOK
        skip = (front + N_META) // COMBINE_TOK
        out_rows = t - (t // lp) * (front + N_META)
        out_f32 = pl.BlockSpec((COMBINE_TOK, d),
                               lambda i: ((i // tps) * (tps - skip) + jnp.maximum(i % tps - skip, 0), 0))
    return pl.pallas_call(
        functools.partial(_combine_kernel, tiles_per_seq=lp // COMBINE_TOK, front=front),
        grid=(n,),
        in_specs=[pl.BlockSpec((blk,), lambda i: (i,), memory_space=pltpu.SMEM),
                  pl.BlockSpec((blk,), lambda i: (jnp.minimum(i + 1, n - 1),), memory_space=pltpu.SMEM),
                  row(LANES), row(d), pl.BlockSpec(memory_space=pl.ANY), const, const],
        out_specs=[out_f32, row(d)],
        out_shape=[jax.ShapeDtypeStruct((out_rows, d), F32), jax.ShapeDtypeStruct((t, d), BF16)],
        scratch_shapes=[pltpu.VMEM((2, TOP_K, COMBINE_TOK) + ROW_TILE, F32), pltpu.SemaphoreType.DMA((2,))],
        compiler_params=_cparams(("arbitrary",)),
        name="moe_combine_ln2",
    )(dest_flat, dest_flat, wts, h, y, lg.reshape(1, d), lb.reshape(1, d))


def _routing_plan(ids, rank, counts, n_tiles):
    n_exp = counts.shape[0]
    padded = (counts + MOE_TILE - 1) // MOE_TILE * MOE_TILE
    ends = jnp.cumsum(padded)
    starts = ends - padded
    dest = (jnp.take(starts, ids) + rank).reshape(-1)
    n_used = (ends[-1] // MOE_TILE).reshape(1)
    tile_start = jnp.arange(n_tiles, dtype=jnp.int32) * MOE_TILE
    tile_expert = jnp.sum((tile_start[:, None] >= ends[None, :]).astype(jnp.int32), axis=1)
    last_used_expert = jnp.max(jnp.where(padded > 0, jnp.arange(n_exp, dtype=jnp.int32), 0))
    tile_expert = jnp.minimum(tile_expert, last_used_expert)
    last_tile = jnp.where(padded > 0, ends // MOE_TILE - 1, -1)
    return dest.astype(jnp.int32), tile_expert.astype(jnp.int32), n_used.astype(jnp.int32), last_tile.astype(jnp.int32)


def _rope_tables(lp, front):
    pos = (jnp.arange(lp, dtype=jnp.int32) - front).astype(F32)
    inv = ROPE_THETA ** (-jnp.arange(0, ATTN_HEAD_DIM, 2, dtype=F32) / ATTN_HEAD_DIM)
    ang = pos[:, None] * inv[None, :]
    ang = jnp.concatenate([ang, ang, ang, ang], axis=-1)
    return jnp.cos(ang), jnp.sin(ang)


def kernel(x, meta_tokens, ln_in_g, ln_in_b, w_in, ssm_lambda_re, ssm_lambda_im, ssm_log_dt, ssm_b_re, ssm_b_im, ssm_c_re, ssm_c_im, ssm_d, ssm_w_glu, ssm_w_out, conv_w, conv_w_out, attn_lambda_q1, attn_lambda_k1, attn_lambda_q2, attn_lambda_k2, attn_subln_g, attn_w_out, gate_w, gate_b, w_o, ln1_g, ln1_b, router_w, router_b, expert_w_gu, expert_b_gu, expert_w_down, expert_b_down, ln2_g, ln2_b):
    bsz, seq, d = x.shape
    assert d == D_MODEL and seq % ATT_BLOCK == 0
    lp = seq + ATT_BLOCK
    front = ATT_BLOCK - N_META
    t = bsz * lp

    h, hb = _ln_in(x.reshape(bsz * seq, d), meta_tokens.astype(x.dtype), ln_in_g, ln_in_b, lp=lp, front=front)
    cos, sin = _rope_tables(lp, front)

    s3 = SSM_WIDTH + 3 * CONV_WIDTH
    s5 = s3 + 2 * ATTN_WIDTH
    for l in range(DEPTH):
        w_in_b = w_in[l].astype(BF16)
        u, cb, z = _proj_sc(hb, w_in_b[:, :s3])
        q, k = _proj_qk(hb, w_in_b[:, s3:s5], cos, sin, lp=lp)
        vt = _proj_vt(hb, w_in_b[:, s5:].T)
        bdb, bdc, pw = _ssm_tables(ssm_lambda_re[l], ssm_lambda_im[l], ssm_log_dt[l], ssm_b_re[l], ssm_b_im[l],
                                   ssm_c_re[l], ssm_c_im[l])
        ys = _ssm_branch(u, bdb, bdc, pw, ssm_d[l].astype(F32), ssm_w_glu[l].astype(BF16),
                         ssm_w_out[l].astype(BF16), bsz=bsz, lp=lp)
        yc = _conv_branch(cb, z, conv_w[l].astype(F32), conv_w_out[l].astype(BF16))

        lam_init = 0.8 - 0.6 * math.exp(-0.3 * l)
        lam = (jnp.exp(jnp.sum(attn_lambda_q1[l].astype(F32) * attn_lambda_k1[l].astype(F32)))
               - jnp.exp(jnp.sum(attn_lambda_q2[l].astype(F32) * attn_lambda_k2[l].astype(F32)))
               + lam_init).reshape(1)
        g_col = jnp.broadcast_to(attn_subln_g[l].astype(F32)[:, None], (ATTN_VALUE_DIM, ATT_BLOCK))
        o = _attention(lam, q, k, vt, g_col, bsz=bsz, lp=lp, front=front, lam_init=lam_init)

        h, hb = _merge(h, hb, ys, yc, o, gate_w[l].astype(BF16), gate_b[l].astype(F32),
                       attn_w_out[l].astype(BF16), w_o[l].astype(BF16), ln1_g[l], ln1_b[l], lp=lp, front=front)

        ids, wts, rank, counts = _router(h, router_w[l].astype(F32), router_b[l].astype(F32))
        n_tiles = t * TOP_K // MOE_TILE + N_EXPERTS
        dest, tile_expert, n_used, last_tile = _routing_plan(ids[:, :TOP_K], rank[:, :TOP_K],
                                                             counts[0].astype(jnp.int32), n_tiles)
        xs = _dispatch(h, dest, last_tile, n_used, n_tiles * MOE_TILE)
        y = _experts(tile_expert, n_used, xs, expert_w_gu.astype(F32), expert_b_gu[l].astype(F32),
                     expert_w_down.astype(F32), expert_b_down[l].astype(F32), layer=l)
        h, hb = _combine(dest, wts, h, y, ln2_g[l], ln2_b[l], lp=lp, front=front, drop_prefix=(l == DEPTH - 1))

    return h.reshape(bsz, seq, d)
```

```python
import functools
import math

import jax
import jax.numpy as jnp
from jax import lax
from jax.experimental import pallas as pl
from jax.experimental.pallas import tpu as pltpu

F32 = jnp.float32
BF16 = jnp.bfloat16

D_MODEL = 1024
DEPTH = 2
CHUNK = 64
N_META = 16
SSM_WIDTH = 512
SSM_GROUP = 16
SSM_GROUPS = 32
SSM_STATE = 64
SSM_CH = SSM_GROUPS * SSM_STATE
CONV_WIDTH = 512
CONV_K = 3
ATTN_HEADS = 8
ATTN_HEAD_DIM = 64
ATTN_VALUE_DIM = 128
ATTN_WIDTH = 1024
ROPE_THETA = 10000.0
N_EXPERTS = 32
TOP_K = 4
D_FF = 1024
SWIGLU_LIMIT = 7.0
SWIGLU_ALPHA = 1.702
DEEPNORM_ALPHA = (2.0 * DEPTH) ** 0.25
LN_EPS = 1e-5
RMS_EPS = 1e-5
NEG_INF = -1e30

ATT_BLOCK = 256
LANES = 128
SUBLANES = 8
VMEM_LIMIT = 48 * 1024 * 1024


def _cparams(sem):
    return pltpu.CompilerParams(dimension_semantics=sem, vmem_limit_bytes=VMEM_LIMIT)


def _pick_tile(n, candidates):
    for c in candidates:
        if n % c == 0:
            return c
    raise ValueError(f"no tile for {n}")


def _layer_norm(x, g, b):
    mu = jnp.mean(x, axis=-1, keepdims=True)
    xc = x - mu
    var = jnp.mean(xc * xc, axis=-1, keepdims=True)
    return xc * lax.rsqrt(var + LN_EPS) * g + b


def _seq_pos(tile_idx, tiles_per_seq, tm):
    base = lax.rem(tile_idx, tiles_per_seq) * tm
    return base + lax.broadcasted_iota(jnp.int32, (tm, 1), 0)


def _ln_in_kernel(x_ref, meta_ref, g_ref, b_ref, h_ref, hb_ref, *, tiles_per_seq, front):
    d = x_ref.shape[1]
    first = lax.rem(pl.program_id(0), tiles_per_seq) == 0

    @pl.when(first)
    def _():
        y = jnp.concatenate([jnp.zeros((front, d), F32), _layer_norm(meta_ref[...], g_ref[...], b_ref[...])], axis=0)
        h_ref[...] = y
        hb_ref[...] = y.astype(BF16)

    @pl.when(jnp.logical_not(first))
    def _():
        y = _layer_norm(x_ref[...], g_ref[...], b_ref[...])
        h_ref[...] = y
        hb_ref[...] = y.astype(BF16)


def _ln_in(x2, meta, g, b, *, lp, front):
    d = x2.shape[1]
    tm = front + N_META
    tps = lp // tm
    bsz = x2.shape[0] // (lp - tm)
    t = bsz * lp
    frames = lambda i: ((i // tps) * (tps - 1) + jnp.maximum(i % tps - 1, 0), 0)
    return pl.pallas_call(
        functools.partial(_ln_in_kernel, tiles_per_seq=tps, front=front),
        grid=(t // tm,),
        in_specs=[pl.BlockSpec((tm, d), frames),
                  pl.BlockSpec((N_META, d), lambda i: (0, 0)),
                  pl.BlockSpec((1, d), lambda i: (0, 0)),
                  pl.BlockSpec((1, d), lambda i: (0, 0))],
        out_specs=[pl.BlockSpec((tm, d), lambda i: (i, 0)),
                   pl.BlockSpec((tm, d), lambda i: (i, 0))],
        out_shape=[jax.ShapeDtypeStruct((t, d), F32), jax.ShapeDtypeStruct((t, d), BF16)],
        compiler_params=_cparams(("parallel",)),
        name="ln_in",
    )(x2, meta, g.reshape(1, d), b.reshape(1, d))


def _proj_sc_kernel(x_ref, w_ref, u_ref, cb_ref, z_ref):
    acc = jnp.dot(x_ref[...], w_ref[...], preferred_element_type=F32)
    s0, s1, s2, s3 = SSM_WIDTH, SSM_WIDTH + CONV_WIDTH, SSM_WIDTH + 2 * CONV_WIDTH, SSM_WIDTH + 3 * CONV_WIDTH
    u_ref[...] = acc[:, :s0]
    cb_ref[...] = acc[:, s0:s1]
    z_ref[...] = acc[:, s1:s2] * acc[:, s2:s3]


def _proj_sc(hb, w):
    t, d = hb.shape
    n = w.shape[1]
    tm = _pick_tile(t, (704, 512, 256))
    out = jax.ShapeDtypeStruct((t, SSM_WIDTH), F32)
    ospec = pl.BlockSpec((tm, SSM_WIDTH), lambda i: (i, 0))
    return pl.pallas_call(
        _proj_sc_kernel,
        grid=(t // tm,),
        in_specs=[pl.BlockSpec((tm, d), lambda i: (i, 0)),
                  pl.BlockSpec((d, n), lambda i: (0, 0))],
        out_specs=[ospec, ospec, ospec],
        out_shape=[out, out, out],
        compiler_params=_cparams(("parallel",)),
        name="proj_ssm_conv",
    )(hb, w)


def _proj_qk_kernel(x_ref, w_ref, cos_ref, sin_ref, q_ref, k_ref):
    acc = jnp.dot(x_ref[...], w_ref[...], preferred_element_type=F32)
    cos = cos_ref[...]
    sin = sin_ref[...]
    lane = lax.broadcasted_iota(jnp.int32, (1, LANES), 1)
    low_half = lax.rem(lane, ATTN_HEAD_DIM) < ATTN_HEAD_DIM // 2
    nblk = ATTN_WIDTH // LANES
    for blk in range(2 * nblk):
        a = acc[:, blk * LANES:(blk + 1) * LANES]
        rot = jnp.where(low_half, -pltpu.roll(a, LANES - ATTN_HEAD_DIM // 2, 1),
                        pltpu.roll(a, ATTN_HEAD_DIM // 2, 1))
        r = a * cos + rot * sin
        if blk < nblk:
            q_ref[:, blk * LANES:(blk + 1) * LANES] = (r * (ATTN_HEAD_DIM ** -0.5 * math.log2(math.e))).astype(BF16)
        else:
            k_ref[:, (blk - nblk) * LANES:(blk - nblk + 1) * LANES] = r.astype(BF16)


def _proj_qk(hb, w, cos, sin, *, lp):
    t, d = hb.shape
    tm = _pick_tile(lp, (704, 512, 256))
    tps = lp // tm
    out = jax.ShapeDtypeStruct((t, ATTN_WIDTH), BF16)
    return pl.pallas_call(
        _proj_qk_kernel,
        grid=(t // tm,),
        in_specs=[pl.BlockSpec((tm, d), lambda i: (i, 0)),
                  pl.BlockSpec((d, 2 * ATTN_WIDTH), lambda i: (0, 0)),
                  pl.BlockSpec((tm, LANES), lambda i: (i % tps, 0)),
                  pl.BlockSpec((tm, LANES), lambda i: (i % tps, 0))],
        out_specs=[pl.BlockSpec((tm, ATTN_WIDTH), lambda i: (i, 0)),
                   pl.BlockSpec((tm, ATTN_WIDTH), lambda i: (i, 0))],
        out_shape=[out, out],
        compiler_params=_cparams(("parallel",)),
        name="proj_qk_rope",
    )(hb, w, cos, sin)


def _proj_vt_kernel(wt_ref, x_ref, o_ref):
    vt = lax.dot_general(wt_ref[...], x_ref[...], (((1,), (1,)), ((), ())),
                         preferred_element_type=F32).astype(BF16)
    vd = ATTN_VALUE_DIM
    row = lax.broadcasted_iota(jnp.int32, (ATT_VROWS - vd, ATT_BLOCK), 0)
    tail = jnp.where(row == 0, 1.0, 0.0).astype(BF16)
    for c in range(o_ref.shape[0]):
        for h in range(ATTN_HEADS):
            o_ref[c, h, :vd, :] = vt[h * vd:(h + 1) * vd, c * ATT_BLOCK:(c + 1) * ATT_BLOCK]
            o_ref[c, h, vd:, :] = tail


def _proj_vt(hb, wt):
    t, d = hb.shape
    tm = _pick_tile(t, (512, 256))
    per = tm // ATT_BLOCK
    return pl.pallas_call(
        _proj_vt_kernel,
        grid=(t // tm,),
        in_specs=[pl.BlockSpec((ATTN_WIDTH, d), lambda i: (0, 0)),
                  pl.BlockSpec((tm, d), lambda i: (i, 0))],
        out_specs=pl.BlockSpec((per, ATTN_HEADS, ATT_VROWS, ATT_BLOCK), lambda i: (i, 0, 0, 0)),
        out_shape=jax.ShapeDtypeStruct((t // ATT_BLOCK, ATTN_HEADS, ATT_VROWS, ATT_BLOCK), BF16),
        compiler_params=_cparams(("parallel",)),
        name="proj_v_t",
    )(wt, hb)


SCAN_STRIP = 512
SSM_SUPER = 2


def _ssm_kernel(u_ref, bdb_ref, bdc_ref, pw_ref, dskip_ref, wglu_ref, wout_ref, y_ref, bu_ref, carry_ref):
    tm = u_ref.shape[0]

    @pl.when(pl.program_id(1) == 0)
    def _():
        carry_ref[...] = jnp.zeros_like(carry_ref)

    u = u_ref[...]
    ub = u.astype(BF16)
    sgw = SSM_WIDTH // SSM_SUPER
    sgc = SSM_CH // SSM_SUPER
    for sg in range(SSM_SUPER):
        bu = jnp.dot(ub[:, sg * sgw:(sg + 1) * sgw], bdb_ref[sg], preferred_element_type=F32)
        bu_ref[:, sg * sgc:(sg + 1) * sgc] = bu[:, :sgc]
        bu_ref[:, SSM_CH + sg * sgc:SSM_CH + (sg + 1) * sgc] = bu[:, sgc:]

    for s in range(SSM_CH // SCAN_STRIP):
        re = slice(s * SCAN_STRIP, (s + 1) * SCAN_STRIP)
        im = slice(SSM_CH + s * SCAN_STRIP, SSM_CH + (s + 1) * SCAN_STRIP)
        steps = []
        for k, d in enumerate((1, 2, 4)):
            rows = slice(k * SUBLANES, (k + 1) * SUBLANES)
            steps.append((d, pw_ref[rows, re], pw_ref[rows, im]))
        pr = pw_ref[3 * SUBLANES:4 * SUBLANES, re]
        pi = pw_ref[3 * SUBLANES:4 * SUBLANES, im]

        def group(gi, carry):
            cr, ci = carry
            r0 = pl.multiple_of(gi * SUBLANES, SUBLANES)
            xr = bu_ref[pl.ds(r0, SUBLANES), re]
            xi = bu_ref[pl.ds(r0, SUBLANES), im]
            for d, ar, ai in steps:
                sr = pltpu.roll(xr, d, 0)
                si = pltpu.roll(xi, d, 0)
                xr, xi = xr + ar * sr - ai * si, xi + ar * si + ai * sr
            xr, xi = xr + pr * cr - pi * ci, xi + pr * ci + pi * cr
            bu_ref[pl.ds(r0, SUBLANES), re] = xr
            bu_ref[pl.ds(r0, SUBLANES), im] = xi
            last_r = jnp.broadcast_to(xr[SUBLANES - 1:SUBLANES, :], (SUBLANES, SCAN_STRIP))
            last_i = jnp.broadcast_to(xi[SUBLANES - 1:SUBLANES, :], (SUBLANES, SCAN_STRIP))
            return last_r, last_i

        cr, ci = lax.fori_loop(0, tm // SUBLANES, group, (carry_ref[:, re], carry_ref[:, im]))
        carry_ref[:, re] = cr
        carry_ref[:, im] = ci

    ys = []
    for sg in range(SSM_SUPER):
        xr = bu_ref[:, sg * sgc:(sg + 1) * sgc].astype(BF16)
        xi = bu_ref[:, SSM_CH + sg * sgc:SSM_CH + (sg + 1) * sgc].astype(BF16)
        ys.append(jnp.dot(xr, bdc_ref[sg, 0], preferred_element_type=F32)
                  + jnp.dot(xi, bdc_ref[sg, 1], preferred_element_type=F32))
    y = jnp.concatenate(ys, axis=1) + dskip_ref[...] * u
    y = jax.nn.gelu(y)
    y = y * jax.nn.sigmoid(jnp.dot(y.astype(BF16), wglu_ref[...], preferred_element_type=F32))
    y_ref[...] = jnp.dot(y.astype(BF16), wout_ref[...], preferred_element_type=F32).astype(BF16)


def _ssm_branch(u, bdb, bdc, pw, dskip, wglu, wout, *, bsz, lp):
    t = u.shape[0]
    tm = _pick_tile(lp, (256,))
    tps = lp // tm
    const = lambda shape: pl.BlockSpec(shape, lambda b, i: (0,) * len(shape))
    return pl.pallas_call(
        _ssm_kernel,
        grid=(bsz, tps),
        in_specs=[pl.BlockSpec((tm, SSM_WIDTH), lambda b, i: (b * tps + i, 0)),
                  const(bdb.shape), const(bdc.shape), const(pw.shape), const((1, SSM_WIDTH)),
                  const(wglu.shape), const(wout.shape)],
        out_specs=pl.BlockSpec((tm, D_MODEL), lambda b, i: (b * tps + i, 0)),
        out_shape=jax.ShapeDtypeStruct((t, D_MODEL), BF16),
        scratch_shapes=[pltpu.VMEM((tm, 2 * SSM_CH), F32), pltpu.VMEM((SUBLANES, 2 * SSM_CH), F32)],
        compiler_params=_cparams(("arbitrary", "arbitrary")),
        name="ssm_branch",
    )(u, bdb, bdc, pw, dskip.reshape(1, SSM_WIDTH), wglu, wout)


def _ssm_tables(lam_re, lam_im, log_dt, b_re, b_im, c_re, c_im):
    lr, li = lam_re.astype(F32), lam_im.astype(F32)
    dt = jnp.exp(log_dt.astype(F32))[:, None]
    mag = jnp.exp(lr * dt)
    ar = mag * jnp.cos(li * dt)
    ai = mag * jnp.sin(li * dt)
    denom = lr * lr + li * li
    nr, ni = ar - 1.0, ai
    coef_r = (nr * lr + ni * li) / denom
    coef_i = (ni * lr - nr * li) / denom
    br, bi = b_re.astype(F32), b_im.astype(F32)
    bbar_r = coef_r[..., None] * br - coef_i[..., None] * bi
    bbar_i = coef_r[..., None] * bi + coef_i[..., None] * br
    gps = SSM_GROUPS // SSM_SUPER
    eye = jnp.eye(gps, dtype=F32)
    sgw, sgc = SSM_WIDTH // SSM_SUPER, SSM_CH // SSM_SUPER
    split = lambda m: m.reshape((SSM_SUPER, gps) + m.shape[1:])
    bd_in = lambda m: jnp.einsum('sgpc,gh->sgchp', split(m), eye).reshape(SSM_SUPER, sgw, sgc)
    bdb = jnp.concatenate([bd_in(bbar_r), bd_in(bbar_i)], axis=2)
    bd_out = lambda m: jnp.einsum('sgcp,gh->sgphc', split(m), eye).reshape(SSM_SUPER, sgc, sgw)
    bdc = jnp.stack([bd_out(c_re.astype(F32)), -bd_out(c_im.astype(F32))], axis=1)
    a1 = (ar.reshape(-1), ai.reshape(-1))
    cmul = lambda x, y: (x[0] * y[0] - x[1] * y[1], x[0] * y[1] + x[1] * y[0])
    pows = [a1]
    for _ in range(SUBLANES - 1):
        pows.append(cmul(pows[-1], a1))
    row = jnp.arange(SUBLANES)[:, None]
    both = lambda p: jnp.concatenate(p)[None, :]
    tables = [jnp.where(row >= d, both(pows[d - 1]), 0.0) for d in (1, 2, 4)]
    tables.append(jnp.concatenate([both(p) for p in pows], axis=0))
    pw = jnp.concatenate(tables, axis=0)
    return bdb.astype(BF16), bdc.astype(BF16), pw


def _conv_kernel(cb_ref, z_ref, zprev_ref, w_ref, wout_ref, y_ref):
    tm = z_ref.shape[0]
    z = z_ref[...]
    zz = jnp.concatenate([zprev_ref[...], z], axis=0)
    w = w_ref[...]
    y = w[2:3, :] * z
    for j in range(CONV_K - 1):
        shift = CONV_K - 1 - j
        y = y + w[j:j + 1, :] * zz[SUBLANES - shift:SUBLANES - shift + tm, :]
    y = cb_ref[...] * y
    y_ref[...] = jnp.dot(y.astype(BF16), wout_ref[...], preferred_element_type=F32).astype(BF16)


def _conv_branch(cb, z, w, wout):
    t = z.shape[0]
    tm = _pick_tile(t, (512, 256))
    per = tm // SUBLANES
    return pl.pallas_call(
        _conv_kernel,
        grid=(t // tm,),
        in_specs=[pl.BlockSpec((tm, CONV_WIDTH), lambda i: (i, 0)),
                  pl.BlockSpec((tm, CONV_WIDTH), lambda i: (i, 0)),
                  pl.BlockSpec((SUBLANES, CONV_WIDTH), lambda i: (jnp.maximum(i * per - 1, 0), 0)),
                  pl.BlockSpec((SUBLANES, CONV_WIDTH), lambda i: (0, 0)),
                  pl.BlockSpec((CONV_WIDTH, D_MODEL), lambda i: (0, 0))],
        out_specs=pl.BlockSpec((tm, D_MODEL), lambda i: (i, 0)),
        out_shape=jax.ShapeDtypeStruct((t, D_MODEL), BF16),
        compiler_params=_cparams(("parallel",)),
        name="conv_branch",
    )(cb, z, z, jnp.pad(w, ((0, SUBLANES - CONV_K), (0, 0))), wout)


ATT_SUB = 3
ATT_UNROLL = 4
BF16_SUBLANES = 16
ATT_VROWS = ATTN_VALUE_DIM + BF16_SUBLANES


def _attn_kernel(lam_ref, q_ref, k_ref, vt_ref, g_ref, o_ref,
                 qlo_ref, qhi_ref, m_ref, acc_ref, s_ref, *, front, lam_init):
    sb = pl.program_id(2)
    blk = ATT_BLOCK
    real0 = front + N_META
    nt = (((1,), (1,)), ((), ()))

    q = q_ref[...]
    lane = lax.broadcasted_iota(jnp.int32, q.shape, 1)
    qlo_ref[...] = jnp.where(lane < ATTN_HEAD_DIM, q, jnp.zeros_like(q))
    qhi_ref[...] = jnp.where(lane >= ATTN_HEAD_DIM, q, jnp.zeros_like(q))
    m_ref[...] = jnp.full_like(m_ref, NEG_INF)
    acc_ref[...] = jnp.zeros_like(acc_ref)

    def qk(a, k):
        rows = slice(a * blk, (a + 1) * blk)
        return [lax.dot_general(k, qref[rows, :], nt, preferred_element_type=F32) for qref in (qlo_ref, qhi_ref)]

    def consume(a, j, scores, vt, masked, key_off=0):
        if masked:
            pk = j * blk + key_off + lax.broadcasted_iota(jnp.int32, (blk - key_off, 1), 0)
            pq = (sb * ATT_SUB + a) * blk + lax.broadcasted_iota(jnp.int32, (1, blk), 1)
            cid_k = jnp.where(pk < front, 1 << 30, jnp.where(pk < real0, 0, 1 + (pk - real0) // CHUNK))
            cid_q = jnp.where(pq < real0, 0, 1 + (pq - real0) // CHUNK)
            vis = cid_k <= cid_q
            scores = [jnp.where(vis, s, NEG_INF) for s in scores]
        for idx, s in enumerate(scores):
            m_old = m_ref[a, idx]
            m_new = jnp.maximum(m_old, jnp.max(s, axis=0, keepdims=True))
            alpha = jnp.exp2(m_old - m_new)
            p = jnp.exp2(s - m_new)
            m_ref[a, idx] = m_new
            acc_ref[a, idx] = alpha * acc_ref[a, idx] + jnp.dot(vt, p.astype(BF16), preferred_element_type=F32)

    def k_block(j):
        return k_ref[pl.ds(pl.multiple_of(j * blk, blk), blk), :]

    def kv_block(j):
        return k_block(j), vt_ref[j]

    def qk_all(k, buf):
        for a in range(ATT_SUB):
            for idx, s in enumerate(qk(a, k)):
                s_ref[buf, a, idx] = s

    def consume_all(j, buf, vt, masked):
        for a in range(ATT_SUB):
            consume(a, j, [s_ref[buf, a, 0], s_ref[buf, a, 1]], vt, masked)

    half = blk // 2
    assert front >= half

    @pl.when(sb > 0)
    def _():
        k = k_ref[half:blk, :]
        vt = vt_ref[0][:, half:]
        for a in range(ATT_SUB):
            for idx, s in enumerate(qk(a, k)):
                s_ref[0, a, idx, :half, :] = s
        for a in range(ATT_SUB):
            consume(a, 0, [s_ref[0, a, 0, :half, :], s_ref[0, a, 1, :half, :]], vt, True, key_off=half)

    n_full = jnp.maximum(sb * ATT_SUB - 1, 0)
    n_trips = n_full // ATT_UNROLL

    @pl.when(n_trips > 0)
    def _():
        qk_all(k_block(1), 0)

    def body(jj, carry):
        j = 1 + ATT_UNROLL * jj
        for u in range(ATT_UNROLL):
            qk_all(k_block(jnp.minimum(j + u + 1, n_full)), (u + 1) % 2)
            consume_all(j + u, u % 2, vt_ref[j + u], False)
        return carry

    lax.fori_loop(0, n_trips, body, 0)

    left0 = 1 + n_trips * ATT_UNROLL
    n_pairs = (n_full + 1 - left0) // 2

    @pl.when(n_pairs > 0)
    def _():
        qk_all(k_block(left0), 0)

    def leftover_pair(jj, carry):
        j = left0 + 2 * jj
        qk_all(k_block(j + 1), 1)
        consume_all(j, 0, vt_ref[j], False)
        qk_all(k_block(jnp.minimum(j + 2, n_full)), 0)
        consume_all(j + 1, 1, vt_ref[j + 1], False)
        return carry

    lax.fori_loop(0, n_pairs, leftover_pair, 0)

    def leftover(j, carry):
        k, vt = kv_block(j)
        qk_all(k, 0)
        consume_all(j, 0, vt, False)
        return carry

    lax.fori_loop(left0 + 2 * n_pairs, 1 + n_full, leftover, 0)

    band = [(dj, a) for dj in range(ATT_SUB) for a in range(dj, ATT_SUB)]
    assert len(band) <= 2 * ATT_SUB
    for n_pair, (dj, a) in enumerate(band):
        for idx, s in enumerate(qk(a, k_block(sb * ATT_SUB + dj))):
            s_ref[n_pair // ATT_SUB, n_pair % ATT_SUB, idx] = s
    for n_pair, (dj, a) in enumerate(band):
        j = sb * ATT_SUB + dj
        slot = (n_pair // ATT_SUB, n_pair % ATT_SUB)
        consume(a, j, [s_ref[slot[0], slot[1], 0], s_ref[slot[0], slot[1], 1]], vt_ref[j], True)

    lam = lam_ref[0]
    vd = ATTN_VALUE_DIM
    for a in range(ATT_SUB):
        o = (acc_ref[a, 0, :vd] / acc_ref[a, 0, vd:vd + 1]
             - lam * (acc_ref[a, 1, :vd] / acc_ref[a, 1, vd:vd + 1]))
        o = o * lax.rsqrt(jnp.mean(o * o, axis=0, keepdims=True) + RMS_EPS)
        o = o * g_ref[...] * (1.0 - lam_init)
        o_ref[a * blk:(a + 1) * blk, :] = o.T.astype(BF16)


def _attention(lam, q, k, vt4, g_col, *, bsz, lp, front, lam_init):
    t = q.shape[0]
    blk = ATT_BLOCK
    nb = lp // blk
    assert nb % ATT_SUB == 0 and ATT_UNROLL % 2 == 0
    nsb = nb // ATT_SUB
    tq = ATT_SUB * blk
    return pl.pallas_call(
        functools.partial(_attn_kernel, front=front, lam_init=lam_init),
        grid=(bsz, ATTN_HEADS, nsb),
        in_specs=[pl.BlockSpec(memory_space=pltpu.SMEM),
                  pl.BlockSpec((tq, LANES), lambda b, h, i: (b * nsb + i, h)),
                  pl.BlockSpec((lp, LANES), lambda b, h, i: (b, h)),
                  pl.BlockSpec((nb, None, ATT_VROWS, blk), lambda b, h, i: (b, h, 0, 0)),
                  pl.BlockSpec((LANES, blk), lambda b, h, i: (0, 0))],
        out_specs=pl.BlockSpec((tq, LANES), lambda b, h, i: (b * nsb + i, h)),
        out_shape=jax.ShapeDtypeStruct((t, ATTN_WIDTH), BF16),
        scratch_shapes=[pltpu.VMEM((tq, LANES), BF16), pltpu.VMEM((tq, LANES), BF16),
                        pltpu.VMEM((ATT_SUB, 2, 1, blk), F32),
                        pltpu.VMEM((ATT_SUB, 2, ATT_VROWS, blk), F32),
                        pltpu.VMEM((2, ATT_SUB, 2, blk, blk), F32)],
        compiler_params=_cparams(("parallel", "parallel", "arbitrary")),
        name="diff_attention",
    )(lam, q, k, vt4, g_col)


def _merge_kernel(h_ref, hb_ref, ys_ref, yc_ref, o_ref, wg_ref, bg_ref, wao_ref, wo_ref, lg_ref, lb_ref,
                  h1_ref, h1b_ref, *, tiles_per_seq, front):
    tm = h_ref.shape[0]
    ya = jnp.dot(o_ref[...], wao_ref[...], preferred_element_type=F32)
    g = jax.nn.sigmoid(jnp.dot(hb_ref[...], wg_ref[...], preferred_element_type=F32) + bg_ref[...])
    merged = g[:, :D_MODEL] * ys_ref[...] + g[:, D_MODEL:2 * D_MODEL] * yc_ref[...] + g[:, 2 * D_MODEL:] * ya
    r = DEEPNORM_ALPHA * h_ref[...] + jnp.dot(merged.astype(BF16), wo_ref[...], preferred_element_type=F32)
    y = _layer_norm(r, lg_ref[...], lb_ref[...])
    pos = _seq_pos(pl.program_id(0), tiles_per_seq, tm)
    y = jnp.where(pos >= front, y, 0.0)
    h1_ref[...] = y
    h1b_ref[...] = y.astype(BF16)


def _merge(h, hb, ys, yc, o, wg, bg, wao, wo, lg, lb, *, lp, front):
    t, d = h.shape
    tm = _pick_tile(lp, (256,))
    row = lambda w: pl.BlockSpec((tm, w), lambda i: (i, 0))
    const = lambda a, b: pl.BlockSpec((a, b), lambda i: (0, 0))
    return pl.pallas_call(
        functools.partial(_merge_kernel, tiles_per_seq=lp // tm, front=front),
        grid=(t // tm,),
        in_specs=[row(d), row(d), row(d), row(d), row(ATTN_WIDTH),
                  const(d, 3 * d), const(1, 3 * d), const(ATTN_WIDTH, d), const(d, d), const(1, d), const(1, d)],
        out_specs=[row(d), row(d)],
        out_shape=[jax.ShapeDtypeStruct((t, d), F32), jax.ShapeDtypeStruct((t, d), BF16)],
        compiler_params=_cparams(("parallel",)),
        name="merge_ln1",
    )(h, hb, ys, yc, o, wg, bg.reshape(1, 3 * d), wao, wo, lg.reshape(1, d), lb.reshape(1, d))


def _router_kernel(h_ref, w_ref, b_ref, ids_ref, wts_ref, rank_ref, cnt_ref, base_ref):
    tm = h_ref.shape[0]

    @pl.when(pl.program_id(0) == 0)
    def _():
        base_ref[...] = jnp.zeros_like(base_ref)

    h = h_ref[...]
    w = w_ref[...]
    h_hi = h.astype(BF16)
    h_lo = (h - h_hi.astype(F32)).astype(BF16)
    w_hi = w.astype(BF16)
    w_lo = (w - w_hi.astype(F32)).astype(BF16)
    logits = (jnp.dot(h_hi, w_hi, preferred_element_type=F32) + jnp.dot(h_hi, w_lo, preferred_element_type=F32)
              + jnp.dot(h_lo, w_hi, preferred_element_type=F32)) + b_ref[...]
    lane = lax.broadcasted_iota(jnp.int32, logits.shape, 1).astype(F32)
    work = logits
    sels, vals, firsts = [], [], []
    for _ in range(TOP_K):
        mx = jnp.max(work, axis=1, keepdims=True)
        first = jnp.min(jnp.where(work == mx, lane, float(N_EXPERTS)), axis=1, keepdims=True)
        sel = lane == first
        sels.append(sel)
        vals.append(mx)
        firsts.append(first)
        work = jnp.where(sel, -jnp.inf, work)
    exps = [jnp.exp(v - vals[0]) for v in vals]
    den = exps[0] + exps[1] + exps[2] + exps[3]

    onehot = jnp.zeros_like(logits)
    for sel in sels:
        onehot = onehot + jnp.where(sel, 1.0, 0.0)
    r = lax.broadcasted_iota(jnp.int32, (tm, tm), 0)
    c = lax.broadcasted_iota(jnp.int32, (tm, tm), 1)
    tri = jnp.where(r > c, 1.0, 0.0).astype(BF16)
    before = jnp.dot(tri, onehot.astype(BF16), preferred_element_type=F32) + base_ref[...]
    base_ref[...] = base_ref[...] + jnp.sum(onehot, axis=0, keepdims=True)
    cnt_ref[...] = base_ref[...]

    out_lane = lax.broadcasted_iota(jnp.int32, (tm, LANES), 1)
    ids = jnp.zeros((tm, LANES), F32)
    wts = jnp.zeros((tm, LANES), F32)
    rank = jnp.zeros((tm, LANES), F32)
    for kk in range(TOP_K):
        rk = jnp.sum(jnp.where(sels[kk], before, 0.0), axis=1, keepdims=True)
        ids = jnp.where(out_lane == kk, firsts[kk], ids)
        wts = jnp.where(out_lane == kk, exps[kk] / den, wts)
        rank = jnp.where(out_lane == kk, rk, rank)
    ids_ref[...] = ids.astype(jnp.int32)
    wts_ref[...] = wts
    rank_ref[...] = rank.astype(jnp.int32)


def _router(h, w, b):
    t, d = h.shape
    tm = _pick_tile(t, (512, 256))
    wide = lambda dt: jax.ShapeDtypeStruct((t, LANES), dt)
    row = pl.BlockSpec((tm, LANES), lambda i: (i, 0))
    return pl.pallas_call(
        _router_kernel,
        grid=(t // tm,),
        in_specs=[pl.BlockSpec((tm, d), lambda i: (i, 0)),
                  pl.BlockSpec((d, N_EXPERTS), lambda i: (0, 0)),
                  pl.BlockSpec((1, N_EXPERTS), lambda i: (0, 0))],
        out_specs=[row, row, row, pl.BlockSpec((1, N_EXPERTS), lambda i: (0, 0))],
        out_shape=[wide(jnp.int32), wide(F32), wide(jnp.int32), jax.ShapeDtypeStruct((1, N_EXPERTS), F32)],
        scratch_shapes=[pltpu.VMEM((1, N_EXPERTS), F32)],
        compiler_params=_cparams(("arbitrary",)),
        name="moe_router",
    )(h, w, b.reshape(1, N_EXPERTS))


MOE_TILE = 512
ROUTE_TOK = 512
ROW_TILE = (SUBLANES, LANES)
assert D_MODEL == SUBLANES * LANES
VMEM_LIMIT_EXPERTS = 58 * 1024 * 1024


def _rows_to_tiles(x):
    return x.reshape((x.shape[0],) + ROW_TILE)


def _tiles_to_rows(x):
    return x.reshape(x.shape[0], D_MODEL)


def _dispatch_kernel(last_ref, nu_ref, dest_ref, h_ref, xs_ref, zero_ref, stage_ref, zsem, sems):
    i = pl.program_id(0)
    n = pl.num_programs(0)
    n_tiles = xs_ref.shape[0] // MOE_TILE

    def fill(tile):
        return pltpu.make_async_copy(zero_ref, xs_ref.at[pl.ds(tile * MOE_TILE, MOE_TILE)], zsem)

    @pl.when(i == 0)
    def _():
        zero_ref[...] = jnp.zeros_like(zero_ref)
        for e in range(N_EXPERTS):
            @pl.when(last_ref[e] >= 0)
            def _():
                fill(last_ref[e]).start()

        def start_unused(tile, carry):
            fill(tile).start()
            return carry

        def wait_unused(tile, carry):
            fill(tile).wait()
            return carry

        lax.fori_loop(nu_ref[0], n_tiles, start_unused, 0)
        for e in range(N_EXPERTS):
            @pl.when(last_ref[e] >= 0)
            def _():
                fill(last_ref[e]).wait()
        lax.fori_loop(nu_ref[0], n_tiles, wait_unused, 0)

    slot = lax.rem(i, 2)
    stage_ref[slot] = _rows_to_tiles(h_ref[...])

    def issue(r8, carry):
        base = pl.multiple_of(r8 * SUBLANES, SUBLANES)
        for rr in range(SUBLANES):
            for kk in range(TOP_K):
                d = dest_ref[(base + rr) * TOP_K + kk]
                pltpu.make_async_copy(stage_ref.at[slot, pl.ds(base + rr, 1)], xs_ref.at[pl.ds(d, 1)],
                                      sems.at[slot]).start(priority=kk % 2)
        return carry

    lax.fori_loop(0, ROUTE_TOK // SUBLANES, issue, 0)

    def drain(s):
        for _ in range(TOP_K):
            pltpu.make_async_copy(stage_ref.at[s], xs_ref.at[pl.ds(0, ROUTE_TOK)], sems.at[s]).wait()

    @pl.when(i > 0)
    def _():
        drain(1 - slot)

    @pl.when(i == n - 1)
    def _():
        drain(slot)


def _dispatch(h, dest_flat, last_tile, n_used, n_rows):
    t, d = h.shape
    return pl.pallas_call(
        _dispatch_kernel,
        grid_spec=pltpu.PrefetchScalarGridSpec(
            num_scalar_prefetch=2,
            grid=(t // ROUTE_TOK,),
            in_specs=[pl.BlockSpec((ROUTE_TOK * TOP_K,), lambda i, last, nu: (i,), memory_space=pltpu.SMEM),
                      pl.BlockSpec((ROUTE_TOK, d), lambda i, last, nu: (i, 0))],
            out_specs=pl.BlockSpec(memory_space=pl.ANY),
            scratch_shapes=[pltpu.VMEM((MOE_TILE,) + ROW_TILE, F32), pltpu.VMEM((2, ROUTE_TOK) + ROW_TILE, F32),
                            pltpu.SemaphoreType.DMA(()), pltpu.SemaphoreType.DMA((2,))]),
        out_shape=jax.ShapeDtypeStruct((n_rows,) + ROW_TILE, F32),
        compiler_params=_cparams(("arbitrary",)),
        name="moe_dispatch",
    )(last_tile, n_used, dest_flat, h)


def _experts_kernel(te_ref, nu_ref, xs_ref, wgu_ref, bgu_ref, wd_ref, bd_ref, y_ref, wgu_b_ref, wd_b_ref):
    i = pl.program_id(0)

    @pl.when(jnp.logical_or(i == 0, te_ref[i] != te_ref[jnp.maximum(i - 1, 0)]))
    def _():
        wgu_b_ref[...] = wgu_ref[0].astype(BF16)
        wd_b_ref[...] = wd_ref[0].astype(BF16)

    @pl.when(i >= nu_ref[0])
    def _():
        y_ref[...] = jnp.zeros_like(y_ref)

    @pl.when(i < nu_ref[0])
    def _():
        x = _tiles_to_rows(xs_ref[...]).astype(BF16)
        gu = jnp.dot(x, wgu_b_ref[...], preferred_element_type=F32) + bgu_ref[0]
        gate = jnp.minimum(gu[:, :D_FF], SWIGLU_LIMIT)
        up = jnp.clip(gu[:, D_FF:], -SWIGLU_LIMIT, SWIGLU_LIMIT)
        hid = (up + 1.0) * gate * jax.nn.sigmoid(SWIGLU_ALPHA * gate)
        y_ref[...] = _rows_to_tiles(jnp.dot(hid.astype(BF16), wd_b_ref[...], preferred_element_type=F32)
                                    + bd_ref[0])


def _experts(tile_expert, n_used, xs, wgu, bgu, wd, bd, *, layer):
    n_rows = xs.shape[0]
    d = D_MODEL
    n_exp = wgu.shape[1]
    rows = lambda i, te, nu: (jnp.minimum(i, nu[0] - 1), 0, 0)
    per_e = lambda i, te, nu: (te[i], 0, 0)
    per_le = lambda i, te, nu: (layer, te[i], 0, 0)
    return pl.pallas_call(
        _experts_kernel,
        grid_spec=pltpu.PrefetchScalarGridSpec(
            num_scalar_prefetch=2,
            grid=(n_rows // MOE_TILE,),
            in_specs=[pl.BlockSpec((MOE_TILE,) + ROW_TILE, rows),
                      pl.BlockSpec((None, 1, d, 2 * D_FF), per_le),
                      pl.BlockSpec((1, 1, 2 * D_FF), per_e),
                      pl.BlockSpec((None, 1, D_FF, d), per_le),
                      pl.BlockSpec((1, 1, d), per_e)],
            out_specs=pl.BlockSpec((MOE_TILE,) + ROW_TILE, lambda i, te, nu: (i, 0, 0)),
            scratch_shapes=[pltpu.VMEM((d, 2 * D_FF), BF16), pltpu.VMEM((D_FF, d), BF16)]),
        out_shape=jax.ShapeDtypeStruct((n_rows,) + ROW_TILE, F32),
        compiler_params=pltpu.CompilerParams(dimension_semantics=("arbitrary",),
                                             vmem_limit_bytes=VMEM_LIMIT_EXPERTS),
        name="moe_experts",
    )(tile_expert, n_used, xs, wgu, bgu.reshape(n_exp, 1, 2 * D_FF), wd, bd.reshape(n_exp, 1, d))


COMBINE_TOK = 256


def _combine_kernel(dcur_ref, dnext_ref, w_ref, h_ref, y_ref, lg_ref, lb_ref, h2_ref, h2b_ref,
                    ybuf_ref, sems, *, tiles_per_seq, front):
    i = pl.program_id(0)
    n = pl.num_programs(0)
    slot = lax.rem(i, 2)

    def issue(d_ref, s):
        def body(r8, carry):
            base = pl.multiple_of(r8 * SUBLANES, SUBLANES)
            for rr in range(SUBLANES):
                for kk in range(TOP_K):
                    d = d_ref[(base + rr) * TOP_K + kk]
                    pltpu.make_async_copy(y_ref.at[pl.ds(d, 1)], ybuf_ref.at[s, kk, pl.ds(base + rr, 1)],
                                          sems.at[s]).start(priority=kk % 2)
            return carry
        lax.fori_loop(0, COMBINE_TOK // SUBLANES, body, 0)

    @pl.when(i == 0)
    def _():
        issue(dcur_ref, 0)

    @pl.when(i + 1 < n)
    def _():
        issue(dnext_ref, 1 - slot)

    for kk in range(TOP_K):
        pltpu.make_async_copy(y_ref.at[pl.ds(0, COMBINE_TOK)], ybuf_ref.at[slot, kk], sems.at[slot]).wait()

    w = w_ref[...]
    acc = w[:, 0:1] * _tiles_to_rows(ybuf_ref[slot, 0])
    for kk in range(1, TOP_K):
        acc = acc + w[:, kk:kk + 1] * _tiles_to_rows(ybuf_ref[slot, kk])
    y = _layer_norm(DEEPNORM_ALPHA * h_ref[...] + acc, lg_ref[...], lb_ref[...])
    pos = _seq_pos(i, tiles_per_seq, COMBINE_TOK)
    y = jnp.where(pos >= front, y, 0.0)
    h2_ref[...] = y
    h2b_ref[...] = y.astype(BF16)


def _combine(dest_flat, wts, h, y, lg, lb, *, lp, front, drop_prefix=False):
    t, d = h.shape
    n = t // COMBINE_TOK
    blk = COMBINE_TOK * TOP_K
    row = lambda w: pl.BlockSpec((COMBINE_TOK, w), lambda i: (i, 0))
    const = pl.BlockSpec((1, d), lambda i: (0, 0))
    out_rows, out_f32 = t, row(d)
    if drop_prefix:
        tps = lp // COMBINE_TOK
        skip = (front + N_META) // COMBINE_TOK
        out_rows = t - (t // lp) * (front + N_META)
        out_f32 = pl.BlockSpec((COMBINE_TOK, d),
                               lambda i: ((i // tps) * (tps - skip) + jnp.maximum(i % tps - skip, 0), 0))
    return pl.pallas_call(
        functools.partial(_combine_kernel, tiles_per_seq=lp // COMBINE_TOK, front=front),
        grid=(n,),
        in_specs=[pl.BlockSpec((blk,), lambda i: (i,), memory_space=pltpu.SMEM),
                  pl.BlockSpec((blk,), lambda i: (jnp.minimum(i + 1, n - 1),), memory_space=pltpu.SMEM),
                  row(LANES), row(d), pl.BlockSpec(memory_space=pl.ANY), const, const],
        out_specs=[out_f32, row(d)],
        out_shape=[jax.ShapeDtypeStruct((out_rows, d), F32), jax.ShapeDtypeStruct((t, d), BF16)],
        scratch_shapes=[pltpu.VMEM((2, TOP_K, COMBINE_TOK) + ROW_TILE, F32), pltpu.SemaphoreType.DMA((2,))],
        compiler_params=_cparams(("arbitrary",)),
        name="moe_combine_ln2",
    )(dest_flat, dest_flat, wts, h, y, lg.reshape(1, d), lb.reshape(1, d))


def _routing_plan(ids, rank, counts, n_tiles):
    n_exp = counts.shape[0]
    padded = (counts + MOE_TILE - 1) // MOE_TILE * MOE_TILE
    ends = jnp.cumsum(padded)
    starts = ends - padded
    dest = (jnp.take(starts, ids) + rank).reshape(-1)
    n_used = (ends[-1] // MOE_TILE).reshape(1)
    tile_start = jnp.arange(n_tiles, dtype=jnp.int32) * MOE_TILE
    tile_expert = jnp.sum((tile_start[:, None] >= ends[None, :]).astype(jnp.int32), axis=1)
    last_used_expert = jnp.max(jnp.where(padded > 0, jnp.arange(n_exp, dtype=jnp.int32), 0))
    tile_expert = jnp.minimum(tile_expert, last_used_expert)
    last_tile = jnp.where(padded > 0, ends // MOE_TILE - 1, -1)
    return dest.astype(jnp.int32), tile_expert.astype(jnp.int32), n_used.astype(jnp.int32), last_tile.astype(jnp.int32)


def _rope_tables(lp, front):
    pos = (jnp.arange(lp, dtype=jnp.int32) - front).astype(F32)
    inv = ROPE_THETA ** (-jnp.arange(0, ATTN_HEAD_DIM, 2, dtype=F32) / ATTN_HEAD_DIM)
    ang = pos[:, None] * inv[None, :]
    ang = jnp.concatenate([ang, ang, ang, ang], axis=-1)
    return jnp.cos(ang), jnp.sin(ang)


def kernel(x, meta_tokens, ln_in_g, ln_in_b, w_in, ssm_lambda_re, ssm_lambda_im, ssm_log_dt, ssm_b_re, ssm_b_im, ssm_c_re, ssm_c_im, ssm_d, ssm_w_glu, ssm_w_out, conv_w, conv_w_out, attn_lambda_q1, attn_lambda_k1, attn_lambda_q2, attn_lambda_k2, attn_subln_g, attn_w_out, gate_w, gate_b, w_o, ln1_g, ln1_b, router_w, router_b, expert_w_gu, expert_b_gu, expert_w_down, expert_b_down, ln2_g, ln2_b):
    bsz, seq, d = x.shape
    assert d == D_MODEL and seq % ATT_BLOCK == 0
    lp = seq + ATT_BLOCK
    front = ATT_BLOCK - N_META
    t = bsz * lp

    h, hb = _ln_in(x.reshape(bsz * seq, d), meta_tokens.astype(x.dtype), ln_in_g, ln_in_b, lp=lp, front=front)
    cos, sin = _rope_tables(lp, front)

    s3 = SSM_WIDTH + 3 * CONV_WIDTH
    s5 = s3 + 2 * ATTN_WIDTH
    for l in range(DEPTH):
        w_in_b = w_in[l].astype(BF16)
        u, cb, z = _proj_sc(hb, w_in_b[:, :s3])
        q, k = _proj_qk(hb, w_in_b[:, s3:s5], cos, sin, lp=lp)
        vt = _proj_vt(hb, w_in_b[:, s5:].T)
        bdb, bdc, pw = _ssm_tables(ssm_lambda_re[l], ssm_lambda_im[l], ssm_log_dt[l], ssm_b_re[l], ssm_b_im[l],
                                   ssm_c_re[l], ssm_c_im[l])
        ys = _ssm_branch(u, bdb, bdc, pw, ssm_d[l].astype(F32), ssm_w_glu[l].astype(BF16),
                         ssm_w_out[l].astype(BF16), bsz=bsz, lp=lp)
        yc = _conv_branch(cb, z, conv_w[l].astype(F32), conv_w_out[l].astype(BF16))

        lam_init = 0.8 - 0.6 * math.exp(-0.3 * l)
        lam = (jnp.exp(jnp.sum(attn_lambda_q1[l].astype(F32) * attn_lambda_k1[l].astype(F32)))
               - jnp.exp(jnp.sum(attn_lambda_q2[l].astype(F32) * attn_lambda_k2[l].astype(F32)))
               + lam_init).reshape(1)
        g_col = jnp.broadcast_to(attn_subln_g[l].astype(F32)[:, None], (ATTN_VALUE_DIM, ATT_BLOCK))
        o = _attention(lam, q, k, vt, g_col, bsz=bsz, lp=lp, front=front, lam_init=lam_init)

        h, hb = _merge(h, hb, ys, yc, o, gate_w[l].astype(BF16), gate_b[l].astype(F32),
                       attn_w_out[l].astype(BF16), w_o[l].astype(BF16), ln1_g[l], ln1_b[l], lp=lp, front=front)

        ids, wts, rank, counts = _router(h, router_w[l].astype(F32), router_b[l].astype(F32))
        n_tiles = t * TOP_K // MOE_TILE + N_EXPERTS
        dest, tile_expert, n_used, last_tile = _routing_plan(ids[:, :TOP_K], rank[:, :TOP_K],
                                                             counts[0].astype(jnp.int32), n_tiles)
        xs = _dispatch(h, dest, last_tile, n_used, n_tiles * MOE_TILE)
        y = _experts(tile_expert, n_used, xs, expert_w_gu.astype(F32), expert_b_gu[l].astype(F32),
                     expert_w_down.astype(F32), expert_b_down[l].astype(F32), layer=l)
        h, hb = _combine(dest, wts, h, y, ln2_g[l], ln2_b[l], lp=lp, front=front, drop_prefix=(l == DEPTH - 1))

    return h.reshape(bsz, seq, d)
```
